```python
import math, functools
import jax, jax.numpy as jnp
from jax import lax
import numpy as np

D_MODEL = 1024
BATCH = 8
SEQ = 2048
DEPTH = 2
DEC_BATCH = 32
DEC_SEQ = 4
PAST_LEN = 16384
PAGE_SIZE = 128

A_HEADS = 4
A_QK_DIM = D_MODEL // 16
A_V_DIM = 2 * A_QK_DIM
A_QK_WIDTH = A_HEADS * 2 * A_QK_DIM
A_WIDTH = A_HEADS * A_V_DIM
REL_BUCKETS = 32
REL_MAX_DIST = 128
Q_BLOCK = 128
B_GROUPS = 4
B_WIDTH = D_MODEL // 4
B_CHUNK = 128
C_GROUPS = 4
C_WIDTH = D_MODEL // 4
C_GROUP_DIM = C_WIDTH // C_GROUPS
POOL_WINDOWS = (2, 4, 8, 16)
POOL_BUF = max(POOL_WINDOWS) - 1
D_HEADS = 4
D_WIDTH = D_MODEL // 4
D_HEAD_DIM = D_WIDTH // D_HEADS
D_CHUNK = 64
D_FF = 4 * D_MODEL
N_BRANCH = 4
EPS = 1e-6
IN_SIZES = (A_QK_WIDTH, A_QK_WIDTH, A_WIDTH, B_WIDTH, B_WIDTH, C_WIDTH,
            D_WIDTH, D_WIDTH, D_WIDTH, D_HEADS, D_HEADS, D_WIDTH, N_BRANCH * D_MODEL)
IN_WIDTH = sum(IN_SIZES)

kernel_name = 'gated_parallel_hybrid_decoder_step'


def _rmsnorm(x, g):
    xf = x.astype(jnp.float32)
    y = xf * lax.rsqrt(jnp.mean(xf * xf, axis=-1, keepdims=True) + EPS)
    return (y * g.astype(jnp.float32)).astype(x.dtype)


def _layernorm(x, g, b):
    xf = x.astype(jnp.float32)
    xc = xf - jnp.mean(xf, axis=-1, keepdims=True)
    y = xc * lax.rsqrt(jnp.mean(xc * xc, axis=-1, keepdims=True) + EPS)
    return (y * g.astype(jnp.float32) + b.astype(jnp.float32)).astype(x.dtype)


def _split_in(z):
    idx = []
    acc = 0
    for s in IN_SIZES[:-1]:
        acc += s
        idx.append(acc)
    return jnp.split(z, idx, axis=-1)


def _rel_bias(table, qpos, kpos):
    n = jnp.maximum(qpos[:, None] - kpos[None, :], 0)
    max_exact = REL_BUCKETS // 2
    large = max_exact + (jnp.log(jnp.maximum(n, 1).astype(jnp.float32) / max_exact)
                         / math.log(REL_MAX_DIST / max_exact) * (REL_BUCKETS - max_exact)).astype(jnp.int32)
    bucket = jnp.where(n < max_exact, n, jnp.minimum(large, REL_BUCKETS - 1))
    return jnp.moveaxis(table[bucket].astype(jnp.float32), -1, 0)


def _diff_probs(s, lam):
    return jax.nn.softmax(s[0], axis=-1) - lam * jax.nn.softmax(s[1], axis=-1)


def _diff_attn_prompt(q, k, v, lam, table):
    B, S = q.shape[:2]
    nb = S // Q_BLOCK
    scale = A_QK_DIM ** -0.5
    kpos = jnp.arange(S)
    qb = jnp.moveaxis(q.reshape(B, nb, Q_BLOCK, A_HEADS, 2, A_QK_DIM), 1, 0)

    def one_block(args):
        qi, i = args
        qpos = i * Q_BLOCK + jnp.arange(Q_BLOCK)
        s = jnp.einsum('bqhmd,bkhmd->mbhqk', qi, k).astype(jnp.float32) * scale
        s = s + _rel_bias(table, qpos, kpos)
        s = jnp.where(kpos[None, :] <= qpos[:, None], s, -jnp.inf)
        p = _diff_probs(s, lam)
        return jnp.einsum('bhqk,bkhd->bqhd', p.astype(v.dtype), v)

    out = lax.map(one_block, (qb, jnp.arange(nb)))
    return jnp.moveaxis(out, 0, 1).reshape(B, S, A_HEADS, A_V_DIM)


def _diff_attn_sample(q, k, v, lam, table, cache_k, cache_v, page_table, layer):
    Bd, T = q.shape[:2]
    past = page_table.shape[1] * cache_k.shape[2]
    kp = cache_k[layer, page_table].reshape(Bd, past, A_HEADS, 2, A_QK_DIM).astype(q.dtype)
    vp = cache_v[layer, page_table].reshape(Bd, past, A_HEADS, A_V_DIM).astype(v.dtype)
    scale = A_QK_DIM ** -0.5
    qpos = past + jnp.arange(T)
    s_past = jnp.einsum('bqhmd,bkhmd->mbhqk', q, kp).astype(jnp.float32) * scale
    s_past = s_past + _rel_bias(table, qpos, jnp.arange(past))
    s_new = jnp.einsum('bqhmd,bkhmd->mbhqk', q, k).astype(jnp.float32) * scale
    s_new = s_new + _rel_bias(table, qpos, qpos)
    s_new = jnp.where(qpos[None, :] <= qpos[:, None], s_new, -jnp.inf)
    p = _diff_probs(jnp.concatenate([s_past, s_new], axis=-1), lam)
    return (jnp.einsum('bhqk,bkhd->bqhd', p[..., :past].astype(v.dtype), vp)
            + jnp.einsum('bhqk,bkhd->bqhd', p[..., past:].astype(v.dtype), v))


def _chunk_mlp(u, v, w_s, b_s):
    B, T, W = v.shape
    L = min(T, B_CHUNK)
    vr = v.reshape(B, T // L, L, B_GROUPS, W // B_GROUPS)
    w = jnp.tril(w_s[:, :L, :L]).astype(v.dtype)
    bias = jnp.transpose(b_s[:, :L]).astype(v.dtype)[:, :, None]
    mixed = jnp.einsum('gts,bcsgd->bctgd', w, vr) + bias
    return u * mixed.reshape(B, T, W)


def _pool_mix(xc, prefix, pos0, w_lin, scale):
    B, T, W = xc.shape
    P = prefix.shape[1]
    xx = jnp.concatenate([prefix.astype(xc.dtype), xc], axis=1)
    xf = xx.astype(jnp.float32)
    cs = jnp.concatenate([jnp.zeros((B, 1, W), jnp.float32), jnp.cumsum(xf, axis=1)], axis=1)
    pos = pos0 + jnp.arange(T)
    end = P + jnp.arange(T) + 1
    groups = []
    for g, w in enumerate(POOL_WINDOWS):
        sl = slice(g * C_GROUP_DIM, (g + 1) * C_GROUP_DIM)
        csg = cs[:, :, sl]
        start = jnp.maximum(end - w, 0)
        cnt = jnp.minimum(pos + 1, w).astype(jnp.float32)
        mean = (csg[:, end] - csg[:, start]) / cnt[None, :, None]
        groups.append(mean - xf[:, P:, sl])
    d = jnp.stack(groups, axis=2)
    y = jnp.einsum('btgd,gde->btge', d, w_lin.astype(jnp.float32)).reshape(B, T, W)
    y = y * scale.astype(jnp.float32)
    return y.astype(xc.dtype), xx[:, -POOL_BUF:]


def _mlstm(q, k, v, ig, lf, C0, n0, m0):
    B, T, H, Dh = q.shape
    L = math.gcd(T, D_CHUNK)
    nc = T // L
    k = k * (Dh ** -0.5)
    tril = jnp.tril(jnp.ones((L, L), dtype=bool))

    def to_chunks(a):
        return jnp.moveaxis(a.reshape((B, nc, L) + a.shape[2:]), 1, 0)

    def step(carry, xs):
        C, n, m = carry
        qc, kc, vc, ic, lfc = xs
        b = jnp.cumsum(lfc, axis=1)
        Dm = b[:, :, None, :] - b[:, None, :, :] + ic[:, None, :, :]
        Dm = jnp.where(tril[None, :, :, None], Dm, -jnp.inf)
        inter = b + m[:, None, :]
        mt = jnp.maximum(inter, jnp.max(Dm, axis=2))
        qk = jnp.einsum('bthd,bshd->btsh', qc, kc) * jnp.exp(Dm - mt[:, :, None, :])
        winter = jnp.exp(inter - mt)
        num = (winter[..., None] * jnp.einsum('bthd,bhde->bthe', qc, C)
               + jnp.einsum('btsh,bshe->bthe', qk, vc))
        den = winter * jnp.einsum('bthd,bhd->bth', qc, n) + jnp.sum(qk, axis=2)
        h = num / jnp.maximum(jnp.abs(den), jnp.exp(-mt))[..., None]
        bL = b[:, -1]
        a = bL[:, None, :] - b + ic
        m_new = jnp.maximum(bL + m, jnp.max(a, axis=1))
        ws = jnp.exp(a - m_new[:, None, :])
        decay = jnp.exp(bL + m - m_new)
        C_new = decay[..., None, None] * C + jnp.einsum('bsh,bshd,bshe->bhde', ws, kc, vc)
        n_new = decay[..., None] * n + jnp.einsum('bsh,bshd->bhd', ws, kc)
        return (C_new, n_new, m_new), h

    (C, n, m), h = lax.scan(step, (C0, n0, m0), tuple(to_chunks(a) for a in (q, k, v, ig, lf)))
    return jnp.moveaxis(h, 0, 1).reshape(B, T, H, Dh), C, n, m


def _layer(x, l, attend, pool_prefix, pos0, C0, n0, m0, p):
    f32 = jnp.float32
    B, T, _ = x.shape
    h = _rmsnorm(x, p['norm1_g'][l])
    (aq, ak, av, bu, bv, cx, dq, dk, dv, di, df, d_o, gz) = _split_in(h @ p['w_in'][l])
    q = _rmsnorm(aq.reshape(B, T, A_HEADS, 2, A_QK_DIM), p['q_norm_g'][l])
    k = _rmsnorm(ak.reshape(B, T, A_HEADS, 2, A_QK_DIM), p['k_norm_g'][l])
    v = av.reshape(B, T, A_HEADS, A_V_DIM)
    lam_init = 0.8 - 0.6 * math.exp(-0.3 * l)
    lam = (jnp.exp(jnp.sum(p['lam_q1'][l] * p['lam_k1'][l]).astype(f32))
           - jnp.exp(jnp.sum(p['lam_q2'][l] * p['lam_k2'][l]).astype(f32)) + lam_init)
    ya = attend(q, k, v, lam)
    ya = (_rmsnorm(ya, p['subln_g'][l]) * (1.0 - lam_init)).reshape(B, T, A_WIDTH)
    vb = _layernorm(jax.nn.gelu(bv), p['b_ln_g'][l], p['b_ln_b'][l])
    yb = _chunk_mlp(jax.nn.gelu(bu), vb, p['b_ws'][l], p['b_bias'][l])
    yc, pool_buf = _pool_mix(cx, pool_prefix, pos0, p['c_lin'][l], p['c_scale'][l])
    hs = (B, T, D_HEADS, D_HEAD_DIM)
    ig = (di + p['d_i_bias'][l]).astype(f32)
    lf = jax.nn.log_sigmoid((df + p['d_f_bias'][l]).astype(f32))
    hd, C, n, m = _mlstm(dq.reshape(hs).astype(f32), dk.reshape(hs).astype(f32),
                         dv.reshape(hs).astype(f32), ig, lf,
                         C0.astype(f32), n0.astype(f32), m0.astype(f32))
    yd = (_rmsnorm(hd, p['d_norm_g'][l]).astype(x.dtype)
          * jax.nn.sigmoid(d_o).reshape(hs)).reshape(B, T, D_WIDTH)
    g = jax.nn.sigmoid(gz).reshape(B, T, N_BRANCH, D_MODEL)
    merged = (g[:, :, 0] * (ya @ p['w_pa'][l]) + g[:, :, 1] * (yb @ p['w_pb'][l])
              + g[:, :, 2] * (yc @ p['w_pc'][l]) + g[:, :, 3] * (yd @ p['w_pd'][l]))
    x = x + merged @ p['w_out'][l]
    h2 = _rmsnorm(x, p['norm2_g'][l])
    x = x + jnp.square(jax.nn.relu(h2 @ p['w_ff1'][l])) @ p['w_ff2'][l]
    dt = x.dtype
    return (x, k.reshape(B, T, A_HEADS, 2 * A_QK_DIM), v, vb, pool_buf,
            C.astype(dt), n.astype(dt), m.astype(dt))


def setup_inputs(seed: int = 0) -> dict:
    key = jax.random.key(seed)
    ks = iter(jax.random.split(key, 48))

    def nrm(shape, s=1.0):
        return s * jax.random.normal(next(ks), shape, jnp.float32)

    n_pages = PAST_LEN // PAGE_SIZE
    n_used = DEC_BATCH * n_pages
    n_phys = n_used + n_used // 4
    page_table = jax.random.permutation(next(ks), n_phys)[:n_used].reshape(DEC_BATCH, n_pages).astype(jnp.int32)
    return {
        'x_prompt': nrm((BATCH, SEQ, D_MODEL)),
        'x_sample': nrm((DEC_BATCH, DEC_SEQ, D_MODEL)),
        'cache_k': nrm((DEPTH, n_phys, PAGE_SIZE, A_HEADS, 2 * A_QK_DIM)),
        'cache_v': nrm((DEPTH, n_phys, PAGE_SIZE, A_HEADS, A_V_DIM)),
        'page_table': page_table,
        'state_pool': nrm((DEPTH, DEC_BATCH, POOL_BUF, C_WIDTH)),
        'state_C': nrm((DEPTH, DEC_BATCH, D_HEADS, D_HEAD_DIM, D_HEAD_DIM), 0.3),
        'state_n': nrm((DEPTH, DEC_BATCH, D_HEADS, D_HEAD_DIM), 0.3),
        'state_m': nrm((DEPTH, DEC_BATCH, D_HEADS), 0.5),
        'rel_bias': nrm((REL_BUCKETS, A_HEADS), 0.5),
        'norm1_g': 1.0 + nrm((DEPTH, D_MODEL), 0.02),
        'norm2_g': 1.0 + nrm((DEPTH, D_MODEL), 0.02),
        'w_in': nrm((DEPTH, D_MODEL, IN_WIDTH), D_MODEL ** -0.5),
        'q_norm_g': 1.0 + nrm((DEPTH, A_QK_DIM), 0.02),
        'k_norm_g': 1.0 + nrm((DEPTH, A_QK_DIM), 0.02),
        'lam_q1': nrm((DEPTH, A_QK_DIM), 0.1),
        'lam_k1': nrm((DEPTH, A_QK_DIM), 0.1),
        'lam_q2': nrm((DEPTH, A_QK_DIM), 0.1),
        'lam_k2': nrm((DEPTH, A_QK_DIM), 0.1),
        'subln_g': 1.0 + nrm((DEPTH, A_V_DIM), 0.02),
        'b_ln_g': 1.0 + nrm((DEPTH, B_WIDTH), 0.02),
        'b_ln_b': nrm((DEPTH, B_WIDTH), 0.02),
        'b_ws': nrm((DEPTH, B_GROUPS, B_CHUNK, B_CHUNK), B_CHUNK ** -0.5),
        'b_bias': 1.0 + nrm((DEPTH, B_GROUPS, B_CHUNK), 0.02),
        'c_lin': nrm((DEPTH, C_GROUPS, C_GROUP_DIM, C_GROUP_DIM), C_GROUP_DIM ** -0.5),
        'c_scale': 1.0 + nrm((DEPTH, C_WIDTH), 0.02),
        'd_i_bias': nrm((DEPTH, D_HEADS), 0.1),
        'd_f_bias': jnp.linspace(3.0, 6.0, D_HEADS, dtype=jnp.float32)[None, :] + nrm((DEPTH, D_HEADS), 0.1),
        'd_norm_g': 1.0 + nrm((DEPTH, D_HEAD_DIM), 0.02),
        'w_pa': nrm((DEPTH, A_WIDTH, D_MODEL), A_WIDTH ** -0.5),
        'w_pb': nrm((DEPTH, B_WIDTH, D_MODEL), B_WIDTH ** -0.5),
        'w_pc': nrm((DEPTH, C_WIDTH, D_MODEL), C_WIDTH ** -0.5),
        'w_pd': nrm((DEPTH, D_WIDTH, D_MODEL), D_WIDTH ** -0.5),
        'w_out': nrm((DEPTH, D_MODEL, D_MODEL), D_MODEL ** -0.5),
        'w_ff1': nrm((DEPTH, D_MODEL, D_FF), D_MODEL ** -0.5),
        'w_ff2': nrm((DEPTH, D_FF, D_MODEL), D_FF ** -0.5),
    }


def reference(x_prompt, x_sample, cache_k, cache_v, page_table, state_pool, state_C, state_n, state_m,
              rel_bias, norm1_g, norm2_g, w_in, q_norm_g, k_norm_g, lam_q1, lam_k1, lam_q2, lam_k2,
              subln_g, b_ln_g, b_ln_b, b_ws, b_bias, c_lin, c_scale, d_i_bias, d_f_bias, d_norm_g,
              w_pa, w_pb, w_pc, w_pd, w_out, w_ff1, w_ff2):
    p = dict(norm1_g=norm1_g, norm2_g=norm2_g, w_in=w_in, q_norm_g=q_norm_g, k_norm_g=k_norm_g,
             lam_q1=lam_q1, lam_k1=lam_k1, lam_q2=lam_q2, lam_k2=lam_k2, subln_g=subln_g,
             b_ln_g=b_ln_g, b_ln_b=b_ln_b, b_ws=b_ws, b_bias=b_bias, c_lin=c_lin, c_scale=c_scale,
             d_i_bias=d_i_bias, d_f_bias=d_f_bias, d_norm_g=d_norm_g, w_pa=w_pa, w_pb=w_pb,
             w_pc=w_pc, w_pd=w_pd, w_out=w_out, w_ff1=w_ff1, w_ff2=w_ff2)
    f32 = jnp.float32
    Bp = x_prompt.shape[0]
    attend_prompt = functools.partial(_diff_attn_prompt, table=rel_bias)
    xp = x_prompt
    kp_l, vp_l, poolp_l, Cp_l, np_l, mp_l = [], [], [], [], [], []
    for l in range(DEPTH):
        xp, k, v, _, buf, C, n, m = _layer(
            xp, l, attend_prompt, jnp.zeros((Bp, 0, C_WIDTH), xp.dtype), 0,
            jnp.zeros((Bp, D_HEADS, D_HEAD_DIM, D_HEAD_DIM), f32),
            jnp.zeros((Bp, D_HEADS, D_HEAD_DIM), f32), jnp.zeros((Bp, D_HEADS), f32), p)
        kp_l.append(k); vp_l.append(v); poolp_l.append(buf)
        Cp_l.append(C); np_l.append(n); mp_l.append(m)
    past = page_table.shape[1] * cache_k.shape[2]
    xs = x_sample
    ks_l, vs_l, vb_l, pools_l, Cs_l, ns_l, ms_l = [], [], [], [], [], [], []
    for l in range(DEPTH):
        attend_sample = functools.partial(_diff_attn_sample, table=rel_bias, cache_k=cache_k,
                                          cache_v=cache_v, page_table=page_table, layer=l)
        xs, k, v, vb, buf, C, n, m = _layer(xs, l, attend_sample, state_pool[l], past,
                                            state_C[l], state_n[l], state_m[l], p)
        ks_l.append(k); vs_l.append(v); vb_l.append(vb); pools_l.append(buf)
        Cs_l.append(C); ns_l.append(n); ms_l.append(m)
    return (xp, xs,
            jnp.stack(kp_l), jnp.stack(vp_l), jnp.stack(ks_l), jnp.stack(vs_l), jnp.stack(vb_l),
            jnp.stack(poolp_l), jnp.stack(pools_l),
            jnp.stack(Cp_l), jnp.stack(np_l), jnp.stack(mp_l),
            jnp.stack(Cs_l), jnp.stack(ns_l), jnp.stack(ms_l))
```

```python
import functools
import math

import jax
import jax.numpy as jnp
from jax import lax
from jax.experimental import pallas as pl
from jax.experimental.pallas import tpu as pltpu

F32 = jnp.float32
BF16 = jnp.bfloat16
NEG_INF = float("-inf")

D_MODEL = 1024
A_HEADS = 4
A_QK_DIM = 64
A_V_DIM = 128
A_WIDTH = 512
REL_BUCKETS = 32
REL_MAX_DIST = 128
PAGE_SIZE = 128
B_GROUPS = 4
B_WIDTH = 256
B_CHUNK = 128
C_WIDTH = 256
C_GROUP_DIM = 64
POOL_WINDOWS = (2, 4, 8, 16)
POOL_BUF = 15
D_HEADS = 4
D_WIDTH = 256
D_HEAD_DIM = 64
D_FF = 4096
N_BRANCH = 4
EPS = 1e-6

ZR_WIDTH = 3584
COL_AQ, COL_AK, COL_AV = 0, 512, 1024
COL_BU, COL_BV, COL_CX = 1536, 1792, 2048
COL_DQ, COL_DK, COL_DV, COL_DO, COL_DG = 2304, 2560, 2816, 3072, 3328
GZ_WIDTH = N_BRANCH * D_MODEL

ATT_TQ = 256
SAMPLE_PAGES = 8
VMEM_LIMIT = 56 * 1024 * 1024


def _cparams(n_axes):
    return pltpu.CompilerParams(dimension_semantics=("arbitrary",) * n_axes,
                                vmem_limit_bytes=VMEM_LIMIT)


def _dot(a, b):
    return jnp.dot(a, b, preferred_element_type=F32)


def _dot_nt(a, b):
    return lax.dot_general(a, b, (((1,), (1,)), ((), ())), preferred_element_type=F32)


def _dot_tn(a, b):
    return lax.dot_general(a, b, (((0,), (0,)), ((), ())), preferred_element_type=F32)


def _rms_kernel(x_ref, g_ref, o_ref):
    x = x_ref[...]
    y = x * lax.rsqrt(jnp.mean(x * x, axis=-1, keepdims=True) + EPS)
    o_ref[...] = (y * g_ref[...]).astype(o_ref.dtype)


def _rms_cast(x, g):
    m = x.shape[0]
    tm = min(m, 512)
    return pl.pallas_call(
        _rms_kernel,
        grid=(m // tm,),
        in_specs=[pl.BlockSpec((tm, D_MODEL), lambda i: (i, 0)),
                  pl.BlockSpec((1, D_MODEL), lambda i: (0, 0))],
        out_specs=pl.BlockSpec((tm, D_MODEL), lambda i: (i, 0)),
        out_shape=jax.ShapeDtypeStruct((m, D_MODEL), BF16),
        compiler_params=_cparams(1),
        name="rms_cast",
    )(x, g.reshape(1, D_MODEL))


def _gate_kernel(h_ref, w_ref, o_ref):
    o_ref[...] = jax.nn.sigmoid(_dot(h_ref[...], w_ref[...])).astype(o_ref.dtype)


def _gate_proj(h, wg):
    m = h.shape[0]
    tm, tn = min(m, 1024), 512
    return pl.pallas_call(
        _gate_kernel,
        grid=(m // tm, GZ_WIDTH // tn),
        in_specs=[pl.BlockSpec((tm, D_MODEL), lambda i, j: (i, 0)),
                  pl.BlockSpec((D_MODEL, tn), lambda i, j: (0, j))],
        out_specs=pl.BlockSpec((tm, tn), lambda i, j: (i, j)),
        out_shape=jax.ShapeDtypeStruct((m, GZ_WIDTH), BF16),
        compiler_params=_cparams(2),
        name="gate_proj",
    )(h, wg)


def _zr_kernel(h_ref, w_ref, gain_ref, ones_ref, o_ref):
    j = pl.program_id(1)
    z = _dot(h_ref[...], w_ref[...])

    @pl.when(j >= 2)
    def _():
        o_ref[...] = z

    @pl.when(j < 2)
    def _():
        z2 = z * z
        hi = z2.astype(BF16)
        lo = (z2 - hi.astype(F32)).astype(BF16)
        ssq = _dot(hi, ones_ref[...]) + _dot(lo, ones_ref[...])
        o_ref[...] = z * lax.rsqrt(ssq * (1.0 / A_QK_DIM) + EPS) * gain_ref[0]


def _zr_proj(h, wr, gains, ones_bd):
    m = h.shape[0]
    tm, tn = min(m, 1024), 512
    return pl.pallas_call(
        _zr_kernel,
        grid=(m // tm, ZR_WIDTH // tn),
        in_specs=[pl.BlockSpec((tm, D_MODEL), lambda i, j: (i, 0)),
                  pl.BlockSpec((D_MODEL, tn), lambda i, j: (0, j)),
                  pl.BlockSpec((1, 1, tn), lambda i, j: (jnp.minimum(j, 1), 0, 0)),
                  pl.BlockSpec((tn, tn), lambda i, j: (0, 0))],
        out_specs=pl.BlockSpec((tm, tn), lambda i, j: (i, j)),
        out_shape=jax.ShapeDtypeStruct((m, ZR_WIDTH), F32),
        compiler_params=_cparams(2),
        name="zr_proj",
    )(h, wr, gains, ones_bd)


def _bucket(n):
    max_exact = REL_BUCKETS // 2
    large = max_exact + (jnp.log(jnp.maximum(n, 1).astype(F32) / max_exact)
                         / math.log(REL_MAX_DIST / max_exact)
                         * (REL_BUCKETS - max_exact)).astype(jnp.int32)
    return jnp.where(n < max_exact, n, jnp.minimum(large, REL_BUCKETS - 1))


def _bias_kernel(tab_ref, bp_ref, bs_ref, *, t_dec):
    h = pl.program_id(0)

    def lookup(n):
        bucket = _bucket(n)
        val = jnp.full(n.shape, tab_ref[h, REL_BUCKETS - 1], F32)
        for b in range(REL_BUCKETS - 1):
            val = jnp.where(bucket == b, tab_ref[h, b], val)
        return val

    tq = bp_ref.shape[2]
    row = lax.broadcasted_iota(jnp.int32, (tq, tq), 0)
    col = lax.broadcasted_iota(jnp.int32, (tq, tq), 1)
    bp_ref[0, 0] = jnp.where(col <= row, lookup(jnp.maximum(row - col, 0)), NEG_INF)
    bp_ref[0, 1] = lookup(row - col + tq)

    rows, cols = bs_ref.shape[1], bs_ref.shape[2]
    r = lax.broadcasted_iota(jnp.int32, (rows, cols), 0)
    c = lax.broadcasted_iota(jnp.int32, (rows, cols), 1)
    t = r % t_dec
    tok = c // A_HEADS
    valid = (c % A_HEADS) == h
    far = jnp.full((rows, cols), tab_ref[h, REL_BUCKETS - 1], F32)
    bs_ref[0] = jnp.where(valid, far, NEG_INF)
    bs_ref[1] = jnp.where(valid, lookup(PAGE_SIZE + t - tok), NEG_INF)
    new_ok = valid & (tok <= t) & (tok < t_dec)
    bs_ref[2] = jnp.where(new_ok, lookup(jnp.maximum(t - tok, 0)), NEG_INF)


def _bias_tiles(table, t_dec):
    rows = 2 * t_dec
    return pl.pallas_call(
        functools.partial(_bias_kernel, t_dec=t_dec),
        grid=(A_HEADS,),
        in_specs=[pl.BlockSpec(memory_space=pltpu.SMEM)],
        out_specs=[pl.BlockSpec((1, 2, ATT_TQ, ATT_TQ), lambda h: (h, 0, 0, 0)),
                   pl.BlockSpec((3, rows, PAGE_SIZE * A_HEADS), lambda h: (0, h, 0))],
        out_shape=[jax.ShapeDtypeStruct((A_HEADS, 2, ATT_TQ, ATT_TQ), F32),
                   jax.ShapeDtypeStruct((3, A_HEADS * rows, PAGE_SIZE * A_HEADS), F32)],
        compiler_params=_cparams(1),
        name="bias_tiles",
    )(table.T)


def _lam(lamv_ref, lam_init):
    s1 = jnp.sum(lamv_ref[0:1, :] * lamv_ref[1:2, :], axis=1, keepdims=True)
    s2 = jnp.sum(lamv_ref[2:3, :] * lamv_ref[3:4, :], axis=1, keepdims=True)
    return jnp.exp(s1) - jnp.exp(s2) + lam_init


def _subln(o, g_ref, lam_init):
    y = o * lax.rsqrt(jnp.mean(o * o, axis=-1, keepdims=True) + EPS)
    return y * g_ref[...] * (1.0 - lam_init)


def _attn_prompt_kernel(tab_ref, q_ref, k_ref, v_ref, bias_ref, lamv_ref, subg_ref, o_ref,
                        kb, vb, m_s, l_s, acc_s, *, lam_init):
    h = pl.program_id(1)
    qi = pl.program_id(2)
    tq = q_ref.shape[1]

    @pl.when(qi == 0)
    def _():
        kb[...] = k_ref[0].astype(BF16)
        vb[...] = v_ref[0].astype(BF16)

    q = q_ref[0]
    lane = lax.broadcasted_iota(jnp.int32, q.shape, 1)
    qm = (jnp.where(lane < A_QK_DIM, q, 0.0).astype(BF16),
          jnp.where(lane >= A_QK_DIM, q, 0.0).astype(BF16))
    m_s[...] = jnp.full(m_s.shape, NEG_INF, F32)
    l_s[...] = jnp.zeros(l_s.shape, F32)
    acc_s[...] = jnp.zeros(acc_s.shape, F32)

    def step(j, bias):
        start = pl.multiple_of(j * tq, tq)
        kj = kb[pl.ds(start, tq), :]
        vj = vb[pl.ds(start, tq), :]
        for m in range(2):
            s = _dot_nt(qm[m], kj) + bias
            m_prev = m_s[m]
            m_new = jnp.maximum(m_prev, jnp.max(s, axis=1, keepdims=True))
            p = jnp.exp(s - m_new)
            alpha = jnp.exp(m_prev - m_new)
            l_s[m] = alpha * l_s[m] + jnp.sum(p, axis=1, keepdims=True)
            acc_s[m] = alpha * acc_s[m] + _dot(p.astype(BF16), vj)
            m_s[m] = m_new

    far = tab_ref[h, REL_BUCKETS - 1]

    def far_body(j, carry):
        step(j, far)
        return carry

    lax.fori_loop(0, jnp.maximum(qi - 1, 0), far_body, 0)

    @pl.when(qi >= 1)
    def _():
        step(qi - 1, bias_ref[0, 1])

    step(qi, bias_ref[0, 0])

    lam = _lam(lamv_ref, lam_init)
    o = acc_s[0] * (1.0 / l_s[0]) - lam * (acc_s[1] * (1.0 / l_s[1]))
    o_ref[0] = _subln(o, subg_ref, lam_init).astype(o_ref.dtype)


def _attn_prompt(zr3, bias_p, tab_t, lamv, subg, lam_init):
    b, s, _ = zr3.shape
    tq = ATT_TQ
    kern = functools.partial(_attn_prompt_kernel, lam_init=lam_init)
    return pl.pallas_call(
        kern,
        grid=(b, A_HEADS, s // tq),
        in_specs=[pl.BlockSpec(memory_space=pltpu.SMEM),
                  pl.BlockSpec((1, tq, 128), lambda bi, h, qi: (bi, qi, COL_AQ // 128 + h)),
                  pl.BlockSpec((1, s, 128), lambda bi, h, qi: (bi, 0, COL_AK // 128 + h)),
                  pl.BlockSpec((1, s, 128), lambda bi, h, qi: (bi, 0, COL_AV // 128 + h)),
                  pl.BlockSpec((1, 2, tq, tq), lambda bi, h, qi: (h, 0, 0, 0)),
                  pl.BlockSpec((4, A_QK_DIM), lambda bi, h, qi: (0, 0)),
                  pl.BlockSpec((1, A_V_DIM), lambda bi, h, qi: (0, 0))],
        out_specs=pl.BlockSpec((1, tq, 128), lambda bi, h, qi: (bi, qi, h)),
        out_shape=jax.ShapeDtypeStruct((b, s, A_WIDTH), BF16),
        scratch_shapes=[pltpu.VMEM((s, 128), BF16), pltpu.VMEM((s, 128), BF16),
                        pltpu.VMEM((2, tq, 1), F32), pltpu.VMEM((2, tq, 1), F32),
                        pltpu.VMEM((2, tq, 128), F32)],
        compiler_params=_cparams(3),
        name="attn_prompt",
    )(tab_t, zr3, zr3, zr3, bias_p, lamv, subg)


def _attn_sample_kernel(pt_ref, q_ref, kn_ref, vn_ref, bias_ref, lamv_ref, subg_ref, *rest,
                        lam_init, n_pg, t_dec):
    k_refs = rest[:n_pg]
    v_refs = rest[n_pg:2 * n_pg]
    o_ref = rest[2 * n_pg]
    q_s, m_s, l_s, acc_s = rest[2 * n_pg + 1:]
    j = pl.program_id(1)
    last = pl.num_programs(1) - 1
    rows = 2 * t_dec

    @pl.when(j == 0)
    def _():
        q = q_ref[0]
        lane = lax.broadcasted_iota(jnp.int32, (t_dec, 128), 1)
        for h in range(A_HEADS):
            qh = q[:, h * 128:(h + 1) * 128]
            q_s[h * rows:h * rows + t_dec, :] = jnp.where(lane < A_QK_DIM, qh, 0.0)
            q_s[h * rows + t_dec:(h + 1) * rows, :] = jnp.where(lane >= A_QK_DIM, qh, 0.0)
        m_s[...] = jnp.full(m_s.shape, NEG_INF, F32)
        l_s[...] = jnp.zeros(l_s.shape, F32)
        acc_s[...] = jnp.zeros(acc_s.shape, F32)

    qb = q_s[...].astype(BF16)

    def update(kb, vb, bias):
        s = _dot_nt(qb, kb) + bias
        m_prev = m_s[...]
        m_new = jnp.maximum(m_prev, jnp.max(s, axis=1, keepdims=True))
        p = jnp.exp(s - m_new)
        alpha = jnp.exp(m_prev - m_new)
        l_s[...] = alpha * l_s[...] + jnp.sum(p, axis=1, keepdims=True)
        acc_s[...] = alpha * acc_s[...] + _dot(p.astype(BF16), vb)
        m_s[...] = m_new

    for p in range(n_pg):
        if p == n_pg - 1:
            bias = jnp.where(j == last, bias_ref[1], bias_ref[0])
        else:
            bias = bias_ref[0]
        update(k_refs[p][0, 0].astype(BF16), v_refs[p][0, 0].astype(BF16), bias)

    @pl.when(j == last)
    def _():
        nk = kn_ref.shape[1]
        update(kn_ref[0].astype(BF16), vn_ref[0].astype(BF16), bias_ref[2][:, :nk])
        lam = _lam(lamv_ref, lam_init)
        on = acc_s[...] * (1.0 / l_s[...])
        for h in range(A_HEADS):
            o = on[h * rows:h * rows + t_dec] - lam * on[h * rows + t_dec:(h + 1) * rows]
            o_ref[0, :, h * 128:(h + 1) * 128] = _subln(o, subg_ref, lam_init).astype(o_ref.dtype)


def _attn_sample(zrs3, kn2d, vn2d, cache_k2d, cache_v2d, page_table, bias_s, lamv, subg,
                 layer, lam_init):
    bd, t_dec, _ = zrs3.shape
    n_pages = page_table.shape[1]
    n_pg = SAMPLE_PAGES
    rows = 2 * t_dec * A_HEADS
    pcols = PAGE_SIZE * A_HEADS

    def page_spec(p):
        return pl.BlockSpec((1, 1, pcols, 128),
                            lambda b, j, pt: (layer, pt[b, j * n_pg + p], 0, 0))

    kern = functools.partial(_attn_sample_kernel, lam_init=lam_init, n_pg=n_pg, t_dec=t_dec)
    grid_spec = pltpu.PrefetchScalarGridSpec(
        num_scalar_prefetch=1,
        grid=(bd, n_pages // n_pg),
        in_specs=[pl.BlockSpec((1, t_dec, A_WIDTH), lambda b, j, pt: (b, 0, 0)),
                  pl.BlockSpec((1,) + kn2d.shape[1:], lambda b, j, pt: (b, 0, 0)),
                  pl.BlockSpec((1,) + vn2d.shape[1:], lambda b, j, pt: (b, 0, 0)),
                  pl.BlockSpec((3, rows, pcols), lambda b, j, pt: (0, 0, 0)),
                  pl.BlockSpec((4, A_QK_DIM), lambda b, j, pt: (0, 0)),
                  pl.BlockSpec((1, A_V_DIM), lambda b, j, pt: (0, 0))]
                 + [page_spec(p) for p in range(n_pg)] * 2,
        out_specs=pl.BlockSpec((1, t_dec, A_WIDTH), lambda b, j, pt: (b, 0, 0)),
        scratch_shapes=[pltpu.VMEM((rows, 128), F32), pltpu.VMEM((rows, 1), F32),
                        pltpu.VMEM((rows, 1), F32), pltpu.VMEM((rows, 128), F32)],
    )
    return pl.pallas_call(
        kern,
        grid_spec=grid_spec,
        out_shape=jax.ShapeDtypeStruct((bd, t_dec, A_WIDTH), BF16),
        compiler_params=_cparams(2),
        name="attn_sample",
    )(page_table, zrs3, kn2d, vn2d, bias_s, lamv, subg,
      *([cache_k2d] * n_pg), *([cache_v2d] * n_pg))


def _gelu(x):
    return 0.5 * x * (1.0 + jnp.tanh(math.sqrt(2.0 / math.pi) * (x + 0.044715 * (x * x * x))))


def _chunk_mlp_kernel(u_ref, v_ref, w_ref, bias_ref, g_ref, b_ref, y_ref, vb_ref):
    n_chunks = u_ref.shape[0] // B_CHUNK
    lane_grp = lax.broadcasted_iota(jnp.int32, (B_CHUNK, B_WIDTH), 1) // (B_WIDTH // B_GROUPS)
    for c in range(n_chunks):
        sl = slice(c * B_CHUNK, (c + 1) * B_CHUNK)
        gv = _gelu(v_ref[sl, :])
        xc = gv - jnp.mean(gv, axis=-1, keepdims=True)
        vb = xc * lax.rsqrt(jnp.mean(xc * xc, axis=-1, keepdims=True) + EPS) * g_ref[...] + b_ref[...]
        vb_ref[sl, :] = vb
        vbb = vb.astype(BF16)
        mixed = bias_ref[...]
        for g in range(B_GROUPS):
            mixed = mixed + jnp.where(lane_grp == g, _dot(w_ref[g], vbb), 0.0)
        y_ref[sl, :] = (_gelu(u_ref[sl, :]) * mixed).astype(y_ref.dtype)


def _chunk_mlp(zr, w_eff, bias_eff, ln_g, ln_b):
    m = zr.shape[0]
    tm = min(m, 512)
    return pl.pallas_call(
        _chunk_mlp_kernel,
        grid=(m // tm,),
        in_specs=[pl.BlockSpec((tm, B_WIDTH), lambda i: (i, COL_BU // B_WIDTH)),
                  pl.BlockSpec((tm, B_WIDTH), lambda i: (i, COL_BV // B_WIDTH)),
                  pl.BlockSpec((B_GROUPS, B_CHUNK, B_CHUNK), lambda i: (0, 0, 0)),
                  pl.BlockSpec((B_CHUNK, B_WIDTH), lambda i: (0, 0)),
                  pl.BlockSpec((1, B_WIDTH), lambda i: (0, 0)),
                  pl.BlockSpec((1, B_WIDTH), lambda i: (0, 0))],
        out_specs=[pl.BlockSpec((tm, B_WIDTH), lambda i: (i, 0)),
                   pl.BlockSpec((tm, B_WIDTH), lambda i: (i, 0))],
        out_shape=[jax.ShapeDtypeStruct((m, B_WIDTH), BF16),
                   jax.ShapeDtypeStruct((m, B_WIDTH), F32)],
        compiler_params=_cparams(1),
        name="chunk_mlp",
    )(zr, zr, w_eff, bias_eff, ln_g, ln_b)


def _pool_kernel(x_ref, w_ref, scale_ref, y_ref, *, prefix, pos0):
    x = x_ref[0]
    rows = x.shape[0]
    row = lax.broadcasted_iota(jnp.int32, x.shape, 0)
    grp = lax.broadcasted_iota(jnp.int32, x.shape, 1) // C_GROUP_DIM

    def shifted(a, k):
        return jnp.where(row >= k, pltpu.roll(a, k, 0), 0.0)

    sums = []
    acc = x
    for k in (1, 2, 4, 8):
        acc = acc + shifted(acc, k)
        sums.append(acc)
    total = sums[3]
    win = jnp.full(x.shape, POOL_WINDOWS[3], jnp.int32)
    for g in range(3):
        total = jnp.where(grp == g, sums[g], total)
        win = jnp.where(grp == g, POOL_WINDOWS[g], win)
    pos = pos0 + row - prefix
    cnt = jnp.clip(pos + 1, 1, win).astype(F32)
    d = total / cnt - x
    y = _dot(d.astype(BF16), w_ref[...]) * scale_ref[...]
    y_ref[0] = y.astype(y_ref.dtype)
    del rows


def _pool_mix(xx, col_block, w_bd, scale, prefix, pos0):
    b, rows, _ = xx.shape
    kern = functools.partial(_pool_kernel, prefix=prefix, pos0=pos0)
    return pl.pallas_call(
        kern,
        grid=(b,),
        in_specs=[pl.BlockSpec((1, rows, C_WIDTH), lambda i: (i, 0, col_block)),
                  pl.BlockSpec((C_WIDTH, C_WIDTH), lambda i: (0, 0)),
                  pl.BlockSpec((1, C_WIDTH), lambda i: (0, 0))],
        out_specs=pl.BlockSpec((1, rows, C_WIDTH), lambda i: (i, 0, 0)),
        out_shape=jax.ShapeDtypeStruct((b, rows, C_WIDTH), BF16),
        compiler_params=_cparams(1),
        name="pool_mix",
    )(xx, w_bd, scale)


def _log_sigmoid(x):
    return jnp.minimum(x, 0.0) - jnp.log1p(jnp.exp(-jnp.abs(x)))


def _mlstm_kernel(q_ref, k_ref, v_ref, o_ref, g_ref, gb_ref, ng_ref, c0_ref, n0_ref, m0_ref,
                  y_ref, c_out, n_out, m_out, c_s, n_s, m_s, *, t_valid):
    ci = pl.program_id(1)
    chunk = q_ref.shape[1]

    @pl.when(ci == 0)
    def _():
        c_s[...] = c0_ref[0]
        n_s[...] = n0_ref[0]
        for h in range(D_HEADS):
            m_s[h] = m0_ref[0, :, h:h + 1]

    g = g_ref[0] + gb_ref[...]
    lf = _log_sigmoid(g)
    row = lax.broadcasted_iota(jnp.int32, g.shape, 0)
    if t_valid < chunk:
        g = jnp.where(row < t_valid, g, NEG_INF)
        lf = jnp.where(row < t_valid, lf, 0.0)
    cum = lf
    k = 1
    while k < chunk:
        cum = cum + jnp.where(row >= k, pltpu.roll(cum, k, 0), 0.0)
        k *= 2

    rr = lax.broadcasted_iota(jnp.int32, (chunk, chunk), 0)
    cc = lax.broadcasted_iota(jnp.int32, (chunk, chunk), 1)
    eye = rr == cc
    tril = cc <= rr

    def as_row(col):
        return jnp.sum(jnp.where(eye, col, 0.0), axis=0, keepdims=True)

    for h in range(D_HEADS):
        sl = slice(h * D_HEAD_DIM, (h + 1) * D_HEAD_DIM)
        qh = q_ref[0, :, sl]
        kh = k_ref[0, :, sl] * (D_HEAD_DIM ** -0.5)
        vh = v_ref[0, :, sl]
        qb, kb, vb = qh.astype(BF16), kh.astype(BF16), vh.astype(BF16)
        ig_c = g[:, h:h + 1]
        b_c = cum[:, D_HEADS + h:D_HEADS + h + 1]
        ig_r, b_r = as_row(ig_c), as_row(b_c)
        m_prev = m_s[h]
        dm = jnp.where(tril, b_c - b_r + ig_r, NEG_INF)
        inter = b_c + m_prev
        mt = jnp.maximum(inter, jnp.max(dm, axis=1, keepdims=True))
        qk = _dot_nt(qb, kb) * jnp.exp(dm - mt)
        winter = jnp.exp(inter - mt)
        num = winter * _dot(qb, c_s[h].astype(BF16)) + _dot(qk.astype(BF16), vb)
        den = (winter * jnp.sum(qh * n_s[h], axis=1, keepdims=True)
               + jnp.sum(qk, axis=1, keepdims=True))
        hh = num / jnp.maximum(jnp.abs(den), jnp.exp(-mt))
        y = hh * lax.rsqrt(jnp.mean(hh * hh, axis=-1, keepdims=True) + EPS) * ng_ref[...]
        y_ref[0, :, sl] = (y * jax.nn.sigmoid(o_ref[0, :, sl])).astype(y_ref.dtype)
        b_last = b_c[chunk - 1:chunk, :]
        a = b_last - b_c + ig_c
        m_new = jnp.maximum(b_last + m_prev, jnp.max(a, axis=0, keepdims=True))
        ws = jnp.exp(a - m_new)
        decay = jnp.exp(b_last + m_prev - m_new)
        kw = ws * kh
        c_s[h] = decay * c_s[h] + _dot_tn(kw.astype(BF16), vb)
        n_s[h] = decay * n_s[h] + jnp.sum(kw, axis=0, keepdims=True)
        m_s[h] = m_new

    @pl.when(ci == pl.num_programs(1) - 1)
    def _():
        c_out[0] = c_s[...]
        n_out[0] = n_s[...]
        for h in range(D_HEADS):
            m_out[0, :, h:h + 1] = m_s[h]


def _mlstm(src, col0, gates_block, chunk, t_valid, gate_bias, norm_g, c0, n0, m0):
    b, t, _ = src.shape
    kern = functools.partial(_mlstm_kernel, t_valid=t_valid)

    def col(cb):
        return pl.BlockSpec((1, chunk, D_WIDTH), lambda bi, ci: (bi, ci, cb))

    hd = D_HEAD_DIM
    return pl.pallas_call(
        kern,
        grid=(b, t // chunk),
        in_specs=[col(col0), col(col0 + 1), col(col0 + 2), col(col0 + 3),
                  pl.BlockSpec((1, chunk, 128), lambda bi, ci: (bi, ci, gates_block)),
                  pl.BlockSpec((1, 128), lambda bi, ci: (0, 0)),
                  pl.BlockSpec((1, hd), lambda bi, ci: (0, 0)),
                  pl.BlockSpec((1, D_HEADS, hd, hd), lambda bi, ci: (bi, 0, 0, 0)),
                  pl.BlockSpec((1, D_HEADS, 1, hd), lambda bi, ci: (bi, 0, 0, 0)),
                  pl.BlockSpec((1, 1, D_HEADS), lambda bi, ci: (bi, 0, 0))],
        out_specs=[pl.BlockSpec((1, chunk, D_WIDTH), lambda bi, ci: (bi, ci, 0)),
                   pl.BlockSpec((1, D_HEADS, hd, hd), lambda bi, ci: (bi, 0, 0, 0)),
                   pl.BlockSpec((1, D_HEADS, 1, hd), lambda bi, ci: (bi, 0, 0, 0)),
                   pl.BlockSpec((1, 1, D_HEADS), lambda bi, ci: (bi, 0, 0))],
        out_shape=[jax.ShapeDtypeStruct((b, t, D_WIDTH), BF16),
                   jax.ShapeDtypeStruct((b, D_HEADS, hd, hd), F32),
                   jax.ShapeDtypeStruct((b, D_HEADS, 1, hd), F32),
                   jax.ShapeDtypeStruct((b, 1, D_HEADS), F32)],
        scratch_shapes=[pltpu.VMEM((D_HEADS, hd, hd), F32), pltpu.VMEM((D_HEADS, 1, hd), F32),
                        pltpu.VMEM((D_HEADS, 1, 1), F32)],
        compiler_params=_cparams(2),
        name="mlstm",
    )(src, src, src, src, src, gate_bias, norm_g, c0, n0, m0)


def _merge_kernel(x_ref, g_ref, ya_ref, yb_ref, yc_ref, yd_ref, wa_ref, wb_ref, wc_ref, wd_ref,
                  wo_ref, o_ref):
    def gate(i):
        return g_ref[:, i * D_MODEL:(i + 1) * D_MODEL].astype(F32)

    merged = gate(0) * _dot(ya_ref[...], wa_ref[...])
    merged = merged + gate(1) * _dot(yb_ref[...], wb_ref[...])
    merged = merged + gate(2) * _dot(yc_ref[...], wc_ref[...])
    merged = merged + gate(3) * _dot(yd_ref[...], wd_ref[...])
    o_ref[...] = x_ref[...] + _dot(merged.astype(BF16), wo_ref[...])


def _merge(x, gates, ya, yb, yc, yd, wa, wb, wc, wd, wo):
    m = x.shape[0]
    tm = min(m, 512)

    def rows(width):
        return pl.BlockSpec((tm, width), lambda i: (i, 0))

    def full(arr):
        return pl.BlockSpec(arr.shape, lambda i: (0, 0))

    return pl.pallas_call(
        _merge_kernel,
        grid=(m // tm,),
        in_specs=[rows(D_MODEL), rows(GZ_WIDTH), rows(A_WIDTH), rows(B_WIDTH), rows(C_WIDTH),
                  rows(D_WIDTH), full(wa), full(wb), full(wc), full(wd), full(wo)],
        out_specs=rows(D_MODEL),
        out_shape=jax.ShapeDtypeStruct((m, D_MODEL), F32),
        compiler_params=_cparams(1),
        name="merge",
    )(x, gates, ya, yb, yc, yd, wa, wb, wc, wd, wo)


def _ffn_kernel(x_ref, g_ref, w1_ref, w2_ref, o_ref, h_s, acc_s):
    j = pl.program_id(1)

    @pl.when(j == 0)
    def _():
        x = x_ref[...]
        y = x * lax.rsqrt(jnp.mean(x * x, axis=-1, keepdims=True) + EPS)
        h_s[...] = (y * g_ref[...]).astype(BF16)
        acc_s[...] = jnp.zeros(acc_s.shape, F32)

    a = jnp.maximum(_dot(h_s[...], w1_ref[...]), 0.0)
    acc_s[...] += _dot((a * a).astype(BF16), w2_ref[...])

    @pl.when(j == pl.num_programs(1) - 1)
    def _():
        o_ref[...] = x_ref[...] + acc_s[...]


def _ffn(x, g, w1, w2):
    m = x.shape[0]
    tm, tf = min(m, 1024), 512
    return pl.pallas_call(
        _ffn_kernel,
        grid=(m // tm, D_FF // tf),
        in_specs=[pl.BlockSpec((tm, D_MODEL), lambda i, j: (i, 0)),
                  pl.BlockSpec((1, D_MODEL), lambda i, j: (0, 0)),
                  pl.BlockSpec((D_MODEL, tf), lambda i, j: (0, j)),
                  pl.BlockSpec((tf, D_MODEL), lambda i, j: (j, 0))],
        out_specs=pl.BlockSpec((tm, D_MODEL), lambda i, j: (i, 0)),
        out_shape=jax.ShapeDtypeStruct((m, D_MODEL), F32),
        scratch_shapes=[pltpu.VMEM((tm, D_MODEL), BF16), pltpu.VMEM((tm, D_MODEL), F32)],
        compiler_params=_cparams(2),
        name="ffn",
    )(x, g.reshape(1, D_MODEL), w1, w2)


def _layer_weights(p, l):
    w_in = p["w_in"][l]
    wr = jnp.concatenate(
        [w_in[:, :3072], w_in[:, 3080:3336], w_in[:, 3072:3080],
         jnp.zeros((D_MODEL, ZR_WIDTH - 3336), F32)], axis=1).astype(BF16)
    wg = w_in[:, 3336:].astype(BF16)
    reps = A_WIDTH // A_QK_DIM
    gains = jnp.stack([jnp.tile(p["q_norm_g"][l], reps) * (A_QK_DIM ** -0.5),
                       jnp.tile(p["k_norm_g"][l], reps)]).reshape(2, 1, A_WIDTH)
    grp = jnp.arange(A_WIDTH) // A_QK_DIM
    ones_bd = (grp[:, None] == grp[None, :]).astype(BF16)
    lamv = jnp.stack([p["lam_q1"][l], p["lam_k1"][l], p["lam_q2"][l], p["lam_k2"][l]])
    gate_bias = jnp.zeros((1, 128), F32)
    gate_bias = gate_bias.at[0, :D_HEADS].set(p["d_i_bias"][l])
    gate_bias = gate_bias.at[0, D_HEADS:2 * D_HEADS].set(p["d_f_bias"][l])
    c_bd = jnp.zeros((C_WIDTH, C_WIDTH), F32)
    for g in range(4):
        sl = slice(g * C_GROUP_DIM, (g + 1) * C_GROUP_DIM)
        c_bd = c_bd.at[sl, sl].set(p["c_lin"][l][g])
    return dict(
        norm1_g=p["norm1_g"][l], norm2_g=p["norm2_g"][l], wr=wr, wg=wg, gains=gains,
        ones_bd=ones_bd, lamv=lamv, subg=p["subln_g"][l].reshape(1, A_V_DIM),
        b_ln_g=p["b_ln_g"][l].reshape(1, B_WIDTH), b_ln_b=p["b_ln_b"][l].reshape(1, B_WIDTH),
        b_ws=p["b_ws"][l], b_bias=p["b_bias"][l],
        c_bd=c_bd.astype(BF16), c_scale=p["c_scale"][l].reshape(1, C_WIDTH),
        gate_bias=gate_bias, d_norm_g=p["d_norm_g"][l].reshape(1, D_HEAD_DIM),
        w_pa=p["w_pa"][l].astype(BF16), w_pb=p["w_pb"][l].astype(BF16),
        w_pc=p["w_pc"][l].astype(BF16), w_pd=p["w_pd"][l].astype(BF16),
        w_out=p["w_out"][l].astype(BF16), w_ff1=p["w_ff1"][l].astype(BF16),
        w_ff2=p["w_ff2"][l].astype(BF16),
        lam_init=0.8 - 0.6 * math.exp(-0.3 * l),
    )


def _chunk_weights(w, t):
    length = min(t, B_CHUNK)
    ws = jnp.tril(w["b_ws"][:, :length, :length])
    bias = jnp.transpose(w["b_bias"][:, :length])
    reps = B_CHUNK // length
    if reps > 1:
        eye = jnp.eye(reps, dtype=F32)
        ws = jax.vmap(lambda a: jnp.kron(eye, a))(ws)
        bias = jnp.tile(bias, (reps, 1))
    return ws.astype(BF16), jnp.repeat(bias, B_WIDTH // B_GROUPS, axis=1)


def _finish_layer(x2, w, gates, zr, ya, yc, yd, t):
    w_eff, bias_eff = _chunk_weights(w, t)
    yb, vb = _chunk_mlp(zr, w_eff, bias_eff, w["b_ln_g"], w["b_ln_b"])
    x2 = _merge(x2, gates, ya, yb, yc, yd, w["w_pa"], w["w_pb"], w["w_pc"], w["w_pd"], w["w_out"])
    return _ffn(x2, w["norm2_g"], w["w_ff1"], w["w_ff2"]), vb


def _prompt_layer(x2, w, b, s, bias_p, tab_t):
    m = b * s
    h = _rms_cast(x2, w["norm1_g"])
    gates = _gate_proj(h, w["wg"])
    zr = _zr_proj(h, w["wr"], w["gains"], w["ones_bd"])
    zr3 = zr.reshape(b, s, ZR_WIDTH)
    ya = _attn_prompt(zr3, bias_p, tab_t, w["lamv"], w["subg"], w["lam_init"]).reshape(m, A_WIDTH)
    yc = _pool_mix(zr3, COL_CX // C_WIDTH, w["c_bd"], w["c_scale"], 0, 0).reshape(m, C_WIDTH)
    hd = D_HEAD_DIM
    yd, c_new, n_new, m_new = _mlstm(
        zr3, COL_DQ // D_WIDTH, COL_DG // 128, min(s, 128), min(s, 128), w["gate_bias"],
        w["d_norm_g"], jnp.zeros((b, D_HEADS, hd, hd), F32), jnp.zeros((b, D_HEADS, 1, hd), F32),
        jnp.zeros((b, 1, D_HEADS), F32))
    x2, _ = _finish_layer(x2, w, gates, zr, ya, yc, yd.reshape(m, D_WIDTH), s)
    outs = (zr3[:, :, COL_AK:COL_AK + A_WIDTH].reshape(b, s, A_HEADS, 2 * A_QK_DIM),
            zr3[:, :, COL_AV:COL_AV + A_WIDTH].reshape(b, s, A_HEADS, A_V_DIM),
            zr3[:, s - POOL_BUF:, COL_CX:COL_CX + C_WIDTH],
            c_new, n_new.reshape(b, D_HEADS, hd), m_new.reshape(b, D_HEADS))
    return x2, outs


def _sample_layer(x2, w, l, bd, t, cache_k2d, cache_v2d, page_table, bias_s, pool0, c0, n0, m0):
    m = bd * t
    past = page_table.shape[1] * PAGE_SIZE
    h = _rms_cast(x2, w["norm1_g"])
    gates = _gate_proj(h, w["wg"])
    zr = _zr_proj(h, w["wr"], w["gains"], w["ones_bd"])
    zr3 = zr.reshape(bd, t, ZR_WIDTH)
    kn = zr3[:, :, COL_AK:COL_AK + A_WIDTH]
    vn = zr3[:, :, COL_AV:COL_AV + A_WIDTH]
    ya = _attn_sample(zr3, kn.reshape(bd, t * A_HEADS, 128), vn.reshape(bd, t * A_HEADS, 128),
                      cache_k2d, cache_v2d, page_table, bias_s, w["lamv"], w["subg"], l,
                      w["lam_init"]).reshape(m, A_WIDTH)
    cx = zr3[:, :, COL_CX:COL_CX + C_WIDTH]
    prefix = POOL_BUF + 1
    rows = -(-(prefix + t) // 8) * 8
    xx = jnp.concatenate([jnp.zeros((bd, 1, C_WIDTH), F32), pool0, cx,
                          jnp.zeros((bd, rows - prefix - t, C_WIDTH), F32)], axis=1)
    yc = _pool_mix(xx, 0, w["c_bd"], w["c_scale"], prefix, past)[:, prefix:prefix + t]
    yc = yc.reshape(m, C_WIDTH)
    chunk = -(-t // 8) * 8
    dsrc = jnp.pad(zr3[:, :, COL_DQ:COL_DG + 128], ((0, 0), (0, chunk - t), (0, 0)))
    hd = D_HEAD_DIM
    yd, c_new, n_new, m_new = _mlstm(
        dsrc, 0, (COL_DG - COL_DQ) // 128, chunk, t, w["gate_bias"], w["d_norm_g"],
        c0, n0.reshape(bd, D_HEADS, 1, hd), m0.reshape(bd, 1, D_HEADS))
    yd = yd[:, :t].reshape(m, D_WIDTH)
    x2, vb = _finish_layer(x2, w, gates, zr, ya, yc, yd, t)
    outs = (kn.reshape(bd, t, A_HEADS, 2 * A_QK_DIM), vn.reshape(bd, t, A_HEADS, A_V_DIM),
            vb.reshape(bd, t, B_WIDTH), jnp.concatenate([pool0, cx], axis=1)[:, -POOL_BUF:],
            c_new, n_new.reshape(bd, D_HEADS, hd), m_new.reshape(bd, D_HEADS))
    return x2, outs


def kernel(x_prompt, x_sample, cache_k, cache_v, page_table, state_pool, state_C, state_n, state_m, rel_bias, norm1_g, norm2_g, w_in, q_norm_g, k_norm_g, lam_q1, lam_k1, lam_q2, lam_k2, subln_g, b_ln_g, b_ln_b, b_ws, b_bias, c_lin, c_scale, d_i_bias, d_f_bias, d_norm_g, w_pa, w_pb, w_pc, w_pd, w_out, w_ff1, w_ff2):
    p = dict(norm1_g=norm1_g, norm2_g=norm2_g, w_in=w_in, q_norm_g=q_norm_g, k_norm_g=k_norm_g,
             lam_q1=lam_q1, lam_k1=lam_k1, lam_q2=lam_q2, lam_k2=lam_k2, subln_g=subln_g,
             b_ln_g=b_ln_g, b_ln_b=b_ln_b, b_ws=b_ws, b_bias=b_bias, c_lin=c_lin, c_scale=c_scale,
             d_i_bias=d_i_bias, d_f_bias=d_f_bias, d_norm_g=d_norm_g, w_pa=w_pa, w_pb=w_pb,
             w_pc=w_pc, w_pd=w_pd, w_out=w_out, w_ff1=w_ff1, w_ff2=w_ff2)
    depth = w_in.shape[0]
    bp, sp, _ = x_prompt.shape
    bd, td, _ = x_sample.shape
    n_phys = cache_k.shape[1]
    cache_k2d = cache_k.reshape(depth, n_phys, PAGE_SIZE * A_HEADS, 128)
    cache_v2d = cache_v.reshape(depth, n_phys, PAGE_SIZE * A_HEADS, 128)
    bias_p, bias_s = _bias_tiles(rel_bias, td)
    tab_t = rel_bias.T

    xp = x_prompt.reshape(bp * sp, D_MODEL)
    xs = x_sample.reshape(bd * td, D_MODEL)
    prompt_outs, sample_outs = [], []
    for l in range(depth):
        w = _layer_weights(p, l)
        xp, po = _prompt_layer(xp, w, bp, sp, bias_p, tab_t)
        xs, so = _sample_layer(xs, w, l, bd, td, cache_k2d, cache_v2d, page_table, bias_s,
                               state_pool[l], state_C[l], state_n[l], state_m[l])
        prompt_outs.append(po)
        sample_outs.append(so)

    def stack(outs, i):
        return jnp.stack([o[i] for o in outs])

    return (xp.reshape(bp, sp, D_MODEL), xs.reshape(bd, td, D_MODEL),
            stack(prompt_outs, 0), stack(prompt_outs, 1), stack(sample_outs, 0), stack(sample_outs, 1),
            stack(sample_outs, 2), stack(prompt_outs, 2), stack(sample_outs, 3),
            stack(prompt_outs, 3), stack(prompt_outs, 4), stack(prompt_outs, 5),
            stack(sample_outs, 4), stack(sample_outs, 5), stack(sample_outs, 6))
```

```python
import functools
import math

import jax
import jax.numpy as jnp
from jax import lax
from jax.experimental import pallas as pl
from jax.experimental.pallas import tpu as pltpu

F32 = jnp.float32
BF16 = jnp.bfloat16
NEG_INF = float("-inf")

D_MODEL = 1024
A_HEADS = 4
A_QK_DIM = 64
A_V_DIM = 128
A_WIDTH = 512
REL_BUCKETS = 32
REL_MAX_DIST = 128
PAGE_SIZE = 128
B_GROUPS = 4
B_WIDTH = 256
B_CHUNK = 128
C_WIDTH = 256
C_GROUP_DIM = 64
POOL_WINDOWS = (2, 4, 8, 16)
POOL_BUF = 15
D_HEADS = 4
D_WIDTH = 256
D_HEAD_DIM = 64
D_FF = 4096
N_BRANCH = 4
EPS = 1e-6

ZR_WIDTH = 3584
COL_AQ, COL_AK, COL_AV = 0, 512, 1024
COL_BU, COL_BV, COL_CX = 1536, 1792, 2048
COL_DQ, COL_DK, COL_DV, COL_DO, COL_DG = 2304, 2560, 2816, 3072, 3328
GZ_WIDTH = N_BRANCH * D_MODEL

ATT_TQ = 512
SAMPLE_PAGES = 16
MLSTM_BATCH = 4
VMEM_LIMIT = 56 * 1024 * 1024


def _cparams(n_axes):
    return pltpu.CompilerParams(dimension_semantics=("arbitrary",) * n_axes,
                                vmem_limit_bytes=VMEM_LIMIT)


def _dot(a, b):
    return jnp.dot(a, b, preferred_element_type=F32)


def _dot_nt(a, b):
    return lax.dot_general(a, b, (((1,), (1,)), ((), ())), preferred_element_type=F32)


def _dot_tn(a, b):
    return lax.dot_general(a, b, (((0,), (0,)), ((), ())), preferred_element_type=F32)


def _rms_kernel(x_ref, g_ref, o_ref):
    x = x_ref[...]
    y = x * lax.rsqrt(jnp.mean(x * x, axis=-1, keepdims=True) + EPS)
    o_ref[...] = (y * g_ref[...]).astype(o_ref.dtype)


def _rms_cast(x, g):
    m = x.shape[0]
    tm = min(m, 512)
    return pl.pallas_call(
        _rms_kernel,
        grid=(m // tm,),
        in_specs=[pl.BlockSpec((tm, D_MODEL), lambda i: (i, 0)),
                  pl.BlockSpec((1, D_MODEL), lambda i: (0, 0))],
        out_specs=pl.BlockSpec((tm, D_MODEL), lambda i: (i, 0)),
        out_shape=jax.ShapeDtypeStruct((m, D_MODEL), BF16),
        compiler_params=_cparams(1),
        name="rms_cast",
    )(x, g.reshape(1, D_MODEL))


def _gate_kernel(h_ref, w_ref, o_ref):
    o_ref[...] = jax.nn.sigmoid(_dot(h_ref[...], w_ref[...])).astype(o_ref.dtype)


def _gate_proj(h, wg):
    m = h.shape[0]
    tm, tn = min(m, 1024), 512
    return pl.pallas_call(
        _gate_kernel,
        grid=(m // tm, GZ_WIDTH // tn),
        in_specs=[pl.BlockSpec((tm, D_MODEL), lambda i, j: (i, 0)),
                  pl.BlockSpec((D_MODEL, tn), lambda i, j: (0, j))],
        out_specs=pl.BlockSpec((tm, tn), lambda i, j: (i, j)),
        out_shape=jax.ShapeDtypeStruct((m, GZ_WIDTH), BF16),
        compiler_params=_cparams(2),
        name="gate_proj",
    )(h, wg)


def _zr_kernel(h_ref, w_ref, gain_ref, ones_ref, o_ref):
    j = pl.program_id(1)
    z = _dot(h_ref[...], w_ref[...])

    @pl.when(j >= 2)
    def _():
        o_ref[...] = z

    @pl.when(j < 2)
    def _():
        z2 = z * z
        hi = z2.astype(BF16)
        lo = (z2 - hi.astype(F32)).astype(BF16)
        ssq = _dot(hi, ones_ref[...]) + _dot(lo, ones_ref[...])
        o_ref[...] = z * lax.rsqrt(ssq * (1.0 / A_QK_DIM) + EPS) * gain_ref[0]


def _zr_proj(h, wr, gains, ones_bd):
    m = h.shape[0]
    tm, tn = min(m, 1024), 512
    return pl.pallas_call(
        _zr_kernel,
        grid=(m // tm, ZR_WIDTH // tn),
        in_specs=[pl.BlockSpec((tm, D_MODEL), lambda i, j: (i, 0)),
                  pl.BlockSpec((D_MODEL, tn), lambda i, j: (0, j)),
                  pl.BlockSpec((1, 1, tn), lambda i, j: (jnp.minimum(j, 1), 0, 0)),
                  pl.BlockSpec((tn, tn), lambda i, j: (0, 0))],
        out_specs=pl.BlockSpec((tm, tn), lambda i, j: (i, j)),
        out_shape=jax.ShapeDtypeStruct((m, ZR_WIDTH), F32),
        compiler_params=_cparams(2),
        name="zr_proj",
    )(h, wr, gains, ones_bd)


def _bucket(n):
    max_exact = REL_BUCKETS // 2
    large = max_exact + (jnp.log(jnp.maximum(n, 1).astype(F32) / max_exact)
                         / math.log(REL_MAX_DIST / max_exact)
                         * (REL_BUCKETS - max_exact)).astype(jnp.int32)
    return jnp.where(n < max_exact, n, jnp.minimum(large, REL_BUCKETS - 1))


def _bias_kernel(tab_ref, bp_ref, bs_ref, *, t_dec):
    h = pl.program_id(0)

    def lookup(n):
        bucket = _bucket(n)
        val = jnp.full(n.shape, tab_ref[h, REL_BUCKETS - 1], F32)
        for b in range(REL_BUCKETS - 1):
            val = jnp.where(bucket == b, tab_ref[h, b], val)
        return val

    tq = bp_ref.shape[2]
    row = lax.broadcasted_iota(jnp.int32, (tq, tq), 0)
    col = lax.broadcasted_iota(jnp.int32, (tq, tq), 1)
    bp_ref[0, 0] = jnp.where(col <= row, lookup(jnp.maximum(row - col, 0)), NEG_INF)
    bp_ref[0, 1] = lookup(row - col + tq)

    rows, cols = bs_ref.shape[1], bs_ref.shape[2]
    r = lax.broadcasted_iota(jnp.int32, (rows, cols), 0)
    c = lax.broadcasted_iota(jnp.int32, (rows, cols), 1)
    t = r % t_dec
    tok = c // A_HEADS
    valid = (c % A_HEADS) == h
    far = jnp.full((rows, cols), tab_ref[h, REL_BUCKETS - 1], F32)
    bs_ref[0] = jnp.where(valid, far, NEG_INF)
    bs_ref[1] = jnp.where(valid, lookup(PAGE_SIZE + t - tok), NEG_INF)
    new_ok = valid & (tok <= t) & (tok < t_dec)
    bs_ref[2] = jnp.where(new_ok, lookup(jnp.maximum(t - tok, 0)), NEG_INF)


def _bias_tiles(table, t_dec):
    rows = 2 * t_dec
    return pl.pallas_call(
        functools.partial(_bias_kernel, t_dec=t_dec),
        grid=(A_HEADS,),
        in_specs=[pl.BlockSpec(memory_space=pltpu.SMEM)],
        out_specs=[pl.BlockSpec((1, 2, ATT_TQ, ATT_TQ), lambda h: (h, 0, 0, 0)),
                   pl.BlockSpec((3, rows, PAGE_SIZE * A_HEADS), lambda h: (0, h, 0))],
        out_shape=[jax.ShapeDtypeStruct((A_HEADS, 2, ATT_TQ, ATT_TQ), F32),
                   jax.ShapeDtypeStruct((3, A_HEADS * rows, PAGE_SIZE * A_HEADS), F32)],
        compiler_params=_cparams(1),
        name="bias_tiles",
    )(table.T)


def _lam(lamv_ref, lam_init):
    s1 = jnp.sum(lamv_ref[0:1, :] * lamv_ref[1:2, :], axis=1, keepdims=True)
    s2 = jnp.sum(lamv_ref[2:3, :] * lamv_ref[3:4, :], axis=1, keepdims=True)
    return jnp.exp(s1) - jnp.exp(s2) + lam_init


def _subln(o, g_ref, lam_init):
    y = o * lax.rsqrt(jnp.mean(o * o, axis=-1, keepdims=True) + EPS)
    return y * g_ref[...] * (1.0 - lam_init)


def _attn_prompt_kernel(tab_ref, q_ref, k_ref, v_ref, bias_ref, lamv_ref, subg_ref, o_ref,
                        kb, vb, m_s, l_s, acc_s, *, lam_init):
    h = pl.program_id(1)
    qi = pl.program_id(2)
    tq = q_ref.shape[1]

    @pl.when(qi == 0)
    def _():
        kb[...] = k_ref[0].astype(BF16)
        vb[...] = v_ref[0].astype(BF16)

    q = q_ref[0]
    lane = lax.broadcasted_iota(jnp.int32, q.shape, 1)
    qm = (jnp.where(lane < A_QK_DIM, q, 0.0).astype(BF16),
          jnp.where(lane >= A_QK_DIM, q, 0.0).astype(BF16))
    m_s[...] = jnp.full(m_s.shape, NEG_INF, F32)
    l_s[...] = jnp.zeros(l_s.shape, F32)
    acc_s[...] = jnp.zeros(acc_s.shape, F32)

    def step(j, bias):
        start = pl.multiple_of(j * tq, tq)
        kj = kb[pl.ds(start, tq), :]
        vj = vb[pl.ds(start, tq), :]
        prev = [(m_s[m], l_s[m], acc_s[m]) for m in range(2)]
        new = []
        for m in range(2):
            m_prev, l_prev, acc_prev = prev[m]
            s = _dot_nt(qm[m], kj) + bias
            m_new = jnp.maximum(m_prev, jnp.max(s, axis=1, keepdims=True))
            p = jnp.exp(s - m_new)
            alpha = jnp.exp(m_prev - m_new)
            new.append((m_new, alpha * l_prev + jnp.sum(p, axis=1, keepdims=True),
                        alpha * acc_prev + _dot(p.astype(BF16), vj)))
        for m in range(2):
            m_s[m], l_s[m], acc_s[m] = new[m]

    far = tab_ref[h, REL_BUCKETS - 1]

    def far_body(j, carry):
        step(j, far)
        return carry

    lax.fori_loop(0, jnp.maximum(qi - 1, 0), far_body, 0)

    @pl.when(qi >= 1)
    def _():
        step(qi - 1, bias_ref[0, 1])

    step(qi, bias_ref[0, 0])

    lam = _lam(lamv_ref, lam_init)
    o = acc_s[0] * (1.0 / l_s[0]) - lam * (acc_s[1] * (1.0 / l_s[1]))
    o_ref[0] = _subln(o, subg_ref, lam_init).astype(o_ref.dtype)


def _attn_prompt(zr3, bias_p, tab_t, lamv, subg, lam_init):
    b, s, _ = zr3.shape
    tq = ATT_TQ
    kern = functools.partial(_attn_prompt_kernel, lam_init=lam_init)
    return pl.pallas_call(
        kern,
        grid=(b, A_HEADS, s // tq),
        in_specs=[pl.BlockSpec(memory_space=pltpu.SMEM),
                  pl.BlockSpec((1, tq, 128), lambda bi, h, qi: (bi, qi, COL_AQ // 128 + h)),
                  pl.BlockSpec((1, s, 128), lambda bi, h, qi: (bi, 0, COL_AK // 128 + h)),
                  pl.BlockSpec((1, s, 128), lambda bi, h, qi: (bi, 0, COL_AV // 128 + h)),
                  pl.BlockSpec((1, 2, tq, tq), lambda bi, h, qi: (h, 0, 0, 0)),
                  pl.BlockSpec((4, A_QK_DIM), lambda bi, h, qi: (0, 0)),
                  pl.BlockSpec((1, A_V_DIM), lambda bi, h, qi: (0, 0))],
        out_specs=pl.BlockSpec((1, tq, 128), lambda bi, h, qi: (bi, qi, h)),
        out_shape=jax.ShapeDtypeStruct((b, s, A_WIDTH), BF16),
        scratch_shapes=[pltpu.VMEM((s, 128), BF16), pltpu.VMEM((s, 128), BF16),
                        pltpu.VMEM((2, tq, 1), F32), pltpu.VMEM((2, tq, 1), F32),
                        pltpu.VMEM((2, tq, 128), F32)],
        compiler_params=_cparams(3),
        name="attn_prompt",
    )(tab_t, zr3, zr3, zr3, bias_p, lamv, subg)


def _attn_sample_kernel(pt_ref, q_ref, kn_ref, vn_ref, bias_ref, lamv_ref, subg_ref, *rest,
                        lam_init, n_pg, t_dec):
    k_refs = rest[:n_pg]
    v_refs = rest[n_pg:2 * n_pg]
    o_ref = rest[2 * n_pg]
    q_s, m_s, l_s, acc_s = rest[2 * n_pg + 1:]
    j = pl.program_id(1)
    last = pl.num_programs(1) - 1
    rows = 2 * t_dec

    @pl.when(j == 0)
    def _():
        q = q_ref[0]
        lane = lax.broadcasted_iota(jnp.int32, (t_dec, 128), 1)
        for h in range(A_HEADS):
            qh = q[:, h * 128:(h + 1) * 128]
            q_s[h * rows:h * rows + t_dec, :] = jnp.where(lane < A_QK_DIM, qh, 0.0)
            q_s[h * rows + t_dec:(h + 1) * rows, :] = jnp.where(lane >= A_QK_DIM, qh, 0.0)
        m_s[...] = jnp.full(m_s.shape, NEG_INF, F32)
        l_s[...] = jnp.zeros(l_s.shape, F32)
        acc_s[...] = jnp.zeros(acc_s.shape, F32)

    qb = q_s[...].astype(BF16)

    def update(ks, vs, biases):
        ss = [_dot_nt(qb, kb) + bias for kb, bias in zip(ks, biases)]
        smax = functools.reduce(jnp.maximum, ss)
        m_prev = m_s[...]
        m_new = jnp.maximum(m_prev, jnp.max(smax, axis=1, keepdims=True))
        ps = [jnp.exp(s - m_new) for s in ss]
        alpha = jnp.exp(m_prev - m_new)
        l_s[...] = alpha * l_s[...] + jnp.sum(functools.reduce(jnp.add, ps), axis=1, keepdims=True)
        pv = functools.reduce(jnp.add, [_dot(p.astype(BF16), vb) for p, vb in zip(ps, vs)])
        acc_s[...] = alpha * acc_s[...] + pv
        m_s[...] = m_new

    biases = [bias_ref[0]] * (n_pg - 1) + [jnp.where(j == last, bias_ref[1], bias_ref[0])]
    update([r[0, 0].astype(BF16) for r in k_refs], [r[0, 0].astype(BF16) for r in v_refs], biases)

    @pl.when(j == last)
    def _():
        nk = kn_ref.shape[1]
        update([kn_ref[0].astype(BF16)], [vn_ref[0].astype(BF16)], [bias_ref[2][:, :nk]])
        lam = _lam(lamv_ref, lam_init)
        on = acc_s[...] * (1.0 / l_s[...])
        for h in range(A_HEADS):
            o = on[h * rows:h * rows + t_dec] - lam * on[h * rows + t_dec:(h + 1) * rows]
            o_ref[0, :, h * 128:(h + 1) * 128] = _subln(o, subg_ref, lam_init).astype(o_ref.dtype)


def _attn_sample(zrs3, kn2d, vn2d, cache_k2d, cache_v2d, page_table, bias_s, lamv, subg,
                 layer, lam_init):
    bd, t_dec, _ = zrs3.shape
    n_pages = page_table.shape[1]
    n_pg = SAMPLE_PAGES
    rows = 2 * t_dec * A_HEADS
    pcols = PAGE_SIZE * A_HEADS

    def page_spec(p):
        return pl.BlockSpec((1, 1, pcols, 128),
                            lambda b, j, pt: (layer, pt[b, j * n_pg + p], 0, 0))

    kern = functools.partial(_attn_sample_kernel, lam_init=lam_init, n_pg=n_pg, t_dec=t_dec)
    grid_spec = pltpu.PrefetchScalarGridSpec(
        num_scalar_prefetch=1,
        grid=(bd, n_pages // n_pg),
        in_specs=[pl.BlockSpec((1, t_dec, A_WIDTH), lambda b, j, pt: (b, 0, 0)),
                  pl.BlockSpec((1,) + kn2d.shape[1:], lambda b, j, pt: (b, 0, 0)),
                  pl.BlockSpec((1,) + vn2d.shape[1:], lambda b, j, pt: (b, 0, 0)),
                  pl.BlockSpec((3, rows, pcols), lambda b, j, pt: (0, 0, 0)),
                  pl.BlockSpec((4, A_QK_DIM), lambda b, j, pt: (0, 0)),
                  pl.BlockSpec((1, A_V_DIM), lambda b, j, pt: (0, 0))]
                 + [page_spec(p) for p in range(n_pg)] * 2,
        out_specs=pl.BlockSpec((1, t_dec, A_WIDTH), lambda b, j, pt: (b, 0, 0)),
        scratch_shapes=[pltpu.VMEM((rows, 128), F32), pltpu.VMEM((rows, 1), F32),
                        pltpu.VMEM((rows, 1), F32), pltpu.VMEM((rows, 128), F32)],
    )
    return pl.pallas_call(
        kern,
        grid_spec=grid_spec,
        out_shape=jax.ShapeDtypeStruct((bd, t_dec, A_WIDTH), BF16),
        compiler_params=_cparams(2),
        name="attn_sample",
    )(page_table, zrs3, kn2d, vn2d, bias_s, lamv, subg,
      *([cache_k2d] * n_pg), *([cache_v2d] * n_pg))


def _gelu(x):
    return 0.5 * x * (1.0 + jnp.tanh(math.sqrt(2.0 / math.pi) * (x + 0.044715 * (x * x * x))))


def _chunk_mlp_kernel(u_ref, v_ref, w_ref, bias_ref, g_ref, b_ref, y_ref, vb_ref):
    n_chunks = u_ref.shape[0] // B_CHUNK
    lane_grp = lax.broadcasted_iota(jnp.int32, (B_CHUNK, B_WIDTH), 1) // (B_WIDTH // B_GROUPS)
    for c in range(n_chunks):
        sl = slice(c * B_CHUNK, (c + 1) * B_CHUNK)
        gv = _gelu(v_ref[sl, :])
        xc = gv - jnp.mean(gv, axis=-1, keepdims=True)
        vb = xc * lax.rsqrt(jnp.mean(xc * xc, axis=-1, keepdims=True) + EPS) * g_ref[...] + b_ref[...]
        vb_ref[sl, :] = vb
        vbb = vb.astype(BF16)
        mixed = bias_ref[...]
        for g in range(B_GROUPS):
            mixed = mixed + jnp.where(lane_grp == g, _dot(w_ref[g], vbb), 0.0)
        y_ref[sl, :] = (_gelu(u_ref[sl, :]) * mixed).astype(y_ref.dtype)


def _chunk_mlp(zr, w_eff, bias_eff, ln_g, ln_b):
    m = zr.shape[0]
    tm = min(m, 512)
    return pl.pallas_call(
        _chunk_mlp_kernel,
        grid=(m // tm,),
        in_specs=[pl.BlockSpec((tm, B_WIDTH), lambda i: (i, COL_BU // B_WIDTH)),
                  pl.BlockSpec((tm, B_WIDTH), lambda i: (i, COL_BV // B_WIDTH)),
                  pl.BlockSpec((B_GROUPS, B_CHUNK, B_CHUNK), lambda i: (0, 0, 0)),
                  pl.BlockSpec((B_CHUNK, B_WIDTH), lambda i: (0, 0)),
                  pl.BlockSpec((1, B_WIDTH), lambda i: (0, 0)),
                  pl.BlockSpec((1, B_WIDTH), lambda i: (0, 0))],
        out_specs=[pl.BlockSpec((tm, B_WIDTH), lambda i: (i, 0)),
                   pl.BlockSpec((tm, B_WIDTH), lambda i: (i, 0))],
        out_shape=[jax.ShapeDtypeStruct((m, B_WIDTH), BF16),
                   jax.ShapeDtypeStruct((m, B_WIDTH), F32)],
        compiler_params=_cparams(1),
        name="chunk_mlp",
    )(zr, zr, w_eff, bias_eff, ln_g, ln_b)


def _pool_kernel(x_ref, w_ref, scale_ref, y_ref, *, prefix, pos0):
    x = x_ref[0]
    rows = x.shape[0]
    row = lax.broadcasted_iota(jnp.int32, x.shape, 0)
    grp = lax.broadcasted_iota(jnp.int32, x.shape, 1) // C_GROUP_DIM

    def shifted(a, k):
        return jnp.where(row >= k, pltpu.roll(a, k, 0), 0.0)

    sums = []
    acc = x
    for k in (1, 2, 4, 8):
        acc = acc + shifted(acc, k)
        sums.append(acc)
    total = sums[3]
    win = jnp.full(x.shape, POOL_WINDOWS[3], jnp.int32)
    for g in range(3):
        total = jnp.where(grp == g, sums[g], total)
        win = jnp.where(grp == g, POOL_WINDOWS[g], win)
    pos = pos0 + row - prefix
    cnt = jnp.clip(pos + 1, 1, win).astype(F32)
    d = total / cnt - x
    y = _dot(d.astype(BF16), w_ref[...]) * scale_ref[...]
    y_ref[0] = y.astype(y_ref.dtype)
    del rows


def _pool_mix(xx, col_block, w_bd, scale, prefix, pos0):
    b, rows, _ = xx.shape
    kern = functools.partial(_pool_kernel, prefix=prefix, pos0=pos0)
    return pl.pallas_call(
        kern,
        grid=(b,),
        in_specs=[pl.BlockSpec((1, rows, C_WIDTH), lambda i: (i, 0, col_block)),
                  pl.BlockSpec((C_WIDTH, C_WIDTH), lambda i: (0, 0)),
                  pl.BlockSpec((1, C_WIDTH), lambda i: (0, 0))],
        out_specs=pl.BlockSpec((1, rows, C_WIDTH), lambda i: (i, 0, 0)),
        out_shape=jax.ShapeDtypeStruct((b, rows, C_WIDTH), BF16),
        compiler_params=_cparams(1),
        name="pool_mix",
    )(xx, w_bd, scale)


def _log_sigmoid(x):
    return jnp.minimum(x, 0.0) - jnp.log1p(jnp.exp(-jnp.abs(x)))


def _mlstm_kernel(q_ref, k_ref, v_ref, o_ref, g_ref, gb_ref, ng_ref, c0_ref, n0_ref, m0_ref,
                  y_ref, c_out, n_out, m_out, c_s, n_s, m_s, *, t_valid):
    ci = pl.program_id(1)
    nb, chunk = q_ref.shape[0], q_ref.shape[1]

    @pl.when(ci == 0)
    def _():
        c_s[...] = c0_ref[...]
        n_s[...] = n0_ref[...]
        for bi in range(nb):
            for h in range(D_HEADS):
                m_s[bi, h] = m0_ref[bi, :, h:h + 1]

    row = lax.broadcasted_iota(jnp.int32, (chunk, 128), 0)
    rr = lax.broadcasted_iota(jnp.int32, (chunk, chunk), 0)
    cc = lax.broadcasted_iota(jnp.int32, (chunk, chunk), 1)
    eye = rr == cc
    tril = cc <= rr

    def as_row(col):
        return jnp.sum(jnp.where(eye, col, 0.0), axis=0, keepdims=True)

    pairs = [(bi, h) for bi in range(nb) for h in range(D_HEADS)]
    state = {(bi, h): (c_s[bi, h], n_s[bi, h], m_s[bi, h]) for bi, h in pairs}
    gates = {}
    for bi in range(nb):
        g = g_ref[bi] + gb_ref[...]
        lf = _log_sigmoid(g)
        if t_valid < chunk:
            g = jnp.where(row < t_valid, g, NEG_INF)
            lf = jnp.where(row < t_valid, lf, 0.0)
        cum = lf
        k = 1
        while k < chunk:
            cum = cum + jnp.where(row >= k, pltpu.roll(cum, k, 0), 0.0)
            k *= 2
        gates[bi] = (g, cum)

    results = {}
    for bi, h in pairs:
        sl = slice(h * D_HEAD_DIM, (h + 1) * D_HEAD_DIM)
        g, cum = gates[bi]
        c_prev, n_prev, m_prev = state[bi, h]
        qh = q_ref[bi, :, sl]
        kh = k_ref[bi, :, sl] * (D_HEAD_DIM ** -0.5)
        vh = v_ref[bi, :, sl]
        qb, kb, vb = qh.astype(BF16), kh.astype(BF16), vh.astype(BF16)
        ig_c = g[:, h:h + 1]
        b_c = cum[:, D_HEADS + h:D_HEADS + h + 1]
        ig_r, b_r = as_row(ig_c), as_row(b_c)
        dm = jnp.where(tril, b_c - b_r + ig_r, NEG_INF)
        inter = b_c + m_prev
        mt = jnp.maximum(inter, jnp.max(dm, axis=1, keepdims=True))
        qk = _dot_nt(qb, kb) * jnp.exp(dm - mt)
        winter = jnp.exp(inter - mt)
        num = winter * _dot(qb, c_prev.astype(BF16)) + _dot(qk.astype(BF16), vb)
        den = (winter * jnp.sum(qh * n_prev, axis=1, keepdims=True)
               + jnp.sum(qk, axis=1, keepdims=True))
        hh = num / jnp.maximum(jnp.abs(den), jnp.exp(-mt))
        y = hh * lax.rsqrt(jnp.mean(hh * hh, axis=-1, keepdims=True) + EPS) * ng_ref[...]
        y = (y * jax.nn.sigmoid(o_ref[bi, :, sl])).astype(y_ref.dtype)
        b_last = b_c[chunk - 1:chunk, :]
        a = b_last - b_c + ig_c
        m_new = jnp.maximum(b_last + m_prev, jnp.max(a, axis=0, keepdims=True))
        ws = jnp.exp(a - m_new)
        decay = jnp.exp(b_last + m_prev - m_new)
        kw = ws * kh
        c_new = decay * c_prev + _dot_tn(kw.astype(BF16), vb)
        n_new = decay * n_prev + jnp.sum(kw, axis=0, keepdims=True)
        results[bi, h] = (y, c_new, n_new, m_new)

    for bi, h in pairs:
        y, c_new, n_new, m_new = results[bi, h]
        y_ref[bi, :, h * D_HEAD_DIM:(h + 1) * D_HEAD_DIM] = y
        c_s[bi, h] = c_new
        n_s[bi, h] = n_new
        m_s[bi, h] = m_new

    @pl.when(ci == pl.num_programs(1) - 1)
    def _():
        c_out[...] = c_s[...]
        n_out[...] = n_s[...]
        for bi in range(nb):
            for h in range(D_HEADS):
                m_out[bi, :, h:h + 1] = m_s[bi, h]


def _mlstm(src, col0, gates_block, chunk, t_valid, gate_bias, norm_g, c0, n0, m0):
    b, t, _ = src.shape
    nb = math.gcd(b, MLSTM_BATCH)
    kern = functools.partial(_mlstm_kernel, t_valid=t_valid)

    def col(cb):
        return pl.BlockSpec((nb, chunk, D_WIDTH), lambda bi, ci: (bi, ci, cb))

    hd = D_HEAD_DIM
    return pl.pallas_call(
        kern,
        grid=(b // nb, t // chunk),
        in_specs=[col(col0), col(col0 + 1), col(col0 + 2), col(col0 + 3),
                  pl.BlockSpec((nb, chunk, 128), lambda bi, ci: (bi, ci, gates_block)),
                  pl.BlockSpec((1, 128), lambda bi, ci: (0, 0)),
                  pl.BlockSpec((1, hd), lambda bi, ci: (0, 0)),
                  pl.BlockSpec((nb, D_HEADS, hd, hd), lambda bi, ci: (bi, 0, 0, 0)),
                  pl.BlockSpec((nb, D_HEADS, 1, hd), lambda bi, ci: (bi, 0, 0, 0)),
                  pl.BlockSpec((nb, 1, D_HEADS), lambda bi, ci: (bi, 0, 0))],
        out_specs=[pl.BlockSpec((nb, chunk, D_WIDTH), lambda bi, ci: (bi, ci, 0)),
                   pl.BlockSpec((nb, D_HEADS, hd, hd), lambda bi, ci: (bi, 0, 0, 0)),
                   pl.BlockSpec((nb, D_HEADS, 1, hd), lambda bi, ci: (bi, 0, 0, 0)),
                   pl.BlockSpec((nb, 1, D_HEADS), lambda bi, ci: (bi, 0, 0))],
        out_shape=[jax.ShapeDtypeStruct((b, t, D_WIDTH), BF16),
                   jax.ShapeDtypeStruct((b, D_HEADS, hd, hd), F32),
                   jax.ShapeDtypeStruct((b, D_HEADS, 1, hd), F32),
                   jax.ShapeDtypeStruct((b, 1, D_HEADS), F32)],
        scratch_shapes=[pltpu.VMEM((nb, D_HEADS, hd, hd), F32),
                        pltpu.VMEM((nb, D_HEADS, 1, hd), F32),
                        pltpu.VMEM((nb, D_HEADS, 1, 1), F32)],
        compiler_params=_cparams(2),
        name="mlstm",
    )(src, src, src, src, src, gate_bias, norm_g, c0, n0, m0)


def _merge_kernel(x_ref, g_ref, ya_ref, yb_ref, yc_ref, yd_ref, wa_ref, wb_ref, wc_ref, wd_ref,
                  wo_ref, o_ref):
    def gate(i):
        return g_ref[:, i * D_MODEL:(i + 1) * D_MODEL].astype(F32)

    merged = gate(0) * _dot(ya_ref[...], wa_ref[...])
    merged = merged + gate(1) * _dot(yb_ref[...], wb_ref[...])
    merged = merged + gate(2) * _dot(yc_ref[...], wc_ref[...])
    merged = merged + gate(3) * _dot(yd_ref[...], wd_ref[...])
    o_ref[...] = x_ref[...] + _dot(merged.astype(BF16), wo_ref[...])


def _merge(x, gates, ya, yb, yc, yd, wa, wb, wc, wd, wo):
    m = x.shape[0]
    tm = min(m, 512)

    def rows(width):
        return pl.BlockSpec((tm, width), lambda i: (i, 0))

    def full(arr):
        return pl.BlockSpec(arr.shape, lambda i: (0, 0))

    return pl.pallas_call(
        _merge_kernel,
        grid=(m // tm,),
        in_specs=[rows(D_MODEL), rows(GZ_WIDTH), rows(A_WIDTH), rows(B_WIDTH), rows(C_WIDTH),
                  rows(D_WIDTH), full(wa), full(wb), full(wc), full(wd), full(wo)],
        out_specs=rows(D_MODEL),
        out_shape=jax.ShapeDtypeStruct((m, D_MODEL), F32),
        compiler_params=_cparams(1),
        name="merge",
    )(x, gates, ya, yb, yc, yd, wa, wb, wc, wd, wo)


def _ffn_kernel(x_ref, g_ref, w1_ref, w2_ref, o_ref, h_s, acc_s):
    j = pl.program_id(1)

    @pl.when(j == 0)
    def _():
        x = x_ref[...]
        y = x * lax.rsqrt(jnp.mean(x * x, axis=-1, keepdims=True) + EPS)
        h_s[...] = (y * g_ref[...]).astype(BF16)
        acc_s[...] = jnp.zeros(acc_s.shape, F32)

    a = jnp.maximum(_dot(h_s[...], w1_ref[...]), 0.0)
    acc_s[...] += _dot((a * a).astype(BF16), w2_ref[...])

    @pl.when(j == pl.num_programs(1) - 1)
    def _():
        o_ref[...] = x_ref[...] + acc_s[...]


def _ffn(x, g, w1, w2):
    m = x.shape[0]
    tm, tf = min(m, 1024), 512
    return pl.pallas_call(
        _ffn_kernel,
        grid=(m // tm, D_FF // tf),
        in_specs=[pl.BlockSpec((tm, D_MODEL), lambda i, j: (i, 0)),
                  pl.BlockSpec((1, D_MODEL), lambda i, j: (0, 0)),
                  pl.BlockSpec((D_MODEL, tf), lambda i, j: (0, j)),
                  pl.BlockSpec((tf, D_MODEL), lambda i, j: (j, 0))],
        out_specs=pl.BlockSpec((tm, D_MODEL), lambda i, j: (i, 0)),
        out_shape=jax.ShapeDtypeStruct((m, D_MODEL), F32),
        scratch_shapes=[pltpu.VMEM((tm, D_MODEL), BF16), pltpu.VMEM((tm, D_MODEL), F32)],
        compiler_params=_cparams(2),
        name="ffn",
    )(x, g.reshape(1, D_MODEL), w1, w2)


def _layer_weights(p, l):
    w_in = p["w_in"][l]
    wr = jnp.concatenate(
        [w_in[:, :3072], w_in[:, 3080:3336], w_in[:, 3072:3080],
         jnp.zeros((D_MODEL, ZR_WIDTH - 3336), F32)], axis=1).astype(BF16)
    wg = w_in[:, 3336:].astype(BF16)
    reps = A_WIDTH // A_QK_DIM
    gains = jnp.stack([jnp.tile(p["q_norm_g"][l], reps) * (A_QK_DIM ** -0.5),
                       jnp.tile(p["k_norm_g"][l], reps)]).reshape(2, 1, A_WIDTH)
    grp = jnp.arange(A_WIDTH) // A_QK_DIM
    ones_bd = (grp[:, None] == grp[None, :]).astype(BF16)
    lamv = jnp.stack([p["lam_q1"][l], p["lam_k1"][l], p["lam_q2"][l], p["lam_k2"][l]])
    gate_bias = jnp.zeros((1, 128), F32)
    gate_bias = gate_bias.at[0, :D_HEADS].set(p["d_i_bias"][l])
    gate_bias = gate_bias.at[0, D_HEADS:2 * D_HEADS].set(p["d_f_bias"][l])
    c_bd = jnp.zeros((C_WIDTH, C_WIDTH), F32)
    for g in range(4):
        sl = slice(g * C_GROUP_DIM, (g + 1) * C_GROUP_DIM)
        c_bd = c_bd.at[sl, sl].set(p["c_lin"][l][g])
    return dict(
        norm1_g=p["norm1_g"][l], norm2_g=p["norm2_g"][l], wr=wr, wg=wg, gains=gains,
        ones_bd=ones_bd, lamv=lamv, subg=p["subln_g"][l].reshape(1, A_V_DIM),
        b_ln_g=p["b_ln_g"][l].reshape(1, B_WIDTH), b_ln_b=p["b_ln_b"][l].reshape(1, B_WIDTH),
        b_ws=p["b_ws"][l], b_bias=p["b_bias"][l],
        c_bd=c_bd.astype(BF16), c_scale=p["c_scale"][l].reshape(1, C_WIDTH),
        gate_bias=gate_bias, d_norm_g=p["d_norm_g"][l].reshape(1, D_HEAD_DIM),
        w_pa=p["w_pa"][l].astype(BF16), w_pb=p["w_pb"][l].astype(BF16),
        w_pc=p["w_pc"][l].astype(BF16), w_pd=p["w_pd"][l].astype(BF16),
        w_out=p["w_out"][l].astype(BF16), w_ff1=p["w_ff1"][l].astype(BF16),
        w_ff2=p["w_ff2"][l].astype(BF16),
        lam_init=0.8 - 0.6 * math.exp(-0.3 * l),
    )


def _chunk_weights(w, t):
    length = min(t, B_CHUNK)
    ws = jnp.tril(w["b_ws"][:, :length, :length])
    bias = jnp.transpose(w["b_bias"][:, :length])
    reps = B_CHUNK // length
    if reps > 1:
        eye = jnp.eye(reps, dtype=F32)
        ws = jax.vmap(lambda a: jnp.kron(eye, a))(ws)
        bias = jnp.tile(bias, (reps, 1))
    return ws.astype(BF16), jnp.repeat(bias, B_WIDTH // B_GROUPS, axis=1)


def _finish_layer(x2, w, gates, zr, ya, yc, yd, t):
    w_eff, bias_eff = _chunk_weights(w, t)
    yb, vb = _chunk_mlp(zr, w_eff, bias_eff, w["b_ln_g"], w["b_ln_b"])
    x2 = _merge(x2, gates, ya, yb, yc, yd, w["w_pa"], w["w_pb"], w["w_pc"], w["w_pd"], w["w_out"])
    return _ffn(x2, w["norm2_g"], w["w_ff1"], w["w_ff2"]), vb


def _prompt_layer(x2, w, b, s, bias_p, tab_t):
    m = b * s
    h = _rms_cast(x2, w["norm1_g"])
    gates = _gate_proj(h, w["wg"])
    zr = _zr_proj(h, w["wr"], w["gains"], w["ones_bd"])
    zr3 = zr.reshape(b, s, ZR_WIDTH)
    ya = _attn_prompt(zr3, bias_p, tab_t, w["lamv"], w["subg"], w["lam_init"]).reshape(m, A_WIDTH)
    yc = _pool_mix(zr3, COL_CX // C_WIDTH, w["c_bd"], w["c_scale"], 0, 0).reshape(m, C_WIDTH)
    hd = D_HEAD_DIM
    yd, c_new, n_new, m_new = _mlstm(
        zr3, COL_DQ // D_WIDTH, COL_DG // 128, min(s, 128), min(s, 128), w["gate_bias"],
        w["d_norm_g"], jnp.zeros((b, D_HEADS, hd, hd), F32), jnp.zeros((b, D_HEADS, 1, hd), F32),
        jnp.zeros((b, 1, D_HEADS), F32))
    x2, _ = _finish_layer(x2, w, gates, zr, ya, yc, yd.reshape(m, D_WIDTH), s)
    outs = (zr3[:, :, COL_AK:COL_AK + A_WIDTH].reshape(b, s, A_HEADS, 2 * A_QK_DIM),
            zr3[:, :, COL_AV:COL_AV + A_WIDTH].reshape(b, s, A_HEADS, A_V_DIM),
            zr3[:, s - POOL_BUF:, COL_CX:COL_CX + C_WIDTH],
            c_new, n_new.reshape(b, D_HEADS, hd), m_new.reshape(b, D_HEADS))
    return x2, outs


def _sample_layer(x2, w, l, bd, t, cache_k2d, cache_v2d, page_table, bias_s, pool0, c0, n0, m0):
    m = bd * t
    past = page_table.shape[1] * PAGE_SIZE
    h = _rms_cast(x2, w["norm1_g"])
    gates = _gate_proj(h, w["wg"])
    zr = _zr_proj(h, w["wr"], w["gains"], w["ones_bd"])
    zr3 = zr.reshape(bd, t, ZR_WIDTH)
    kn = zr3[:, :, COL_AK:COL_AK + A_WIDTH]
    vn = zr3[:, :, COL_AV:COL_AV + A_WIDTH]
    ya = _attn_sample(zr3, kn.reshape(bd, t * A_HEADS, 128), vn.reshape(bd, t * A_HEADS, 128),
                      cache_k2d, cache_v2d, page_table, bias_s, w["lamv"], w["subg"], l,
                      w["lam_init"]).reshape(m, A_WIDTH)
    cx = zr3[:, :, COL_CX:COL_CX + C_WIDTH]
    prefix = POOL_BUF + 1
    rows = -(-(prefix + t) // 8) * 8
    xx = jnp.concatenate([jnp.zeros((bd, 1, C_WIDTH), F32), pool0, cx,
                          jnp.zeros((bd, rows - prefix - t, C_WIDTH), F32)], axis=1)
    yc = _pool_mix(xx, 0, w["c_bd"], w["c_scale"], prefix, past)[:, prefix:prefix + t]
    yc = yc.reshape(m, C_WIDTH)
    chunk = -(-t // 8) * 8
    dsrc = jnp.pad(zr3[:, :, COL_DQ:COL_DG + 128], ((0, 0), (0, chunk - t), (0, 0)))
    hd = D_HEAD_DIM
    yd, c_new, n_new, m_new = _mlstm(
        dsrc, 0, (COL_DG - COL_DQ) // 128, chunk, t, w["gate_bias"], w["d_norm_g"],
        c0, n0.reshape(bd, D_HEADS, 1, hd), m0.reshape(bd, 1, D_HEADS))
    yd = yd[:, :t].reshape(m, D_WIDTH)
    x2, vb = _finish_layer(x2, w, gates, zr, ya, yc, yd, t)
    outs = (kn.reshape(bd, t, A_HEADS, 2 * A_QK_DIM), vn.reshape(bd, t, A_HEADS, A_V_DIM),
            vb.reshape(bd, t, B_WIDTH), jnp.concatenate([pool0, cx], axis=1)[:, -POOL_BUF:],
            c_new, n_new.reshape(bd, D_HEADS, hd), m_new.reshape(bd, D_HEADS))
    return x2, outs


def kernel(x_prompt, x_sample, cache_k, cache_v, page_table, state_pool, state_C, state_n, state_m, rel_bias, norm1_g, norm2_g, w_in, q_norm_g, k_norm_g, lam_q1, lam_k1, lam_q2, lam_k2, subln_g, b_ln_g, b_ln_b, b_ws, b_bias, c_lin, c_scale, d_i_bias, d_f_bias, d_norm_g, w_pa, w_pb, w_pc, w_pd, w_out, w_ff1, w_ff2):
    p = dict(norm1_g=norm1_g, norm2_g=norm2_g, w_in=w_in, q_norm_g=q_norm_g, k_norm_g=k_norm_g,
             lam_q1=lam_q1, lam_k1=lam_k1, lam_q2=lam_q2, lam_k2=lam_k2, subln_g=subln_g,
             b_ln_g=b_ln_g, b_ln_b=b_ln_b, b_ws=b_ws, b_bias=b_bias, c_lin=c_lin, c_scale=c_scale,
             d_i_bias=d_i_bias, d_f_bias=d_f_bias, d_norm_g=d_norm_g, w_pa=w_pa, w_pb=w_pb,
             w_pc=w_pc, w_pd=w_pd, w_out=w_out, w_ff1=w_ff1, w_ff2=w_ff2)
    depth = w_in.shape[0]
    bp, sp, _ = x_prompt.shape
    bd, td, _ = x_sample.shape
    n_phys = cache_k.shape[1]
    cache_k2d = cache_k.reshape(depth, n_phys, PAGE_SIZE * A_HEADS, 128)
    cache_v2d = cache_v.reshape(depth, n_phys, PAGE_SIZE * A_HEADS, 128)
    bias_p, bias_s = _bias_tiles(rel_bias, td)
    tab_t = rel_bias.T

    xp = x_prompt.reshape(bp * sp, D_MODEL)
    xs = x_sample.reshape(bd * td, D_MODEL)
    prompt_outs, sample_outs = [], []
    for l in range(depth):
        w = _layer_weights(p, l)
        xp, po = _prompt_layer(xp, w, bp, sp, bias_p, tab_t)
        xs, so = _sample_layer(xs, w, l, bd, td, cache_k2d, cache_v2d, page_table, bias_s,
                               state_pool[l], state_C[l], state_n[l], state_m[l])
        prompt_outs.append(po)
        sample_outs.append(so)

    def stack(outs, i):
        return jnp.stack([o[i] for o in outs])

    return (xp.reshape(bp, sp, D_MODEL), xs.reshape(bd, td, D_MODEL),
            stack(prompt_outs, 0), stack(prompt_outs, 1), stack(sample_outs, 0), stack(sample_outs, 1),
            stack(sample_outs, 2), stack(prompt_outs, 2), stack(sample_outs, 3),
            stack(prompt_outs, 3), stack(prompt_outs, 4), stack(prompt_outs, 5),
            stack(sample_outs, 4), stack(sample_outs, 5), stack(sample_outs, 6))
```

```python
import functools
import math

import jax
import jax.numpy as jnp
from jax import lax
from jax.experimental import pallas as pl
from jax.experimental.pallas import tpu as pltpu

F32 = jnp.float32
BF16 = jnp.bfloat16
NEG_INF = float("-inf")

D_MODEL = 1024
A_HEADS = 4
A_QK_DIM = 64
A_V_DIM = 128
A_WIDTH = 512
REL_BUCKETS = 32
REL_MAX_DIST = 128
PAGE_SIZE = 128
B_GROUPS = 4
B_WIDTH = 256
B_CHUNK = 128
C_WIDTH = 256
C_GROUP_DIM = 64
POOL_WINDOWS = (2, 4, 8, 16)
POOL_BUF = 15
D_HEADS = 4
D_WIDTH = 256
D_HEAD_DIM = 64
D_FF = 4096
N_BRANCH = 4
EPS = 1e-6

ZR_WIDTH = 3584
COL_AQ, COL_AK, COL_AV = 0, 512, 1024
COL_BU, COL_BV, COL_CX = 1536, 1792, 2048
COL_DQ, COL_DK, COL_DV, COL_DO = 2304, 2560, 2816, 3072
COL_DGI, COL_DGF = 3328, 3456
GZ_WIDTH = N_BRANCH * D_MODEL
EMT_CAP = 80.0

ATT_TQ = 512
SAMPLE_PAGES = 16
MLSTM_BATCH = 4
VMEM_LIMIT = 56 * 1024 * 1024


def _cparams(n_axes):
    return pltpu.CompilerParams(dimension_semantics=("arbitrary",) * n_axes,
                                vmem_limit_bytes=VMEM_LIMIT)


def _dot(a, b):
    return jnp.dot(a, b, preferred_element_type=F32)


def _dot_nt(a, b):
    return lax.dot_general(a, b, (((1,), (1,)), ((), ())), preferred_element_type=F32)


def _dot_tn(a, b):
    return lax.dot_general(a, b, (((0,), (0,)), ((), ())), preferred_element_type=F32)


def _rms_kernel(x_ref, g_ref, o_ref):
    x = x_ref[...]
    y = x * lax.rsqrt(jnp.mean(x * x, axis=-1, keepdims=True) + EPS)
    o_ref[...] = (y * g_ref[...]).astype(o_ref.dtype)


def _rms_cast(x, g):
    m = x.shape[0]
    tm = min(m, 512)
    return pl.pallas_call(
        _rms_kernel,
        grid=(m // tm,),
        in_specs=[pl.BlockSpec((tm, D_MODEL), lambda i: (i, 0)),
                  pl.BlockSpec((1, D_MODEL), lambda i: (0, 0))],
        out_specs=pl.BlockSpec((tm, D_MODEL), lambda i: (i, 0)),
        out_shape=jax.ShapeDtypeStruct((m, D_MODEL), BF16),
        compiler_params=_cparams(1),
        name="rms_cast",
    )(x, g.reshape(1, D_MODEL))


def _gate_kernel(h_ref, w_ref, o_ref):
    o_ref[...] = jax.nn.sigmoid(_dot(h_ref[...], w_ref[...])).astype(o_ref.dtype)


def _gate_proj(h, wg):
    m = h.shape[0]
    tm, tn = min(m, 1024), 512
    return pl.pallas_call(
        _gate_kernel,
        grid=(m // tm, GZ_WIDTH // tn),
        in_specs=[pl.BlockSpec((tm, D_MODEL), lambda i, j: (i, 0)),
                  pl.BlockSpec((D_MODEL, tn), lambda i, j: (0, j))],
        out_specs=pl.BlockSpec((tm, tn), lambda i, j: (i, j)),
        out_shape=jax.ShapeDtypeStruct((m, GZ_WIDTH), BF16),
        compiler_params=_cparams(2),
        name="gate_proj",
    )(h, wg)


def _zr_kernel(h_ref, w_ref, gain_ref, ones_ref, o_ref, kout_ref, vout_ref):
    j = pl.program_id(1)
    tm = h_ref.shape[0]
    z = _dot(h_ref[...], w_ref[...])

    def head_rows(dst_ref, val):
        for hh in range(A_HEADS):
            dst_ref[pl.ds(hh, tm, stride=A_HEADS), :] = val[:, hh * 128:(hh + 1) * 128]

    @pl.when(j > 2)
    def _():
        o_ref[...] = z

    @pl.when(j == 2)
    def _():
        o_ref[...] = z
        head_rows(vout_ref, z)

    @pl.when(j < 2)
    def _():
        z2 = z * z
        hi = z2.astype(BF16)
        lo = (z2 - hi.astype(F32)).astype(BF16)
        ssq = _dot(hi, ones_ref[...]) + _dot(lo, ones_ref[...])
        zn = z * lax.rsqrt(ssq * (1.0 / A_QK_DIM) + EPS) * gain_ref[0]
        o_ref[...] = zn

        @pl.when(j == 1)
        def _():
            head_rows(kout_ref, zn)


def _zr_proj(h, wr, gains, ones_bd):
    m = h.shape[0]
    tm, tn = min(m, 1024), 512
    kv_spec = pl.BlockSpec((tm * A_HEADS, 128), lambda i, j: (i, 0))
    kv_shape = jax.ShapeDtypeStruct((m * A_HEADS, 128), F32)
    return pl.pallas_call(
        _zr_kernel,
        grid=(m // tm, ZR_WIDTH // tn),
        in_specs=[pl.BlockSpec((tm, D_MODEL), lambda i, j: (i, 0)),
                  pl.BlockSpec((D_MODEL, tn), lambda i, j: (0, j)),
                  pl.BlockSpec((1, 1, tn), lambda i, j: (jnp.minimum(j, 1), 0, 0)),
                  pl.BlockSpec((tn, tn), lambda i, j: (0, 0))],
        out_specs=[pl.BlockSpec((tm, tn), lambda i, j: (i, j)), kv_spec, kv_spec],
        out_shape=[jax.ShapeDtypeStruct((m, ZR_WIDTH), F32), kv_shape, kv_shape],
        compiler_params=_cparams(2),
        name="zr_proj",
    )(h, wr, gains, ones_bd)


def _bucket(n):
    max_exact = REL_BUCKETS // 2
    large = max_exact + (jnp.log(jnp.maximum(n, 1).astype(F32) / max_exact)
                         / math.log(REL_MAX_DIST / max_exact)
                         * (REL_BUCKETS - max_exact)).astype(jnp.int32)
    return jnp.where(n < max_exact, n, jnp.minimum(large, REL_BUCKETS - 1))


def _bias_kernel(tab_ref, bp_ref, bs_ref, *, t_dec):
    h = pl.program_id(0)

    def lookup(n):
        bucket = _bucket(n)
        val = jnp.full(n.shape, tab_ref[h, REL_BUCKETS - 1], F32)
        for b in range(REL_BUCKETS - 1):
            val = jnp.where(bucket == b, tab_ref[h, b], val)
        return val

    tq = bp_ref.shape[2]
    key = lax.broadcasted_iota(jnp.int32, (tq, tq), 0)
    qry = lax.broadcasted_iota(jnp.int32, (tq, tq), 1)
    bp_ref[0, 0] = jnp.where(key <= qry, lookup(jnp.maximum(qry - key, 0)), NEG_INF)
    bp_ref[0, 1] = lookup(qry - key + tq)

    rows, cols = bs_ref.shape[1], bs_ref.shape[2]
    r = lax.broadcasted_iota(jnp.int32, (rows, cols), 0)
    c = lax.broadcasted_iota(jnp.int32, (rows, cols), 1)
    t = r % t_dec
    tok = c // A_HEADS
    valid = (c % A_HEADS) == h
    far = jnp.full((rows, cols), tab_ref[h, REL_BUCKETS - 1], F32)
    bs_ref[0] = jnp.where(valid, far, NEG_INF)
    bs_ref[1] = jnp.where(valid, lookup(PAGE_SIZE + t - tok), NEG_INF)
    new_ok = valid & (tok <= t) & (tok < t_dec)
    bs_ref[2] = jnp.where(new_ok, lookup(jnp.maximum(t - tok, 0)), NEG_INF)


def _bias_tiles(table, t_dec):
    rows = 2 * t_dec
    return pl.pallas_call(
        functools.partial(_bias_kernel, t_dec=t_dec),
        grid=(A_HEADS,),
        in_specs=[pl.BlockSpec(memory_space=pltpu.SMEM)],
        out_specs=[pl.BlockSpec((1, 2, ATT_TQ, ATT_TQ), lambda h: (h, 0, 0, 0)),
                   pl.BlockSpec((3, rows, PAGE_SIZE * A_HEADS), lambda h: (0, h, 0))],
        out_shape=[jax.ShapeDtypeStruct((A_HEADS, 2, ATT_TQ, ATT_TQ), F32),
                   jax.ShapeDtypeStruct((3, A_HEADS * rows, PAGE_SIZE * A_HEADS), F32)],
        compiler_params=_cparams(1),
        name="bias_tiles",
    )(table.T)


def _lam(lamv_ref, lam_init):
    s1 = jnp.sum(lamv_ref[0:1, :] * lamv_ref[1:2, :], axis=1, keepdims=True)
    s2 = jnp.sum(lamv_ref[2:3, :] * lamv_ref[3:4, :], axis=1, keepdims=True)
    return jnp.exp(s1) - jnp.exp(s2) + lam_init


def _subln(o, g_ref, lam_init):
    y = o * lax.rsqrt(jnp.mean(o * o, axis=-1, keepdims=True) + EPS)
    return y * g_ref[...] * (1.0 - lam_init)


def _attn_prompt_kernel(tab_ref, q_ref, k_ref, v_ref, bias_ref, lamv_ref, subg_ref, o_ref,
                        kb, vt, m_s, l_s, acc_s, *, lam_init):
    h = pl.program_id(1)
    qi = pl.program_id(2)
    tq = q_ref.shape[1]

    @pl.when(qi == 0)
    def _():
        kb[...] = k_ref[0].astype(BF16)
        for jj in range(vt.shape[0]):
            vt[jj] = jnp.transpose(v_ref[0, jj * tq:(jj + 1) * tq, :]).astype(BF16)

    qt = jnp.transpose(q_ref[0])
    sub = lax.broadcasted_iota(jnp.int32, qt.shape, 0)
    qm = (jnp.where(sub < A_QK_DIM, qt, 0.0).astype(BF16),
          jnp.where(sub >= A_QK_DIM, qt, 0.0).astype(BF16))
    m_s[...] = jnp.full(m_s.shape, NEG_INF, F32)
    l_s[...] = jnp.zeros(l_s.shape, F32)
    acc_s[...] = jnp.zeros(acc_s.shape, F32)

    def step(j, bias):
        start = pl.multiple_of(j * tq, tq)
        kj = kb[pl.ds(start, tq), :]
        vj = vt[j]
        for m in range(2):
            m_prev = m_s[m]
            s = _dot(kj, qm[m]) + bias
            m_new = jnp.maximum(m_prev, jnp.max(s, axis=0, keepdims=True))
            p = jnp.exp(s - m_new)
            alpha = jnp.exp(m_prev - m_new)
            l_s[m] = alpha * l_s[m] + jnp.sum(p, axis=0, keepdims=True)
            acc_s[m] = alpha * acc_s[m] + _dot(vj, p.astype(BF16))
            m_s[m] = m_new

    far = tab_ref[h, REL_BUCKETS - 1]

    def far_body(j, carry):
        step(j, far)
        return carry

    lax.fori_loop(0, jnp.maximum(qi - 1, 0), far_body, 0)

    @pl.when(qi >= 1)
    def _():
        step(qi - 1, bias_ref[0, 1])

    step(qi, bias_ref[0, 0])

    lam = _lam(lamv_ref, lam_init)
    o = acc_s[0] * (1.0 / l_s[0]) - lam * (acc_s[1] * (1.0 / l_s[1]))
    y = o * lax.rsqrt(jnp.mean(o * o, axis=0, keepdims=True) + EPS)
    y = y * subg_ref[...] * (1.0 - lam_init)
    o_ref[0] = jnp.transpose(y).astype(o_ref.dtype)


def _attn_prompt(zr3, bias_p, tab_t, lamv, subg, lam_init):
    b, s, _ = zr3.shape
    tq = ATT_TQ
    kern = functools.partial(_attn_prompt_kernel, lam_init=lam_init)
    return pl.pallas_call(
        kern,
        grid=(b, A_HEADS, s // tq),
        in_specs=[pl.BlockSpec(memory_space=pltpu.SMEM),
                  pl.BlockSpec((1, tq, 128), lambda bi, h, qi: (bi, qi, COL_AQ // 128 + h)),
                  pl.BlockSpec((1, s, 128), lambda bi, h, qi: (bi, 0, COL_AK // 128 + h)),
                  pl.BlockSpec((1, s, 128), lambda bi, h, qi: (bi, 0, COL_AV // 128 + h)),
                  pl.BlockSpec((1, 2, tq, tq), lambda bi, h, qi: (h, 0, 0, 0)),
                  pl.BlockSpec((4, A_QK_DIM), lambda bi, h, qi: (0, 0)),
                  pl.BlockSpec((A_V_DIM, 1), lambda bi, h, qi: (0, 0))],
        out_specs=pl.BlockSpec((1, tq, 128), lambda bi, h, qi: (bi, qi, h)),
        out_shape=jax.ShapeDtypeStruct((b, s, A_WIDTH), BF16),
        scratch_shapes=[pltpu.VMEM((s, 128), BF16), pltpu.VMEM((s // tq, 128, tq), BF16),
                        pltpu.VMEM((2, 1, tq), F32), pltpu.VMEM((2, 1, tq), F32),
                        pltpu.VMEM((2, 128, tq), F32)],
        compiler_params=_cparams(3),
        name="attn_prompt",
    )(tab_t, zr3, zr3, zr3, bias_p, lamv, subg.reshape(A_V_DIM, 1))


def _attn_sample_kernel(pt_ref, q_ref, kn_ref, vn_ref, bias_ref, lamv_ref, subg_ref, *rest,
                        lam_init, n_pg, t_dec):
    k_refs = rest[:n_pg]
    v_refs = rest[n_pg:2 * n_pg]
    o_ref = rest[2 * n_pg]
    q_s, m_s, l_s, acc_s = rest[2 * n_pg + 1:]
    j = pl.program_id(1)
    last = pl.num_programs(1) - 1
    rows = 2 * t_dec

    @pl.when(j == 0)
    def _():
        q = q_ref[0]
        lane = lax.broadcasted_iota(jnp.int32, (t_dec, 128), 1)
        for h in range(A_HEADS):
            qh = q[:, h * 128:(h + 1) * 128]
            q_s[h * rows:h * rows + t_dec, :] = jnp.where(lane < A_QK_DIM, qh, 0.0)
            q_s[h * rows + t_dec:(h + 1) * rows, :] = jnp.where(lane >= A_QK_DIM, qh, 0.0)
        m_s[...] = jnp.full(m_s.shape, NEG_INF, F32)
        l_s[...] = jnp.zeros(l_s.shape, F32)
        acc_s[...] = jnp.zeros(acc_s.shape, F32)

    qb = q_s[...].astype(BF16)

    def update(ks, vs, biases):
        ss = [_dot_nt(qb, kb) + bias for kb, bias in zip(ks, biases)]
        smax = functools.reduce(jnp.maximum, ss)
        m_prev = m_s[...]
        m_new = jnp.maximum(m_prev, jnp.max(smax, axis=1, keepdims=True))
        ps = [jnp.exp(s - m_new) for s in ss]
        alpha = jnp.exp(m_prev - m_new)
        l_s[...] = alpha * l_s[...] + jnp.sum(functools.reduce(jnp.add, ps), axis=1, keepdims=True)
        pv = functools.reduce(jnp.add, [_dot(p.astype(BF16), vb) for p, vb in zip(ps, vs)])
        acc_s[...] = alpha * acc_s[...] + pv
        m_s[...] = m_new

    biases = [bias_ref[0]] * (n_pg - 1) + [jnp.where(j == last, bias_ref[1], bias_ref[0])]
    update([r[0, 0].astype(BF16) for r in k_refs], [r[0, 0].astype(BF16) for r in v_refs], biases)

    @pl.when(j == last)
    def _():
        nk = kn_ref.shape[1]
        update([kn_ref[0].astype(BF16)], [vn_ref[0].astype(BF16)], [bias_ref[2][:, :nk]])
        lam = _lam(lamv_ref, lam_init)
        on = acc_s[...] * (1.0 / l_s[...])
        for h in range(A_HEADS):
            o = on[h * rows:h * rows + t_dec] - lam * on[h * rows + t_dec:(h + 1) * rows]
            o_ref[0, :, h * 128:(h + 1) * 128] = _subln(o, subg_ref, lam_init).astype(o_ref.dtype)


def _attn_sample(zrs3, kn2d, vn2d, cache_k2d, cache_v2d, page_table, bias_s, lamv, subg,
                 layer, lam_init):
    bd, t_dec, _ = zrs3.shape
    n_pages = page_table.shape[1]
    n_pg = SAMPLE_PAGES
    rows = 2 * t_dec * A_HEADS
    pcols = PAGE_SIZE * A_HEADS

    def page_spec(p):
        return pl.BlockSpec((1, 1, pcols, 128),
                            lambda b, j, pt: (layer, pt[b, j * n_pg + p], 0, 0))

    kern = functools.partial(_attn_sample_kernel, lam_init=lam_init, n_pg=n_pg, t_dec=t_dec)
    grid_spec = pltpu.PrefetchScalarGridSpec(
        num_scalar_prefetch=1,
        grid=(bd, n_pages // n_pg),
        in_specs=[pl.BlockSpec((1, t_dec, A_WIDTH), lambda b, j, pt: (b, 0, 0)),
                  pl.BlockSpec((1,) + kn2d.shape[1:], lambda b, j, pt: (b, 0, 0)),
                  pl.BlockSpec((1,) + vn2d.shape[1:], lambda b, j, pt: (b, 0, 0)),
                  pl.BlockSpec((3, rows, pcols), lambda b, j, pt: (0, 0, 0)),
                  pl.BlockSpec((4, A_QK_DIM), lambda b, j, pt: (0, 0)),
                  pl.BlockSpec((1, A_V_DIM), lambda b, j, pt: (0, 0))]
                 + [page_spec(p) for p in range(n_pg)] * 2,
        out_specs=pl.BlockSpec((1, t_dec, A_WIDTH), lambda b, j, pt: (b, 0, 0)),
        scratch_shapes=[pltpu.VMEM((rows, 128), F32), pltpu.VMEM((rows, 1), F32),
                        pltpu.VMEM((rows, 1), F32), pltpu.VMEM((rows, 128), F32)],
    )
    return pl.pallas_call(
        kern,
        grid_spec=grid_spec,
        out_shape=jax.ShapeDtypeStruct((bd, t_dec, A_WIDTH), BF16),
        compiler_params=_cparams(2),
        name="attn_sample",
    )(page_table, zrs3, kn2d, vn2d, bias_s, lamv, subg,
      *([cache_k2d] * n_pg), *([cache_v2d] * n_pg))


def _gelu(x):
    return 0.5 * x * (1.0 + jnp.tanh(math.sqrt(2.0 / math.pi) * (x + 0.044715 * (x * x * x))))


def _chunk_mlp_kernel(u_ref, v_ref, w_ref, bias_ref, g_ref, b_ref, y_ref, vb_ref):
    n_chunks = u_ref.shape[0] // B_CHUNK
    lane_grp = lax.broadcasted_iota(jnp.int32, (B_CHUNK, B_WIDTH), 1) // (B_WIDTH // B_GROUPS)
    for c in range(n_chunks):
        sl = slice(c * B_CHUNK, (c + 1) * B_CHUNK)
        gv = _gelu(v_ref[sl, :])
        xc = gv - jnp.mean(gv, axis=-1, keepdims=True)
        vb = xc * lax.rsqrt(jnp.mean(xc * xc, axis=-1, keepdims=True) + EPS) * g_ref[...] + b_ref[...]
        vb_ref[sl, :] = vb
        vbb = vb.astype(BF16)
        mixed = bias_ref[...]
        for g in range(B_GROUPS):
            mixed = mixed + jnp.where(lane_grp == g, _dot(w_ref[g], vbb), 0.0)
        y_ref[sl, :] = (_gelu(u_ref[sl, :]) * mixed).astype(y_ref.dtype)


def _chunk_mlp(zr, w_eff, bias_eff, ln_g, ln_b):
    m = zr.shape[0]
    tm = min(m, 512)
    return pl.pallas_call(
        _chunk_mlp_kernel,
        grid=(m // tm,),
        in_specs=[pl.BlockSpec((tm, B_WIDTH), lambda i: (i, COL_BU // B_WIDTH)),
                  pl.BlockSpec((tm, B_WIDTH), lambda i: (i, COL_BV // B_WIDTH)),
                  pl.BlockSpec((B_GROUPS, B_CHUNK, B_CHUNK), lambda i: (0, 0, 0)),
                  pl.BlockSpec((B_CHUNK, B_WIDTH), lambda i: (0, 0)),
                  pl.BlockSpec((1, B_WIDTH), lambda i: (0, 0)),
                  pl.BlockSpec((1, B_WIDTH), lambda i: (0, 0))],
        out_specs=[pl.BlockSpec((tm, B_WIDTH), lambda i: (i, 0)),
                   pl.BlockSpec((tm, B_WIDTH), lambda i: (i, 0))],
        out_shape=[jax.ShapeDtypeStruct((m, B_WIDTH), BF16),
                   jax.ShapeDtypeStruct((m, B_WIDTH), F32)],
        compiler_params=_cparams(1),
        name="chunk_mlp",
    )(zr, zr, w_eff, bias_eff, ln_g, ln_b)


def _pool_kernel(x_ref, w_ref, scale_ref, y_ref, *, prefix, pos0):
    x = x_ref[0]
    rows = x.shape[0]
    row = lax.broadcasted_iota(jnp.int32, x.shape, 0)
    grp = lax.broadcasted_iota(jnp.int32, x.shape, 1) // C_GROUP_DIM

    def shifted(a, k):
        return jnp.where(row >= k, pltpu.roll(a, k, 0), 0.0)

    sums = []
    acc = x
    for k in (1, 2, 4, 8):
        acc = acc + shifted(acc, k)
        sums.append(acc)
    total = sums[3]
    win = jnp.full(x.shape, POOL_WINDOWS[3], jnp.int32)
    for g in range(3):
        total = jnp.where(grp == g, sums[g], total)
        win = jnp.where(grp == g, POOL_WINDOWS[g], win)
    pos = pos0 + row - prefix
    cnt = jnp.clip(pos + 1, 1, win).astype(F32)
    d = total / cnt - x
    y = _dot(d.astype(BF16), w_ref[...]) * scale_ref[...]
    y_ref[0] = y.astype(y_ref.dtype)
    del rows


def _pool_mix(xx, col_block, w_bd, scale, prefix, pos0):
    b, rows, _ = xx.shape
    kern = functools.partial(_pool_kernel, prefix=prefix, pos0=pos0)
    return pl.pallas_call(
        kern,
        grid=(b,),
        in_specs=[pl.BlockSpec((1, rows, C_WIDTH), lambda i: (i, 0, col_block)),
                  pl.BlockSpec((C_WIDTH, C_WIDTH), lambda i: (0, 0)),
                  pl.BlockSpec((1, C_WIDTH), lambda i: (0, 0))],
        out_specs=pl.BlockSpec((1, rows, C_WIDTH), lambda i: (i, 0, 0)),
        out_shape=jax.ShapeDtypeStruct((b, rows, C_WIDTH), BF16),
        compiler_params=_cparams(1),
        name="pool_mix",
    )(xx, w_bd, scale)


def _log_sigmoid(x):
    return jnp.minimum(x, 0.0) - jnp.log1p(jnp.exp(-jnp.abs(x)))


def _split3(x):
    def top8(a):
        bits = lax.bitcast_convert_type(a, jnp.int32) & jnp.int32(-65536)
        return lax.bitcast_convert_type(bits, F32)

    p1 = top8(x)
    r1 = x - p1
    p2 = top8(r1)
    return p1, p2, r1 - p2


def _mlstm_kernel(q_ref, k_ref, v_ref, o_ref, gi_ref, gf_ref, gb_ref, ng_ref, sel_ref, bd_ref,
                  c0_ref, n0_ref, m0_ref, y_ref, c_out, n_out, m_out, c_s, n_s, m_s, *, t_valid):
    ci = pl.program_id(1)
    nb, chunk = q_ref.shape[0], q_ref.shape[1]
    hd = D_HEAD_DIM
    eye_h = (lax.broadcasted_iota(jnp.int32, (hd, hd), 0)
             == lax.broadcasted_iota(jnp.int32, (hd, hd), 1))

    @pl.when(ci == 0)
    def _():
        c_s[...] = jnp.zeros(c_s.shape, F32)
        n_s[...] = jnp.zeros(n_s.shape, F32)
        m_s[...] = m0_ref[...]
        for bi in range(nb):
            for h in range(D_HEADS):
                hs = slice(h * hd, (h + 1) * hd)
                c_s[bi, hs, hs] = c0_ref[bi, h]
                n_col = jnp.sum(jnp.where(eye_h, n0_ref[bi, h], 0.0), axis=1, keepdims=True)
                n_s[bi, hs, hs] = jnp.broadcast_to(n_col, (hd, hd))

    row = lax.broadcasted_iota(jnp.int32, (chunk, 128), 0)
    lane = lax.broadcasted_iota(jnp.int32, (chunk, 128), 1)
    head_lane = lane < D_HEADS
    grp = lax.broadcasted_iota(jnp.int32, (chunk, D_WIDTH), 1) // hd
    rr = lax.broadcasted_iota(jnp.int32, (chunk, chunk), 0)
    cc = lax.broadcasted_iota(jnp.int32, (chunk, chunk), 1)
    allowed = (cc <= rr) & (cc < t_valid)
    ones_t = jnp.ones((chunk, 128), F32)
    ones_w = jnp.ones((chunk, D_WIDTH), BF16)
    same_head = (lax.broadcasted_iota(jnp.int32, (D_WIDTH, D_WIDTH), 0) // hd
                 == lax.broadcasted_iota(jnp.int32, (D_WIDTH, D_WIDTH), 1) // hd)
    pick = [lane == h for h in range(D_HEADS)]
    in_head = [grp == h for h in range(D_HEADS)]
    pick_one = [jnp.where(pk, 1.0, 0.0) for pk in pick]
    head_one = [jnp.where(ih, 1.0, 0.0).astype(BF16) for ih in in_head]

    for bi in range(nb):
        gi = gi_ref[bi] + gb_ref[0:1, :]
        lf = _log_sigmoid(gf_ref[bi] + gb_ref[1:2, :])
        if t_valid < chunk:
            gi = jnp.where(row < t_valid, gi, NEG_INF)
            lf = jnp.where(row < t_valid, lf, 0.0)
        b = lf
        k = 1
        while k < chunk:
            b = b + jnp.where(row >= k, pltpu.roll(b, k, 0), 0.0)
            k *= 2
        u = gi - b
        cm = u
        k = 1
        while k < chunk:
            cm = jnp.maximum(cm, jnp.where(row >= k, pltpu.roll(cm, k, 0), NEG_INF))
            k *= 2
        m_prev = m_s[bi]
        big_m = jnp.maximum(m_prev, cm)
        m_last = big_m[chunk - 1:chunk, :]
        winter = jnp.exp(m_prev - big_m)
        emt = jnp.exp(jnp.minimum(-(b + big_m), EMT_CAP))
        ws = jnp.exp(u - m_last)
        m_s[bi] = b[chunk - 1:chunk, :] + m_last

        def per_head_lanes(z):
            pieces = _split3(jnp.where(head_lane, z, 0.0))
            return _dot(jnp.concatenate(pieces, axis=1).astype(BF16), sel_ref[...])

        winter_r, emt_r, ws_r = per_head_lanes(winter), per_head_lanes(emt), per_head_lanes(ws)
        decay_r = winter_r[chunk - 1:chunk, :]

        q = q_ref[bi]
        qb = q.astype(BF16)
        kf = k_ref[bi] * (hd ** -0.5)
        kb = kf.astype(BF16)
        vf = v_ref[bi]
        vb = vf.astype(BF16)
        u_fin = jnp.where(head_lane & (row < t_valid), u, 0.0)
        y_side = jnp.concatenate(_split3(u_fin) + (ones_t, ones_t, ones_t), axis=1).astype(BF16)
        m_neg = _split3(jnp.where(head_lane, -big_m, 0.0))
        acc = jnp.zeros((chunk, 2 * D_WIDTH), F32)
        for h in range(D_HEADS):
            x_side = jnp.concatenate(
                [pick_one[h]] * 3 + [jnp.where(pick[h], piece, 0.0) for piece in m_neg],
                axis=1).astype(BF16)
            expo = jnp.where(allowed, _dot_nt(x_side, y_side), NEG_INF)
            qk = _dot_nt(jnp.where(in_head[h], q, 0.0).astype(BF16), kb) * jnp.exp(expo)
            rhs = jnp.concatenate([jnp.where(in_head[h], vf, 0.0).astype(BF16), head_one[h]], axis=1)
            acc = acc + _dot(qk.astype(BF16), rhs)
        state = jnp.concatenate([c_s[bi].astype(BF16), n_s[bi].astype(BF16)], axis=1)
        inter = _dot(qb, state)
        num = winter_r * inter[:, :D_WIDTH] + acc[:, :D_WIDTH]
        den = winter_r * inter[:, D_WIDTH:] + acc[:, D_WIDTH:]
        hh = num / jnp.maximum(jnp.abs(den), emt_r)
        h2 = hh * hh
        hi = h2.astype(BF16)
        lo = (h2 - hi.astype(F32)).astype(BF16)
        ssq = _dot(hi, bd_ref[...]) + _dot(lo, bd_ref[...])
        y = hh * lax.rsqrt(ssq * (1.0 / hd) + EPS) * ng_ref[...]
        y_ref[bi] = (y * jax.nn.sigmoid(o_ref[bi])).astype(y_ref.dtype)
        kw = (ws_r * kf).astype(BF16)
        upd = _dot_tn(kw, jnp.concatenate([vb, ones_w], axis=1))
        c_s[bi] = decay_r * c_s[bi] + jnp.where(same_head, upd[:, :D_WIDTH], 0.0)
        n_s[bi] = decay_r * n_s[bi] + jnp.where(same_head, upd[:, D_WIDTH:], 0.0)

    @pl.when(ci == pl.num_programs(1) - 1)
    def _():
        m_out[...] = m_s[...]
        for bi in range(nb):
            for h in range(D_HEADS):
                hs = slice(h * hd, (h + 1) * hd)
                c_out[bi, h] = c_s[bi, hs, hs]
                n_out[bi, h] = jnp.sum(jnp.where(eye_h, n_s[bi, hs, hs], 0.0), axis=0, keepdims=True)


def _mlstm(src, col0, gate_block, chunk, t_valid, gate_bias, norm_g, c0, n0, m0):
    b, t, _ = src.shape
    nb = math.gcd(b, MLSTM_BATCH)
    hd = D_HEAD_DIM
    kern = functools.partial(_mlstm_kernel, t_valid=t_valid)
    head_of_lane = jnp.arange(D_WIDTH) // hd
    sel = (jnp.arange(128)[:, None] == head_of_lane[None, :]).astype(BF16)
    sel3 = jnp.concatenate([sel, sel, sel], axis=0)
    same_head = (head_of_lane[:, None] == head_of_lane[None, :]).astype(BF16)
    m0p = jnp.pad(m0, ((0, 0), (0, 0), (0, 128 - D_HEADS)))

    def col(cb, width=D_WIDTH):
        return pl.BlockSpec((nb, chunk, width), lambda bi, ci: (bi, ci, cb))

    def const(shape):
        return pl.BlockSpec(shape, lambda bi, ci: (0,) * len(shape))

    def per_seq(shape):
        return pl.BlockSpec((nb,) + shape, lambda bi, ci: (bi,) + (0,) * len(shape))

    y, c_new, n_new, m_new = pl.pallas_call(
        kern,
        grid=(b // nb, t // chunk),
        in_specs=[col(col0), col(col0 + 1), col(col0 + 2), col(col0 + 3),
                  col(gate_block, 128), col(gate_block + 1, 128),
                  const((2, 128)), const((1, D_WIDTH)), const((3 * 128, D_WIDTH)),
                  const((D_WIDTH, D_WIDTH)),
                  per_seq((D_HEADS, hd, hd)), per_seq((D_HEADS, 1, hd)), per_seq((1, 128))],
        out_specs=[col(0), per_seq((D_HEADS, hd, hd)), per_seq((D_HEADS, 1, hd)),
                   per_seq((1, 128))],
        out_shape=[jax.ShapeDtypeStruct((b, t, D_WIDTH), BF16),
                   jax.ShapeDtypeStruct((b, D_HEADS, hd, hd), F32),
                   jax.ShapeDtypeStruct((b, D_HEADS, 1, hd), F32),
                   jax.ShapeDtypeStruct((b, 1, 128), F32)],
        scratch_shapes=[pltpu.VMEM((nb, D_WIDTH, D_WIDTH), F32),
                        pltpu.VMEM((nb, D_WIDTH, D_WIDTH), F32),
                        pltpu.VMEM((nb, 1, 128), F32)],
        compiler_params=_cparams(2),
        name="mlstm",
    )(src, src, src, src, src, src, gate_bias, jnp.tile(norm_g, (1, D_HEADS)), sel3, same_head,
      c0, n0, m0p)
    return y, c_new, n_new, m_new[:, :, :D_HEADS]


def _merge_kernel(x_ref, g_ref, ya_ref, yb_ref, yc_ref, yd_ref, wa_ref, wb_ref, wc_ref, wd_ref,
                  wo_ref, o_ref):
    def gate(i):
        return g_ref[:, i * D_MODEL:(i + 1) * D_MODEL].astype(F32)

    merged = gate(0) * _dot(ya_ref[...], wa_ref[...])
    merged = merged + gate(1) * _dot(yb_ref[...], wb_ref[...])
    merged = merged + gate(2) * _dot(yc_ref[...], wc_ref[...])
    merged = merged + gate(3) * _dot(yd_ref[...], wd_ref[...])
    o_ref[...] = x_ref[...] + _dot(merged.astype(BF16), wo_ref[...])


def _merge(x, gates, ya, yb, yc, yd, wa, wb, wc, wd, wo):
    m = x.shape[0]
    tm = min(m, 512)

    def rows(width):
        return pl.BlockSpec((tm, width), lambda i: (i, 0))

    def full(arr):
        return pl.BlockSpec(arr.shape, lambda i: (0, 0))

    return pl.pallas_call(
        _merge_kernel,
        grid=(m // tm,),
        in_specs=[rows(D_MODEL), rows(GZ_WIDTH), rows(A_WIDTH), rows(B_WIDTH), rows(C_WIDTH),
                  rows(D_WIDTH), full(wa), full(wb), full(wc), full(wd), full(wo)],
        out_specs=rows(D_MODEL),
        out_shape=jax.ShapeDtypeStruct((m, D_MODEL), F32),
        compiler_params=_cparams(1),
        name="merge",
    )(x, gates, ya, yb, yc, yd, wa, wb, wc, wd, wo)


def _ffn_kernel(x_ref, g_ref, w1_ref, w2_ref, o_ref, h_s, acc_s):
    j = pl.program_id(1)

    @pl.when(j == 0)
    def _():
        x = x_ref[...]
        y = x * lax.rsqrt(jnp.mean(x * x, axis=-1, keepdims=True) + EPS)
        h_s[...] = (y * g_ref[...]).astype(BF16)
        acc_s[...] = jnp.zeros(acc_s.shape, F32)

    a = jnp.maximum(_dot(h_s[...], w1_ref[...]), 0.0)
    acc_s[...] += _dot((a * a).astype(BF16), w2_ref[...])

    @pl.when(j == pl.num_programs(1) - 1)
    def _():
        o_ref[...] = x_ref[...] + acc_s[...]


def _ffn(x, g, w1, w2):
    m = x.shape[0]
    tm, tf = min(m, 1024), 512
    return pl.pallas_call(
        _ffn_kernel,
        grid=(m // tm, D_FF // tf),
        in_specs=[pl.BlockSpec((tm, D_MODEL), lambda i, j: (i, 0)),
                  pl.BlockSpec((1, D_MODEL), lambda i, j: (0, 0)),
                  pl.BlockSpec((D_MODEL, tf), lambda i, j: (0, j)),
                  pl.BlockSpec((tf, D_MODEL), lambda i, j: (j, 0))],
        out_specs=pl.BlockSpec((tm, D_MODEL), lambda i, j: (i, 0)),
        out_shape=jax.ShapeDtypeStruct((m, D_MODEL), F32),
        scratch_shapes=[pltpu.VMEM((tm, D_MODEL), BF16), pltpu.VMEM((tm, D_MODEL), F32)],
        compiler_params=_cparams(2),
        name="ffn",
    )(x, g.reshape(1, D_MODEL), w1, w2)


def _layer_weights(p, l):
    w_in = p["w_in"][l]
    gate_pad = jnp.zeros((D_MODEL, 128 - D_HEADS), F32)
    wr = jnp.concatenate(
        [w_in[:, :3072], w_in[:, 3080:3336], w_in[:, 3072:3076], gate_pad,
         w_in[:, 3076:3080], gate_pad], axis=1).astype(BF16)
    wg = w_in[:, 3336:].astype(BF16)
    reps = A_WIDTH // A_QK_DIM
    gains = jnp.stack([jnp.tile(p["q_norm_g"][l], reps) * (A_QK_DIM ** -0.5),
                       jnp.tile(p["k_norm_g"][l], reps)]).reshape(2, 1, A_WIDTH)
    grp = jnp.arange(A_WIDTH) // A_QK_DIM
    ones_bd = (grp[:, None] == grp[None, :]).astype(BF16)
    lamv = jnp.stack([p["lam_q1"][l], p["lam_k1"][l], p["lam_q2"][l], p["lam_k2"][l]])
    gate_bias = jnp.pad(jnp.stack([p["d_i_bias"][l], p["d_f_bias"][l]]),
                        ((0, 0), (0, 128 - D_HEADS)))
    c_bd = jnp.zeros((C_WIDTH, C_WIDTH), F32)
    for g in range(4):
        sl = slice(g * C_GROUP_DIM, (g + 1) * C_GROUP_DIM)
        c_bd = c_bd.at[sl, sl].set(p["c_lin"][l][g])
    return dict(
        norm1_g=p["norm1_g"][l], norm2_g=p["norm2_g"][l], wr=wr, wg=wg, gains=gains,
        ones_bd=ones_bd, lamv=lamv, subg=p["subln_g"][l].reshape(1, A_V_DIM),
        b_ln_g=p["b_ln_g"][l].reshape(1, B_WIDTH), b_ln_b=p["b_ln_b"][l].reshape(1, B_WIDTH),
        b_ws=p["b_ws"][l], b_bias=p["b_bias"][l],
        c_bd=c_bd.astype(BF16), c_scale=p["c_scale"][l].reshape(1, C_WIDTH),
        gate_bias=gate_bias, d_norm_g=p["d_norm_g"][l].reshape(1, D_HEAD_DIM),
        w_pa=p["w_pa"][l].astype(BF16), w_pb=p["w_pb"][l].astype(BF16),
        w_pc=p["w_pc"][l].astype(BF16), w_pd=p["w_pd"][l].astype(BF16),
        w_out=p["w_out"][l].astype(BF16), w_ff1=p["w_ff1"][l].astype(BF16),
        w_ff2=p["w_ff2"][l].astype(BF16),
        lam_init=0.8 - 0.6 * math.exp(-0.3 * l),
    )


def _chunk_weights(w, t):
    length = min(t, B_CHUNK)
    ws = jnp.tril(w["b_ws"][:, :length, :length])
    bias = jnp.transpose(w["b_bias"][:, :length])
    reps = B_CHUNK // length
    if reps > 1:
        eye = jnp.eye(reps, dtype=F32)
        ws = jax.vmap(lambda a: jnp.kron(eye, a))(ws)
        bias = jnp.tile(bias, (reps, 1))
    return ws.astype(BF16), jnp.repeat(bias, B_WIDTH // B_GROUPS, axis=1)


def _finish_layer(x2, w, gates, zr, ya, yc, yd, t):
    w_eff, bias_eff = _chunk_weights(w, t)
    yb, vb = _chunk_mlp(zr, w_eff, bias_eff, w["b_ln_g"], w["b_ln_b"])
    x2 = _merge(x2, gates, ya, yb, yc, yd, w["w_pa"], w["w_pb"], w["w_pc"], w["w_pd"], w["w_out"])
    return _ffn(x2, w["norm2_g"], w["w_ff1"], w["w_ff2"]), vb


def _prompt_layer(x2, w, b, s, bias_p, tab_t):
    m = b * s
    h = _rms_cast(x2, w["norm1_g"])
    gates = _gate_proj(h, w["wg"])
    zr, k_rows, v_rows = _zr_proj(h, w["wr"], w["gains"], w["ones_bd"])
    zr3 = zr.reshape(b, s, ZR_WIDTH)
    ya = _attn_prompt(zr3, bias_p, tab_t, w["lamv"], w["subg"], w["lam_init"]).reshape(m, A_WIDTH)
    yc = _pool_mix(zr3, COL_CX // C_WIDTH, w["c_bd"], w["c_scale"], 0, 0).reshape(m, C_WIDTH)
    hd = D_HEAD_DIM
    yd, c_new, n_new, m_new = _mlstm(
        zr3, COL_DQ // D_WIDTH, COL_DGI // 128, min(s, 128), min(s, 128), w["gate_bias"],
        w["d_norm_g"], jnp.zeros((b, D_HEADS, hd, hd), F32), jnp.zeros((b, D_HEADS, 1, hd), F32),
        jnp.zeros((b, 1, D_HEADS), F32))
    x2, _ = _finish_layer(x2, w, gates, zr, ya, yc, yd.reshape(m, D_WIDTH), s)
    outs = (k_rows.reshape(b, s, A_HEADS, 2 * A_QK_DIM), v_rows.reshape(b, s, A_HEADS, A_V_DIM),
            zr3[:, s - POOL_BUF:, COL_CX:COL_CX + C_WIDTH],
            c_new, n_new.reshape(b, D_HEADS, hd), m_new.reshape(b, D_HEADS))
    return x2, outs


def _sample_layer(x2, w, l, bd, t, cache_k2d, cache_v2d, page_table, bias_s, pool0, c0, n0, m0):
    m = bd * t
    past = page_table.shape[1] * PAGE_SIZE
    h = _rms_cast(x2, w["norm1_g"])
    gates = _gate_proj(h, w["wg"])
    zr, kn, vn = _zr_proj(h, w["wr"], w["gains"], w["ones_bd"])
    zr3 = zr.reshape(bd, t, ZR_WIDTH)
    ya = _attn_sample(zr3, kn.reshape(bd, t * A_HEADS, 128), vn.reshape(bd, t * A_HEADS, 128),
                      cache_k2d, cache_v2d, page_table, bias_s, w["lamv"], w["subg"], l,
                      w["lam_init"]).reshape(m, A_WIDTH)
    cx = zr3[:, :, COL_CX:COL_CX + C_WIDTH]
    prefix = POOL_BUF + 1
    rows = -(-(prefix + t) // 8) * 8
    xx = jnp.concatenate([jnp.zeros((bd, 1, C_WIDTH), F32), pool0, cx,
                          jnp.zeros((bd, rows - prefix - t, C_WIDTH), F32)], axis=1)
    yc = _pool_mix(xx, 0, w["c_bd"], w["c_scale"], prefix, past)[:, prefix:prefix + t]
    yc = yc.reshape(m, C_WIDTH)
    chunk = 128
    dsrc = jnp.pad(zr3[:, :, COL_DQ:COL_DGF + 128], ((0, 0), (0, chunk - t), (0, 0)))
    hd = D_HEAD_DIM
    yd, c_new, n_new, m_new = _mlstm(
        dsrc, 0, (COL_DGI - COL_DQ) // 128, chunk, t, w["gate_bias"], w["d_norm_g"],
        c0, n0.reshape(bd, D_HEADS, 1, hd), m0.reshape(bd, 1, D_HEADS))
    yd = yd[:, :t].reshape(m, D_WIDTH)
    x2, vb = _finish_layer(x2, w, gates, zr, ya, yc, yd, t)
    outs = (kn.reshape(bd, t, A_HEADS, 2 * A_QK_DIM), vn.reshape(bd, t, A_HEADS, A_V_DIM),
            vb.reshape(bd, t, B_WIDTH), jnp.concatenate([pool0, cx], axis=1)[:, -POOL_BUF:],
            c_new, n_new.reshape(bd, D_HEADS, hd), m_new.reshape(bd, D_HEADS))
    return x2, outs


def kernel(x_prompt, x_sample, cache_k, cache_v, page_table, state_pool, state_C, state_n, state_m, rel_bias, norm1_g, norm2_g, w_in, q_norm_g, k_norm_g, lam_q1, lam_k1, lam_q2, lam_k2, subln_g, b_ln_g, b_ln_b, b_ws, b_bias, c_lin, c_scale, d_i_bias, d_f_bias, d_norm_g, w_pa, w_pb, w_pc, w_pd, w_out, w_ff1, w_ff2):
    p = dict(norm1_g=norm1_g, norm2_g=norm2_g, w_in=w_in, q_norm_g=q_norm_g, k_norm_g=k_norm_g,
             lam_q1=lam_q1, lam_k1=lam_k1, lam_q2=lam_q2, lam_k2=lam_k2, subln_g=subln_g,
             b_ln_g=b_ln_g, b_ln_b=b_ln_b, b_ws=b_ws, b_bias=b_bias, c_lin=c_lin, c_scale=c_scale,
             d_i_bias=d_i_bias, d_f_bias=d_f_bias, d_norm_g=d_norm_g, w_pa=w_pa, w_pb=w_pb,
             w_pc=w_pc, w_pd=w_pd, w_out=w_out, w_ff1=w_ff1, w_ff2=w_ff2)
    depth = w_in.shape[0]
    bp, sp, _ = x_prompt.shape
    bd, td, _ = x_sample.shape
    n_phys = cache_k.shape[1]
    cache_k2d = cache_k.reshape(depth, n_phys, PAGE_SIZE * A_HEADS, 128)
    cache_v2d = cache_v.reshape(depth, n_phys, PAGE_SIZE * A_HEADS, 128)
    bias_p, bias_s = _bias_tiles(rel_bias, td)
    tab_t = rel_bias.T

    xp = x_prompt.reshape(bp * sp, D_MODEL)
    xs = x_sample.reshape(bd * td, D_MODEL)
    prompt_outs, sample_outs = [], []
    for l in range(depth):
        w = _layer_weights(p, l)
        xp, po = _prompt_layer(xp, w, bp, sp, bias_p, tab_t)
        xs, so = _sample_layer(xs, w, l, bd, td, cache_k2d, cache_v2d, page_table, bias_s,
                               state_pool[l], state_C[l], state_n[l], state_m[l])
        prompt_outs.append(po)
        sample_outs.append(so)

    def stack(outs, i):
        return jnp.stack([o[i] for o in outs])

    return (xp.reshape(bp, sp, D_MODEL), xs.reshape(bd, td, D_MODEL),
            stack(prompt_outs, 0), stack(prompt_outs, 1), stack(sample_outs, 0), stack(sample_outs, 1),
            stack(sample_outs, 2), stack(prompt_outs, 2), stack(sample_outs, 3),
            stack(prompt_outs, 3), stack(prompt_outs, 4), stack(prompt_outs, 5),
            stack(sample_outs, 4), stack(sample_outs, 5), stack(sample_outs, 6))
```

```python
import functools
import math

import jax
import jax.numpy as jnp
from jax import lax
from jax.experimental import pallas as pl
from jax.experimental.pallas import tpu as pltpu

F32 = jnp.float32
BF16 = jnp.bfloat16
NEG_INF = float("-inf")

D_MODEL = 1024
A_HEADS = 4
A_QK_DIM = 64
A_V_DIM = 128
A_WIDTH = 512
REL_BUCKETS = 32
REL_MAX_DIST = 128
PAGE_SIZE = 128
B_GROUPS = 4
B_WIDTH = 256
B_CHUNK = 128
C_WIDTH = 256
C_GROUP_DIM = 64
POOL_WINDOWS = (2, 4, 8, 16)
POOL_BUF = 15
D_HEADS = 4
D_WIDTH = 256
D_HEAD_DIM = 64
D_FF = 4096
N_BRANCH = 4
EPS = 1e-6

ZR_WIDTH = 3584
COL_AQ, COL_AK, COL_AV = 0, 512, 1024
COL_BU, COL_BV, COL_CX = 1536, 1792, 2048
COL_DQ, COL_DK, COL_DV, COL_DO = 2304, 2560, 2816, 3072
COL_DGI, COL_DGF = 3328, 3456
GZ_WIDTH = N_BRANCH * D_MODEL
EMT_CAP = 80.0
LOG2E = math.log2(math.e)
ONES_ROWS = 16

IN_PROJ_TM = 512
IN_PROJ_TN = 512
ATT_TQ = 512
SAMPLE_PAGES = 32
MLSTM_BATCH = 4
VMEM_LIMIT = 56 * 1024 * 1024


def _cparams(n_axes):
    return pltpu.CompilerParams(dimension_semantics=("arbitrary",) * n_axes,
                                vmem_limit_bytes=VMEM_LIMIT)


def _dot(a, b):
    return jnp.dot(a, b, preferred_element_type=F32)


def _dot_nt(a, b):
    return lax.dot_general(a, b, (((1,), (1,)), ((), ())), preferred_element_type=F32)


def _dot_tn(a, b):
    return lax.dot_general(a, b, (((0,), (0,)), ((), ())), preferred_element_type=F32)


def _in_proj_kernel(x_ref, g_ref, wg_ref, wr_ref, gain_ref, ones_ref,
                    gates_ref, zr_ref, kout_ref, vout_ref):
    tm = x_ref.shape[0]
    tn = IN_PROJ_TN
    x = x_ref[...]
    h = (x * lax.rsqrt(jnp.mean(x * x, axis=-1, keepdims=True) + EPS) * g_ref[...]).astype(BF16)

    def head_rows(dst_ref, val):
        for hh in range(A_HEADS):
            dst_ref[pl.ds(hh, tm, stride=A_HEADS), :] = val[:, hh * 128:(hh + 1) * 128]

    for c in range(ZR_WIDTH // tn):
        cols = slice(c * tn, (c + 1) * tn)
        z = _dot(h, wr_ref[:, cols])
        if c * tn in (COL_AQ, COL_AK):
            z2 = z * z
            hi = z2.astype(BF16)
            lo = (z2 - hi.astype(F32)).astype(BF16)
            ssq = _dot(hi, ones_ref[...]) + _dot(lo, ones_ref[...])
            z = z * lax.rsqrt(ssq * (1.0 / A_QK_DIM) + EPS) * gain_ref[c]
        zr_ref[:, cols] = z
        if c * tn == COL_AK:
            head_rows(kout_ref, z)
        if c * tn == COL_AV:
            head_rows(vout_ref, z)

    for c in range(GZ_WIDTH // tn):
        cols = slice(c * tn, (c + 1) * tn)
        gates_ref[:, cols] = jax.nn.sigmoid(_dot(h, wg_ref[:, cols])).astype(gates_ref.dtype)


def _in_proj(x, g, wg, wr, gains, ones_bd):
    m = x.shape[0]
    tm = min(m, IN_PROJ_TM)

    def rows(width):
        return pl.BlockSpec((tm, width), lambda i: (i, 0))

    def resident(arr):
        return pl.BlockSpec(arr.shape, lambda i: (0,) * arr.ndim, pipeline_mode=pl.Buffered(1))

    kv_spec = pl.BlockSpec((tm * A_HEADS, 128), lambda i: (i, 0))
    kv_shape = jax.ShapeDtypeStruct((m * A_HEADS, 128), F32)
    g = g.reshape(1, D_MODEL)
    return pl.pallas_call(
        _in_proj_kernel,
        grid=(m // tm,),
        in_specs=[rows(D_MODEL), resident(g), resident(wg), resident(wr), resident(gains),
                  resident(ones_bd)],
        out_specs=[rows(GZ_WIDTH), rows(ZR_WIDTH), kv_spec, kv_spec],
        out_shape=[jax.ShapeDtypeStruct((m, GZ_WIDTH), BF16),
                   jax.ShapeDtypeStruct((m, ZR_WIDTH), F32), kv_shape, kv_shape],
        compiler_params=_cparams(1),
        name="in_proj",
    )(x, g, wg, wr, gains, ones_bd)


def _bucket(n):
    max_exact = REL_BUCKETS // 2
    large = max_exact + (jnp.log(jnp.maximum(n, 1).astype(F32) / max_exact)
                         / math.log(REL_MAX_DIST / max_exact)
                         * (REL_BUCKETS - max_exact)).astype(jnp.int32)
    return jnp.where(n < max_exact, n, jnp.minimum(large, REL_BUCKETS - 1))


def _bias_kernel(tab_ref, bp_ref, bs_ref, *, t_dec):
    h = pl.program_id(0)

    def lookup(n):
        bucket = _bucket(n)
        val = jnp.full(n.shape, tab_ref[h, REL_BUCKETS - 1], F32)
        for b in range(REL_BUCKETS - 1):
            val = jnp.where(bucket == b, tab_ref[h, b], val)
        return val

    tq = bp_ref.shape[2]
    key = lax.broadcasted_iota(jnp.int32, (tq, tq), 0)
    qry = lax.broadcasted_iota(jnp.int32, (tq, tq), 1)
    bp_ref[0, 0] = jnp.where(key <= qry, lookup(jnp.maximum(qry - key, 0)) * LOG2E, NEG_INF)
    bp_ref[0, 1] = lookup(qry - key + tq) * LOG2E

    rows, cols = bs_ref.shape[1], bs_ref.shape[2]
    r = lax.broadcasted_iota(jnp.int32, (rows, cols), 0)
    c = lax.broadcasted_iota(jnp.int32, (rows, cols), 1)
    t = r % t_dec
    tok = c // A_HEADS
    valid = (c % A_HEADS) == h
    far = jnp.full((rows, cols), tab_ref[h, REL_BUCKETS - 1], F32)
    bs_ref[0] = jnp.where(valid, far, NEG_INF)
    bs_ref[1] = jnp.where(valid, lookup(PAGE_SIZE + t - tok), NEG_INF)
    new_ok = valid & (tok <= t) & (tok < t_dec)
    bs_ref[2] = jnp.where(new_ok, lookup(jnp.maximum(t - tok, 0)), NEG_INF)


def _bias_tiles(table, t_dec):
    rows = 2 * t_dec
    return pl.pallas_call(
        functools.partial(_bias_kernel, t_dec=t_dec),
        grid=(A_HEADS,),
        in_specs=[pl.BlockSpec(memory_space=pltpu.SMEM)],
        out_specs=[pl.BlockSpec((1, 2, ATT_TQ, ATT_TQ), lambda h: (h, 0, 0, 0)),
                   pl.BlockSpec((3, rows, PAGE_SIZE * A_HEADS), lambda h: (0, h, 0))],
        out_shape=[jax.ShapeDtypeStruct((A_HEADS, 2, ATT_TQ, ATT_TQ), F32),
                   jax.ShapeDtypeStruct((3, A_HEADS * rows, PAGE_SIZE * A_HEADS), F32)],
        compiler_params=_cparams(1),
        name="bias_tiles",
    )(table.T)


def _lam(lamv_ref, lam_init):
    s1 = jnp.sum(lamv_ref[0:1, :] * lamv_ref[1:2, :], axis=1, keepdims=True)
    s2 = jnp.sum(lamv_ref[2:3, :] * lamv_ref[3:4, :], axis=1, keepdims=True)
    return jnp.exp(s1) - jnp.exp(s2) + lam_init


def _subln(o, g_ref, lam_init):
    y = o * lax.rsqrt(jnp.mean(o * o, axis=-1, keepdims=True) + EPS)
    return y * g_ref[...] * (1.0 - lam_init)


def _attn_prompt_kernel(tab_ref, q_ref, k_ref, v_ref, bias_ref, lamv_ref, subg_ref, o_ref,
                        kb, vt, m_s, acc_s, *, lam_init):
    h = pl.program_id(1)
    qi = pl.program_id(2)
    tq = q_ref.shape[1]

    @pl.when(qi == 0)
    def _():
        kb[...] = k_ref[0].astype(BF16)
        for jj in range(vt.shape[0]):
            vt[jj, :A_V_DIM] = jnp.transpose(v_ref[0, jj * tq:(jj + 1) * tq, :]).astype(BF16)
            vt[jj, A_V_DIM:] = jnp.ones((vt.shape[1] - A_V_DIM, tq), BF16)

    qt = jnp.transpose(q_ref[0]) * LOG2E
    sub = lax.broadcasted_iota(jnp.int32, qt.shape, 0)
    qm = (jnp.where(sub < A_QK_DIM, qt, 0.0).astype(BF16),
          jnp.where(sub >= A_QK_DIM, qt, 0.0).astype(BF16))
    m_s[...] = jnp.full(m_s.shape, NEG_INF, F32)
    acc_s[...] = jnp.zeros(acc_s.shape, F32)

    def step(j, bias):
        start = pl.multiple_of(j * tq, tq)
        kj = kb[pl.ds(start, tq), :]
        vj = vt[j]
        for m in range(2):
            m_prev = m_s[m]
            s = _dot(kj, qm[m]) + bias
            m_new = jnp.maximum(m_prev, jnp.max(s, axis=0, keepdims=True))
            p = jnp.exp2(s - m_new)
            acc_s[m] = jnp.exp2(m_prev - m_new) * acc_s[m] + _dot(vj, p.astype(BF16))
            m_s[m] = m_new

    far = tab_ref[h, REL_BUCKETS - 1] * LOG2E

    def far_body(j, carry):
        step(j, far)
        return carry

    lax.fori_loop(0, jnp.maximum(qi - 1, 0), far_body, 0)

    @pl.when(qi >= 1)
    def _():
        step(qi - 1, bias_ref[0, 1])

    step(qi, bias_ref[0, 0])

    lam = _lam(lamv_ref, lam_init)
    d = A_V_DIM
    o = (acc_s[0, :d] * (1.0 / acc_s[0, d:d + 1])
         - lam * (acc_s[1, :d] * (1.0 / acc_s[1, d:d + 1])))
    y = o * lax.rsqrt(jnp.mean(o * o, axis=0, keepdims=True) + EPS)
    y = y * subg_ref[...] * (1.0 - lam_init)
    o_ref[0] = jnp.transpose(y).astype(o_ref.dtype)


def _attn_prompt(zr3, bias_p, tab_t, lamv, subg, lam_init):
    b, s, _ = zr3.shape
    tq = ATT_TQ
    kern = functools.partial(_attn_prompt_kernel, lam_init=lam_init)
    return pl.pallas_call(
        kern,
        grid=(b, A_HEADS, s // tq),
        in_specs=[pl.BlockSpec(memory_space=pltpu.SMEM),
                  pl.BlockSpec((1, tq, 128), lambda bi, h, qi: (bi, qi, COL_AQ // 128 + h)),
                  pl.BlockSpec((1, s, 128), lambda bi, h, qi: (bi, 0, COL_AK // 128 + h)),
                  pl.BlockSpec((1, s, 128), lambda bi, h, qi: (bi, 0, COL_AV // 128 + h)),
                  pl.BlockSpec((1, 2, tq, tq), lambda bi, h, qi: (h, 0, 0, 0)),
                  pl.BlockSpec((4, A_QK_DIM), lambda bi, h, qi: (0, 0)),
                  pl.BlockSpec((A_V_DIM, 1), lambda bi, h, qi: (0, 0))],
        out_specs=pl.BlockSpec((1, tq, 128), lambda bi, h, qi: (bi, qi, h)),
        out_shape=jax.ShapeDtypeStruct((b, s, A_WIDTH), BF16),
        scratch_shapes=[pltpu.VMEM((s, 128), BF16),
                        pltpu.VMEM((s // tq, A_V_DIM + ONES_ROWS, tq), BF16),
                        pltpu.VMEM((2, 1, tq), F32),
                        pltpu.VMEM((2, A_V_DIM + ONES_ROWS, tq), F32)],
        compiler_params=_cparams(3),
        name="attn_prompt",
    )(tab_t, zr3, zr3, zr3, bias_p, lamv, subg.reshape(A_V_DIM, 1))


def _attn_sample_kernel(pt_ref, q_ref, kn_ref, vn_ref, bias_ref, lamv_ref, subg_ref, *rest,
                        lam_init, n_pg, t_dec):
    k_refs = rest[:n_pg]
    v_refs = rest[n_pg:2 * n_pg]
    o_ref = rest[2 * n_pg]
    q_s, m_s, l_s, acc_s = rest[2 * n_pg + 1:]
    j = pl.program_id(1)
    last = pl.num_programs(1) - 1
    rows = 2 * t_dec

    @pl.when(j == 0)
    def _():
        q = q_ref[0]
        lane = lax.broadcasted_iota(jnp.int32, (t_dec, 128), 1)
        for h in range(A_HEADS):
            qh = q[:, h * 128:(h + 1) * 128]
            q_s[h * rows:h * rows + t_dec, :] = jnp.where(lane < A_QK_DIM, qh, 0.0)
            q_s[h * rows + t_dec:(h + 1) * rows, :] = jnp.where(lane >= A_QK_DIM, qh, 0.0)
        m_s[...] = jnp.full(m_s.shape, NEG_INF, F32)
        l_s[...] = jnp.zeros(l_s.shape, F32)
        acc_s[...] = jnp.zeros(acc_s.shape, F32)

    qb = q_s[...].astype(BF16)

    def update(ks, vs, biases):
        ss = [_dot_nt(qb, kb) + bias for kb, bias in zip(ks, biases)]
        smax = functools.reduce(jnp.maximum, ss)
        m_prev = m_s[...]
        m_new = jnp.maximum(m_prev, jnp.max(smax, axis=1, keepdims=True))
        ps = [jnp.exp(s - m_new) for s in ss]
        alpha = jnp.exp(m_prev - m_new)
        l_s[...] = alpha * l_s[...] + jnp.sum(functools.reduce(jnp.add, ps), axis=1, keepdims=True)
        pv = functools.reduce(jnp.add, [_dot(p.astype(BF16), vb) for p, vb in zip(ps, vs)])
        acc_s[...] = alpha * acc_s[...] + pv
        m_s[...] = m_new

    biases = [bias_ref[0]] * (n_pg - 1) + [jnp.where(j == last, bias_ref[1], bias_ref[0])]
    update([r[0, 0].astype(BF16) for r in k_refs], [r[0, 0].astype(BF16) for r in v_refs], biases)

    @pl.when(j == last)
    def _():
        nk = kn_ref.shape[1]
        update([kn_ref[0].astype(BF16)], [vn_ref[0].astype(BF16)], [bias_ref[2][:, :nk]])
        lam = _lam(lamv_ref, lam_init)
        on = acc_s[...] * (1.0 / l_s[...])
        for h in range(A_HEADS):
            o = on[h * rows:h * rows + t_dec] - lam * on[h * rows + t_dec:(h + 1) * rows]
            o_ref[0, :, h * 128:(h + 1) * 128] = _subln(o, subg_ref, lam_init).astype(o_ref.dtype)


def _attn_sample(zrs3, kn2d, vn2d, cache_k2d, cache_v2d, page_table, bias_s, lamv, subg,
                 layer, lam_init):
    bd, t_dec, _ = zrs3.shape
    n_pages = page_table.shape[1]
    n_pg = min(SAMPLE_PAGES, n_pages)
    rows = 2 * t_dec * A_HEADS
    pcols = PAGE_SIZE * A_HEADS

    def page_spec(p):
        return pl.BlockSpec((1, 1, pcols, 128),
                            lambda b, j, pt: (layer, pt[b, j * n_pg + p], 0, 0))

    kern = functools.partial(_attn_sample_kernel, lam_init=lam_init, n_pg=n_pg, t_dec=t_dec)
    grid_spec = pltpu.PrefetchScalarGridSpec(
        num_scalar_prefetch=1,
        grid=(bd, n_pages // n_pg),
        in_specs=[pl.BlockSpec((1, t_dec, A_WIDTH), lambda b, j, pt: (b, 0, 0)),
                  pl.BlockSpec((1,) + kn2d.shape[1:], lambda b, j, pt: (b, 0, 0)),
                  pl.BlockSpec((1,) + vn2d.shape[1:], lambda b, j, pt: (b, 0, 0)),
                  pl.BlockSpec((3, rows, pcols), lambda b, j, pt: (0, 0, 0)),
                  pl.BlockSpec((4, A_QK_DIM), lambda b, j, pt: (0, 0)),
                  pl.BlockSpec((1, A_V_DIM), lambda b, j, pt: (0, 0))]
                 + [page_spec(p) for p in range(n_pg)] * 2,
        out_specs=pl.BlockSpec((1, t_dec, A_WIDTH), lambda b, j, pt: (b, 0, 0)),
        scratch_shapes=[pltpu.VMEM((rows, 128), F32), pltpu.VMEM((rows, 1), F32),
                        pltpu.VMEM((rows, 1), F32), pltpu.VMEM((rows, 128), F32)],
    )
    return pl.pallas_call(
        kern,
        grid_spec=grid_spec,
        out_shape=jax.ShapeDtypeStruct((bd, t_dec, A_WIDTH), BF16),
        compiler_params=_cparams(2),
        name="attn_sample",
    )(page_table, zrs3, kn2d, vn2d, bias_s, lamv, subg,
      *([cache_k2d] * n_pg), *([cache_v2d] * n_pg))


def _gelu(x):
    return 0.5 * x * (1.0 + jnp.tanh(math.sqrt(2.0 / math.pi) * (x + 0.044715 * (x * x * x))))


def _chunk_mlp_kernel(u_ref, v_ref, w_ref, bias_ref, g_ref, b_ref, y_ref, vb_ref):
    n_chunks = u_ref.shape[0] // B_CHUNK
    lane_grp = lax.broadcasted_iota(jnp.int32, (B_CHUNK, B_WIDTH), 1) // (B_WIDTH // B_GROUPS)
    for c in range(n_chunks):
        sl = slice(c * B_CHUNK, (c + 1) * B_CHUNK)
        gv = _gelu(v_ref[sl, :])
        xc = gv - jnp.mean(gv, axis=-1, keepdims=True)
        vb = xc * lax.rsqrt(jnp.mean(xc * xc, axis=-1, keepdims=True) + EPS) * g_ref[...] + b_ref[...]
        vb_ref[sl, :] = vb
        vbb = vb.astype(BF16)
        mixed = bias_ref[...]
        for g in range(B_GROUPS):
            mixed = mixed + jnp.where(lane_grp == g, _dot(w_ref[g], vbb), 0.0)
        y_ref[sl, :] = (_gelu(u_ref[sl, :]) * mixed).astype(y_ref.dtype)


def _chunk_mlp(zr, w_eff, bias_eff, ln_g, ln_b):
    m = zr.shape[0]
    tm = min(m, 512)
    return pl.pallas_call(
        _chunk_mlp_kernel,
        grid=(m // tm,),
        in_specs=[pl.BlockSpec((tm, B_WIDTH), lambda i: (i, COL_BU // B_WIDTH)),
                  pl.BlockSpec((tm, B_WIDTH), lambda i: (i, COL_BV // B_WIDTH)),
                  pl.BlockSpec((B_GROUPS, B_CHUNK, B_CHUNK), lambda i: (0, 0, 0)),
                  pl.BlockSpec((B_CHUNK, B_WIDTH), lambda i: (0, 0)),
                  pl.BlockSpec((1, B_WIDTH), lambda i: (0, 0)),
                  pl.BlockSpec((1, B_WIDTH), lambda i: (0, 0))],
        out_specs=[pl.BlockSpec((tm, B_WIDTH), lambda i: (i, 0)),
                   pl.BlockSpec((tm, B_WIDTH), lambda i: (i, 0))],
        out_shape=[jax.ShapeDtypeStruct((m, B_WIDTH), BF16),
                   jax.ShapeDtypeStruct((m, B_WIDTH), F32)],
        compiler_params=_cparams(1),
        name="chunk_mlp",
    )(zr, zr, w_eff, bias_eff, ln_g, ln_b)


def _pool_kernel(x_ref, w_ref, scale_ref, y_ref, *, prefix, pos0):
    x = x_ref[0]
    rows = x.shape[0]
    row = lax.broadcasted_iota(jnp.int32, x.shape, 0)
    grp = lax.broadcasted_iota(jnp.int32, x.shape, 1) // C_GROUP_DIM

    def shifted(a, k):
        return jnp.where(row >= k, pltpu.roll(a, k, 0), 0.0)

    sums = []
    acc = x
    for k in (1, 2, 4, 8):
        acc = acc + shifted(acc, k)
        sums.append(acc)
    total = sums[3]
    win = jnp.full(x.shape, POOL_WINDOWS[3], jnp.int32)
    for g in range(3):
        total = jnp.where(grp == g, sums[g], total)
        win = jnp.where(grp == g, POOL_WINDOWS[g], win)
    pos = pos0 + row - prefix
    cnt = jnp.clip(pos + 1, 1, win).astype(F32)
    d = total / cnt - x
    y = _dot(d.astype(BF16), w_ref[...]) * scale_ref[...]
    y_ref[0] = y.astype(y_ref.dtype)
    del rows


def _pool_mix(xx, col_block, w_bd, scale, prefix, pos0):
    b, rows, _ = xx.shape
    kern = functools.partial(_pool_kernel, prefix=prefix, pos0=pos0)
    return pl.pallas_call(
        kern,
        grid=(b,),
        in_specs=[pl.BlockSpec((1, rows, C_WIDTH), lambda i: (i, 0, col_block)),
                  pl.BlockSpec((C_WIDTH, C_WIDTH), lambda i: (0, 0)),
                  pl.BlockSpec((1, C_WIDTH), lambda i: (0, 0))],
        out_specs=pl.BlockSpec((1, rows, C_WIDTH), lambda i: (i, 0, 0)),
        out_shape=jax.ShapeDtypeStruct((b, rows, C_WIDTH), BF16),
        compiler_params=_cparams(1),
        name="pool_mix",
    )(xx, w_bd, scale)


def _log_sigmoid(x):
    return jnp.minimum(x, 0.0) - jnp.log1p(jnp.exp(-jnp.abs(x)))


def _split3(x):
    def top8(a):
        bits = lax.bitcast_convert_type(a, jnp.int32) & jnp.int32(-65536)
        return lax.bitcast_convert_type(bits, F32)

    p1 = top8(x)
    r1 = x - p1
    p2 = top8(r1)
    return p1, p2, r1 - p2


def _mlstm_kernel(q_ref, k_ref, v_ref, o_ref, gi_ref, gf_ref, gb_ref, ng_ref, sel_ref, bd_ref,
                  c0_ref, n0_ref, m0_ref, y_ref, c_out, n_out, m_out, c_s, n_s, m_s, *, t_valid):
    ci = pl.program_id(1)
    nb, chunk = q_ref.shape[0], q_ref.shape[1]
    hd = D_HEAD_DIM
    eye_h = (lax.broadcasted_iota(jnp.int32, (hd, hd), 0)
             == lax.broadcasted_iota(jnp.int32, (hd, hd), 1))

    @pl.when(ci == 0)
    def _():
        c_s[...] = jnp.zeros(c_s.shape, F32)
        n_s[...] = jnp.zeros(n_s.shape, F32)
        m_s[...] = m0_ref[...]
        for bi in range(nb):
            for h in range(D_HEADS):
                hs = slice(h * hd, (h + 1) * hd)
                c_s[bi, hs, hs] = c0_ref[bi, h]
                n_col = jnp.sum(jnp.where(eye_h, n0_ref[bi, h], 0.0), axis=1, keepdims=True)
                n_s[bi, hs, hs] = jnp.broadcast_to(n_col, (hd, hd))

    row = lax.broadcasted_iota(jnp.int32, (chunk, 128), 0)
    lane = lax.broadcasted_iota(jnp.int32, (chunk, 128), 1)
    head_lane = lane < D_HEADS
    grp = lax.broadcasted_iota(jnp.int32, (chunk, D_WIDTH), 1) // hd
    rr = lax.broadcasted_iota(jnp.int32, (chunk, chunk), 0)
    cc = lax.broadcasted_iota(jnp.int32, (chunk, chunk), 1)
    allowed = (cc <= rr) & (cc < t_valid)
    ones_t = jnp.ones((chunk, 128), F32)
    ones_w = jnp.ones((chunk, D_WIDTH), BF16)
    same_head = (lax.broadcasted_iota(jnp.int32, (D_WIDTH, D_WIDTH), 0) // hd
                 == lax.broadcasted_iota(jnp.int32, (D_WIDTH, D_WIDTH), 1) // hd)
    pick = [lane == h for h in range(D_HEADS)]
    in_head = [grp == h for h in range(D_HEADS)]
    pick_one = [jnp.where(pk, 1.0, 0.0) for pk in pick]
    head_one = [jnp.where(ih, 1.0, 0.0).astype(BF16) for ih in in_head]

    for bi in range(nb):
        gi = gi_ref[bi] + gb_ref[0:1, :]
        lf = _log_sigmoid(gf_ref[bi] + gb_ref[1:2, :])
        if t_valid < chunk:
            gi = jnp.where(row < t_valid, gi, NEG_INF)
            lf = jnp.where(row < t_valid, lf, 0.0)
        b = lf
        k = 1
        while k < chunk:
            b = b + jnp.where(row >= k, pltpu.roll(b, k, 0), 0.0)
            k *= 2
        u = gi - b
        cm = u
        k = 1
        while k < chunk:
            cm = jnp.maximum(cm, jnp.where(row >= k, pltpu.roll(cm, k, 0), NEG_INF))
            k *= 2
        m_prev = m_s[bi]
        big_m = jnp.maximum(m_prev, cm)
        m_last = big_m[chunk - 1:chunk, :]
        winter = jnp.exp(m_prev - big_m)
        emt = jnp.exp(jnp.minimum(-(b + big_m), EMT_CAP))
        ws = jnp.exp(u - m_last)
        m_s[bi] = b[chunk - 1:chunk, :] + m_last

        def per_head_lanes(z):
            pieces = _split3(jnp.where(head_lane, z, 0.0))
            return _dot(jnp.concatenate(pieces, axis=1).astype(BF16), sel_ref[...])

        winter_r, emt_r, ws_r = per_head_lanes(winter), per_head_lanes(emt), per_head_lanes(ws)
        decay_r = winter_r[chunk - 1:chunk, :]

        q = q_ref[bi]
        qb = q.astype(BF16)
        kf = k_ref[bi] * (hd ** -0.5)
        kb = kf.astype(BF16)
        vf = v_ref[bi]
        vb = vf.astype(BF16)
        u_fin = jnp.where(head_lane & (row < t_valid), u, 0.0)
        y_side = jnp.concatenate(_split3(u_fin) + (ones_t, ones_t, ones_t), axis=1).astype(BF16)
        m_neg = _split3(jnp.where(head_lane, -big_m, 0.0))
        acc = jnp.zeros((chunk, 2 * D_WIDTH), F32)
        for h in range(D_HEADS):
            x_side = jnp.concatenate(
                [pick_one[h]] * 3 + [jnp.where(pick[h], piece, 0.0) for piece in m_neg],
                axis=1).astype(BF16)
            expo = jnp.where(allowed, _dot_nt(x_side, y_side), NEG_INF)
            qk = _dot_nt(jnp.where(in_head[h], q, 0.0).astype(BF16), kb) * jnp.exp(expo)
            rhs = jnp.concatenate([jnp.where(in_head[h], vf, 0.0).astype(BF16), head_one[h]], axis=1)
            acc = acc + _dot(qk.astype(BF16), rhs)
        state = jnp.concatenate([c_s[bi].astype(BF16), n_s[bi].astype(BF16)], axis=1)
        inter = _dot(qb, state)
        num = winter_r * inter[:, :D_WIDTH] + acc[:, :D_WIDTH]
        den = winter_r * inter[:, D_WIDTH:] + acc[:, D_WIDTH:]
        hh = num / jnp.maximum(jnp.abs(den), emt_r)
        h2 = hh * hh
        hi = h2.astype(BF16)
        lo = (h2 - hi.astype(F32)).astype(BF16)
        ssq = _dot(hi, bd_ref[...]) + _dot(lo, bd_ref[...])
        y = hh * lax.rsqrt(ssq * (1.0 / hd) + EPS) * ng_ref[...]
        y_ref[bi] = (y * jax.nn.sigmoid(o_ref[bi])).astype(y_ref.dtype)
        kw = (ws_r * kf).astype(BF16)
        upd = _dot_tn(kw, jnp.concatenate([vb, ones_w], axis=1))
        c_s[bi] = decay_r * c_s[bi] + jnp.where(same_head, upd[:, :D_WIDTH], 0.0)
        n_s[bi] = decay_r * n_s[bi] + jnp.where(same_head, upd[:, D_WIDTH:], 0.0)

    @pl.when(ci == pl.num_programs(1) - 1)
    def _():
        m_out[...] = m_s[...]
        for bi in range(nb):
            for h in range(D_HEADS):
                hs = slice(h * hd, (h + 1) * hd)
                c_out[bi, h] = c_s[bi, hs, hs]
                n_out[bi, h] = jnp.sum(jnp.where(eye_h, n_s[bi, hs, hs], 0.0), axis=0, keepdims=True)


def _mlstm(src, col0, gate_block, chunk, t_valid, gate_bias, norm_g, c0, n0, m0):
    b, t, _ = src.shape
    nb = math.gcd(b, MLSTM_BATCH)
    hd = D_HEAD_DIM
    kern = functools.partial(_mlstm_kernel, t_valid=t_valid)
    head_of_lane = jnp.arange(D_WIDTH) // hd
    sel = (jnp.arange(128)[:, None] == head_of_lane[None, :]).astype(BF16)
    sel3 = jnp.concatenate([sel, sel, sel], axis=0)
    same_head = (head_of_lane[:, None] == head_of_lane[None, :]).astype(BF16)
    m0p = jnp.pad(m0, ((0, 0), (0, 0), (0, 128 - D_HEADS)))

    def col(cb, width=D_WIDTH):
        return pl.BlockSpec((nb, chunk, width), lambda bi, ci: (bi, ci, cb))

    def const(shape):
        return pl.BlockSpec(shape, lambda bi, ci: (0,) * len(shape))

    def per_seq(shape):
        return pl.BlockSpec((nb,) + shape, lambda bi, ci: (bi,) + (0,) * len(shape))

    y, c_new, n_new, m_new = pl.pallas_call(
        kern,
        grid=(b // nb, t // chunk),
        in_specs=[col(col0), col(col0 + 1), col(col0 + 2), col(col0 + 3),
                  col(gate_block, 128), col(gate_block + 1, 128),
                  const((2, 128)), const((1, D_WIDTH)), const((3 * 128, D_WIDTH)),
                  const((D_WIDTH, D_WIDTH)),
                  per_seq((D_HEADS, hd, hd)), per_seq((D_HEADS, 1, hd)), per_seq((1, 128))],
        out_specs=[col(0), per_seq((D_HEADS, hd, hd)), per_seq((D_HEADS, 1, hd)),
                   per_seq((1, 128))],
        out_shape=[jax.ShapeDtypeStruct((b, t, D_WIDTH), BF16),
                   jax.ShapeDtypeStruct((b, D_HEADS, hd, hd), F32),
                   jax.ShapeDtypeStruct((b, D_HEADS, 1, hd), F32),
                   jax.ShapeDtypeStruct((b, 1, 128), F32)],
        scratch_shapes=[pltpu.VMEM((nb, D_WIDTH, D_WIDTH), F32),
                        pltpu.VMEM((nb, D_WIDTH, D_WIDTH), F32),
                        pltpu.VMEM((nb, 1, 128), F32)],
        compiler_params=_cparams(2),
        name="mlstm",
    )(src, src, src, src, src, src, gate_bias, jnp.tile(norm_g, (1, D_HEADS)), sel3, same_head,
      c0, n0, m0p)
    return y, c_new, n_new, m_new[:, :, :D_HEADS]


def _merge_kernel(x_ref, g_ref, ya_ref, yb_ref, yc_ref, yd_ref, wa_ref, wb_ref, wc_ref, wd_ref,
                  wo_ref, o_ref):
    def gate(i):
        return g_ref[:, i * D_MODEL:(i + 1) * D_MODEL].astype(F32)

    merged = gate(0) * _dot(ya_ref[...], wa_ref[...])
    merged = merged + gate(1) * _dot(yb_ref[...], wb_ref[...])
    merged = merged + gate(2) * _dot(yc_ref[...], wc_ref[...])
    merged = merged + gate(3) * _dot(yd_ref[...], wd_ref[...])
    o_ref[...] = x_ref[...] + _dot(merged.astype(BF16), wo_ref[...])


def _merge(x, gates, ya, yb, yc, yd, wa, wb, wc, wd, wo):
    m = x.shape[0]
    tm = min(m, 512)

    def rows(width):
        return pl.BlockSpec((tm, width), lambda i: (i, 0))

    def full(arr):
        return pl.BlockSpec(arr.shape, lambda i: (0, 0))

    return pl.pallas_call(
        _merge_kernel,
        grid=(m // tm,),
        in_specs=[rows(D_MODEL), rows(GZ_WIDTH), rows(A_WIDTH), rows(B_WIDTH), rows(C_WIDTH),
                  rows(D_WIDTH), full(wa), full(wb), full(wc), full(wd), full(wo)],
        out_specs=rows(D_MODEL),
        out_shape=jax.ShapeDtypeStruct((m, D_MODEL), F32),
        compiler_params=_cparams(1),
        name="merge",
    )(x, gates, ya, yb, yc, yd, wa, wb, wc, wd, wo)


def _ffn_kernel(x_ref, g_ref, w1_ref, w2_ref, o_ref, h_s, acc_s):
    j = pl.program_id(1)

    @pl.when(j == 0)
    def _():
        x = x_ref[...]
        y = x * lax.rsqrt(jnp.mean(x * x, axis=-1, keepdims=True) + EPS)
        h_s[...] = (y * g_ref[...]).astype(BF16)
        acc_s[...] = jnp.zeros(acc_s.shape, F32)

    a = jnp.maximum(_dot(h_s[...], w1_ref[...]), 0.0)
    acc_s[...] += _dot((a * a).astype(BF16), w2_ref[...])

    @pl.when(j == pl.num_programs(1) - 1)
    def _():
        o_ref[...] = x_ref[...] + acc_s[...]


def _ffn(x, g, w1, w2):
    m = x.shape[0]
    tm, tf = min(m, 1024), 1024
    return pl.pallas_call(
        _ffn_kernel,
        grid=(m // tm, D_FF // tf),
        in_specs=[pl.BlockSpec((tm, D_MODEL), lambda i, j: (i, 0)),
                  pl.BlockSpec((1, D_MODEL), lambda i, j: (0, 0)),
                  pl.BlockSpec((D_MODEL, tf), lambda i, j: (0, j)),
                  pl.BlockSpec((tf, D_MODEL), lambda i, j: (j, 0))],
        out_specs=pl.BlockSpec((tm, D_MODEL), lambda i, j: (i, 0)),
        out_shape=jax.ShapeDtypeStruct((m, D_MODEL), F32),
        scratch_shapes=[pltpu.VMEM((tm, D_MODEL), BF16), pltpu.VMEM((tm, D_MODEL), F32)],
        compiler_params=_cparams(2),
        name="ffn",
    )(x, g.reshape(1, D_MODEL), w1, w2)


def _layer_weights(p, l):
    w_in = p["w_in"][l]
    gate_pad = jnp.zeros((D_MODEL, 128 - D_HEADS), F32)
    wr = jnp.concatenate(
        [w_in[:, :3072], w_in[:, 3080:3336], w_in[:, 3072:3076], gate_pad,
         w_in[:, 3076:3080], gate_pad], axis=1).astype(BF16)
    wg = w_in[:, 3336:].astype(BF16)
    reps = A_WIDTH // A_QK_DIM
    gains = jnp.stack([jnp.tile(p["q_norm_g"][l], reps) * (A_QK_DIM ** -0.5),
                       jnp.tile(p["k_norm_g"][l], reps)]).reshape(2, 1, A_WIDTH)
    grp = jnp.arange(A_WIDTH) // A_QK_DIM
    ones_bd = (grp[:, None] == grp[None, :]).astype(BF16)
    lamv = jnp.stack([p["lam_q1"][l], p["lam_k1"][l], p["lam_q2"][l], p["lam_k2"][l]])
    gate_bias = jnp.pad(jnp.stack([p["d_i_bias"][l], p["d_f_bias"][l]]),
                        ((0, 0), (0, 128 - D_HEADS)))
    c_bd = jnp.zeros((C_WIDTH, C_WIDTH), F32)
    for g in range(4):
        sl = slice(g * C_GROUP_DIM, (g + 1) * C_GROUP_DIM)
        c_bd = c_bd.at[sl, sl].set(p["c_lin"][l][g])
    return dict(
        norm1_g=p["norm1_g"][l], norm2_g=p["norm2_g"][l], wr=wr, wg=wg, gains=gains,
        ones_bd=ones_bd, lamv=lamv, subg=p["subln_g"][l].reshape(1, A_V_DIM),
        b_ln_g=p["b_ln_g"][l].reshape(1, B_WIDTH), b_ln_b=p["b_ln_b"][l].reshape(1, B_WIDTH),
        b_ws=p["b_ws"][l], b_bias=p["b_bias"][l],
        c_bd=c_bd.astype(BF16), c_scale=p["c_scale"][l].reshape(1, C_WIDTH),
        gate_bias=gate_bias, d_norm_g=p["d_norm_g"][l].reshape(1, D_HEAD_DIM),
        w_pa=p["w_pa"][l].astype(BF16), w_pb=p["w_pb"][l].astype(BF16),
        w_pc=p["w_pc"][l].astype(BF16), w_pd=p["w_pd"][l].astype(BF16),
        w_out=p["w_out"][l].astype(BF16), w_ff1=p["w_ff1"][l].astype(BF16),
        w_ff2=p["w_ff2"][l].astype(BF16),
        lam_init=0.8 - 0.6 * math.exp(-0.3 * l),
    )


def _chunk_weights(w, t):
    length = min(t, B_CHUNK)
    ws = jnp.tril(w["b_ws"][:, :length, :length])
    bias = jnp.transpose(w["b_bias"][:, :length])
    reps = B_CHUNK // length
    if reps > 1:
        eye = jnp.eye(reps, dtype=F32)
        ws = jax.vmap(lambda a: jnp.kron(eye, a))(ws)
        bias = jnp.tile(bias, (reps, 1))
    return ws.astype(BF16), jnp.repeat(bias, B_WIDTH // B_GROUPS, axis=1)


def _finish_layer(x2, w, gates, zr, ya, yc, yd, t):
    w_eff, bias_eff = _chunk_weights(w, t)
    yb, vb = _chunk_mlp(zr, w_eff, bias_eff, w["b_ln_g"], w["b_ln_b"])
    x2 = _merge(x2, gates, ya, yb, yc, yd, w["w_pa"], w["w_pb"], w["w_pc"], w["w_pd"], w["w_out"])
    return _ffn(x2, w["norm2_g"], w["w_ff1"], w["w_ff2"]), vb


def _prompt_layer(x2, w, b, s, bias_p, tab_t):
    m = b * s
    gates, zr, k_rows, v_rows = _in_proj(x2, w["norm1_g"], w["wg"], w["wr"], w["gains"],
                                         w["ones_bd"])
    zr3 = zr.reshape(b, s, ZR_WIDTH)
    ya = _attn_prompt(zr3, bias_p, tab_t, w["lamv"], w["subg"], w["lam_init"]).reshape(m, A_WIDTH)
    yc = _pool_mix(zr3, COL_CX // C_WIDTH, w["c_bd"], w["c_scale"], 0, 0).reshape(m, C_WIDTH)
    hd = D_HEAD_DIM
    yd, c_new, n_new, m_new = _mlstm(
        zr3, COL_DQ // D_WIDTH, COL_DGI // 128, min(s, 128), min(s, 128), w["gate_bias"],
        w["d_norm_g"], jnp.zeros((b, D_HEADS, hd, hd), F32), jnp.zeros((b, D_HEADS, 1, hd), F32),
        jnp.zeros((b, 1, D_HEADS), F32))
    x2, _ = _finish_layer(x2, w, gates, zr, ya, yc, yd.reshape(m, D_WIDTH), s)
    outs = (k_rows.reshape(b, s, A_HEADS, 2 * A_QK_DIM), v_rows.reshape(b, s, A_HEADS, A_V_DIM),
            zr3[:, s - POOL_BUF:, COL_CX:COL_CX + C_WIDTH],
            c_new, n_new.reshape(b, D_HEADS, hd), m_new.reshape(b, D_HEADS))
    return x2, outs


def _sample_layer(x2, w, l, bd, t, cache_k2d, cache_v2d, page_table, bias_s, pool0, c0, n0, m0):
    m = bd * t
    past = page_table.shape[1] * PAGE_SIZE
    gates, zr, kn, vn = _in_proj(x2, w["norm1_g"], w["wg"], w["wr"], w["gains"], w["ones_bd"])
    zr3 = zr.reshape(bd, t, ZR_WIDTH)
    ya = _attn_sample(zr3, kn.reshape(bd, t * A_HEADS, 128), vn.reshape(bd, t * A_HEADS, 128),
                      cache_k2d, cache_v2d, page_table, bias_s, w["lamv"], w["subg"], l,
                      w["lam_init"]).reshape(m, A_WIDTH)
    cx = zr3[:, :, COL_CX:COL_CX + C_WIDTH]
    prefix = POOL_BUF + 1
    rows = -(-(prefix + t) // 8) * 8
    xx = jnp.concatenate([jnp.zeros((bd, 1, C_WIDTH), F32), pool0, cx,
                          jnp.zeros((bd, rows - prefix - t, C_WIDTH), F32)], axis=1)
    yc = _pool_mix(xx, 0, w["c_bd"], w["c_scale"], prefix, past)[:, prefix:prefix + t]
    yc = yc.reshape(m, C_WIDTH)
    chunk = 128
    dsrc = jnp.pad(zr3[:, :, COL_DQ:COL_DGF + 128], ((0, 0), (0, chunk - t), (0, 0)))
    hd = D_HEAD_DIM
    yd, c_new, n_new, m_new = _mlstm(
        dsrc, 0, (COL_DGI - COL_DQ) // 128, chunk, t, w["gate_bias"], w["d_norm_g"],
        c0, n0.reshape(bd, D_HEADS, 1, hd), m0.reshape(bd, 1, D_HEADS))
    yd = yd[:, :t].reshape(m, D_WIDTH)
    x2, vb = _finish_layer(x2, w, gates, zr, ya, yc, yd, t)
    outs = (kn.reshape(bd, t, A_HEADS, 2 * A_QK_DIM), vn.reshape(bd, t, A_HEADS, A_V_DIM),
            vb.reshape(bd, t, B_WIDTH), jnp.concatenate([pool0, cx], axis=1)[:, -POOL_BUF:],
            c_new, n_new.reshape(bd, D_HEADS, hd), m_new.reshape(bd, D_HEADS))
    return x2, outs


def kernel(x_prompt, x_sample, cache_k, cache_v, page_table, state_pool, state_C, state_n, state_m, rel_bias, norm1_g, norm2_g, w_in, q_norm_g, k_norm_g, lam_q1, lam_k1, lam_q2, lam_k2, subln_g, b_ln_g, b_ln_b, b_ws, b_bias, c_lin, c_scale, d_i_bias, d_f_bias, d_norm_g, w_pa, w_pb, w_pc, w_pd, w_out, w_ff1, w_ff2):
    p = dict(norm1_g=norm1_g, norm2_g=norm2_g, w_in=w_in, q_norm_g=q_norm_g, k_norm_g=k_norm_g,
             lam_q1=lam_q1, lam_k1=lam_k1, lam_q2=lam_q2, lam_k2=lam_k2, subln_g=subln_g,
             b_ln_g=b_ln_g, b_ln_b=b_ln_b, b_ws=b_ws, b_bias=b_bias, c_lin=c_lin, c_scale=c_scale,
             d_i_bias=d_i_bias, d_f_bias=d_f_bias, d_norm_g=d_norm_g, w_pa=w_pa, w_pb=w_pb,
             w_pc=w_pc, w_pd=w_pd, w_out=w_out, w_ff1=w_ff1, w_ff2=w_ff2)
    depth = w_in.shape[0]
    bp, sp, _ = x_prompt.shape
    bd, td, _ = x_sample.shape
    n_phys = cache_k.shape[1]
    cache_k2d = cache_k.reshape(depth, n_phys, PAGE_SIZE * A_HEADS, 128)
    cache_v2d = cache_v.reshape(depth, n_phys, PAGE_SIZE * A_HEADS, 128)
    bias_p, bias_s = _bias_tiles(rel_bias, td)
    tab_t = rel_bias.T

    xp = x_prompt.reshape(bp * sp, D_MODEL)
    xs = x_sample.reshape(bd * td, D_MODEL)
    prompt_outs, sample_outs = [], []
    for l in range(depth):
        w = _layer_weights(p, l)
        xp, po = _prompt_layer(xp, w, bp, sp, bias_p, tab_t)
        xs, so = _sample_layer(xs, w, l, bd, td, cache_k2d, cache_v2d, page_table, bias_s,
                               state_pool[l], state_C[l], state_n[l], state_m[l])
        prompt_outs.append(po)
        sample_outs.append(so)

    def stack(outs, i):
        return jnp.stack([o[i] for o in outs])

    return (xp.reshape(bp, sp, D_MODEL), xs.reshape(bd, td, D_MODEL),
            stack(prompt_outs, 0), stack(prompt_outs, 1), stack(sample_outs, 0), stack(sample_outs, 1),
            stack(sample_outs, 2), stack(prompt_outs, 2), stack(sample_outs, 3),
            stack(prompt_outs, 3), stack(prompt_outs, 4), stack(prompt_outs, 5),
            stack(sample_outs, 4), stack(sample_outs, 5), stack(sample_outs, 6))
```

```python
import functools
import math

import jax
import jax.numpy as jnp
from jax import lax
from jax.experimental import pallas as pl
from jax.experimental.pallas import tpu as pltpu

F32 = jnp.float32
BF16 = jnp.bfloat16
NEG_INF = float("-inf")

D_MODEL = 1024
A_HEADS = 4
A_QK_DIM = 64
A_V_DIM = 128
A_WIDTH = 512
REL_BUCKETS = 32
REL_MAX_DIST = 128
PAGE_SIZE = 128
B_GROUPS = 4
B_WIDTH = 256
B_CHUNK = 128
C_WIDTH = 256
C_GROUP_DIM = 64
POOL_WINDOWS = (2, 4, 8, 16)
POOL_BUF = 15
D_HEADS = 4
D_WIDTH = 256
D_HEAD_DIM = 64
D_FF = 4096
N_BRANCH = 4
EPS = 1e-6

ZR_WIDTH = 3584
COL_AQ, COL_AK, COL_AV = 0, 512, 1024
COL_BU, COL_BV, COL_CX = 1536, 1792, 2048
COL_DQ, COL_DK, COL_DV, COL_DO = 2304, 2560, 2816, 3072
COL_DGI, COL_DGF = 3328, 3456
GZ_WIDTH = N_BRANCH * D_MODEL
EMT_CAP = 80.0
LOG2E = math.log2(math.e)
ONES_ROWS = 16

IN_PROJ_TM = 512
IN_PROJ_TN = 512
ATT_TQ = 512
ATT_HEADS_PER_STEP = 2
SAMPLE_PAGES = 32
MLSTM_BATCH = 4
SAMPLE_MLSTM_CHUNK = 32
VMEM_LIMIT = 56 * 1024 * 1024


def _cparams(n_axes):
    return pltpu.CompilerParams(dimension_semantics=("arbitrary",) * n_axes,
                                vmem_limit_bytes=VMEM_LIMIT)


def _dot(a, b):
    return jnp.dot(a, b, preferred_element_type=F32)


def _dot_nt(a, b):
    return lax.dot_general(a, b, (((1,), (1,)), ((), ())), preferred_element_type=F32)


def _dot_tn(a, b):
    return lax.dot_general(a, b, (((0,), (0,)), ((), ())), preferred_element_type=F32)


def _in_proj_kernel(x_ref, g_ref, wg_ref, wr_ref, gain_ref, ones_ref,
                    gates_ref, zr_ref, kout_ref, vout_ref):
    tm = x_ref.shape[0]
    tn = IN_PROJ_TN
    x = x_ref[...]
    h = (x * lax.rsqrt(jnp.mean(x * x, axis=-1, keepdims=True) + EPS) * g_ref[...]).astype(BF16)

    def head_rows(dst_ref, val):
        for hh in range(A_HEADS):
            dst_ref[pl.ds(hh, tm, stride=A_HEADS), :] = val[:, hh * 128:(hh + 1) * 128]

    for c in range(ZR_WIDTH // tn):
        cols = slice(c * tn, (c + 1) * tn)
        z = _dot(h, wr_ref[:, cols])
        if c * tn in (COL_AQ, COL_AK):
            z2 = z * z
            hi = z2.astype(BF16)
            lo = (z2 - hi.astype(F32)).astype(BF16)
            ssq = _dot(hi, ones_ref[...]) + _dot(lo, ones_ref[...])
            z = z * lax.rsqrt(ssq * (1.0 / A_QK_DIM) + EPS) * gain_ref[c]
        zr_ref[:, cols] = z
        if c * tn == COL_AK:
            head_rows(kout_ref, z)
        if c * tn == COL_AV:
            head_rows(vout_ref, z)

    for c in range(GZ_WIDTH // tn):
        cols = slice(c * tn, (c + 1) * tn)
        gates_ref[:, cols] = jax.nn.sigmoid(_dot(h, wg_ref[:, cols])).astype(gates_ref.dtype)


def _in_proj(x, g, wg, wr, gains, ones_bd):
    m = x.shape[0]
    tm = min(m, IN_PROJ_TM)

    def rows(width):
        return pl.BlockSpec((tm, width), lambda i: (i, 0))

    def resident(arr):
        return pl.BlockSpec(arr.shape, lambda i: (0,) * arr.ndim, pipeline_mode=pl.Buffered(1))

    kv_spec = pl.BlockSpec((tm * A_HEADS, 128), lambda i: (i, 0))
    kv_shape = jax.ShapeDtypeStruct((m * A_HEADS, 128), F32)
    g = g.reshape(1, D_MODEL)
    return pl.pallas_call(
        _in_proj_kernel,
        grid=(m // tm,),
        in_specs=[rows(D_MODEL), resident(g), resident(wg), resident(wr), resident(gains),
                  resident(ones_bd)],
        out_specs=[rows(GZ_WIDTH), rows(ZR_WIDTH), kv_spec, kv_spec],
        out_shape=[jax.ShapeDtypeStruct((m, GZ_WIDTH), BF16),
                   jax.ShapeDtypeStruct((m, ZR_WIDTH), F32), kv_shape, kv_shape],
        compiler_params=_cparams(1),
        name="in_proj",
    )(x, g, wg, wr, gains, ones_bd)


def _bucket(n):
    max_exact = REL_BUCKETS // 2
    large = max_exact + (jnp.log(jnp.maximum(n, 1).astype(F32) / max_exact)
                         / math.log(REL_MAX_DIST / max_exact)
                         * (REL_BUCKETS - max_exact)).astype(jnp.int32)
    return jnp.where(n < max_exact, n, jnp.minimum(large, REL_BUCKETS - 1))


def _bias_kernel(tab_ref, bp_ref, bs_ref, *, t_dec):
    h = pl.program_id(0)

    def lookup(n):
        bucket = _bucket(n)
        val = jnp.full(n.shape, tab_ref[h, REL_BUCKETS - 1], F32)
        for b in range(REL_BUCKETS - 1):
            val = jnp.where(bucket == b, tab_ref[h, b], val)
        return val

    tq = bp_ref.shape[2]
    key = lax.broadcasted_iota(jnp.int32, (tq, tq), 0)
    qry = lax.broadcasted_iota(jnp.int32, (tq, tq), 1)
    bp_ref[0, 0] = jnp.where(key <= qry, lookup(jnp.maximum(qry - key, 0)) * LOG2E, NEG_INF)
    bp_ref[0, 1] = lookup(qry - key + tq) * LOG2E

    rows, cols = bs_ref.shape[1], bs_ref.shape[2]
    r = lax.broadcasted_iota(jnp.int32, (rows, cols), 0)
    c = lax.broadcasted_iota(jnp.int32, (rows, cols), 1)
    t = r % t_dec
    tok = c // A_HEADS
    valid = (c % A_HEADS) == h
    far = jnp.full((rows, cols), tab_ref[h, REL_BUCKETS - 1], F32)
    bs_ref[0] = jnp.where(valid, far, NEG_INF)
    bs_ref[1] = jnp.where(valid, lookup(PAGE_SIZE + t - tok), NEG_INF)
    new_ok = valid & (tok <= t) & (tok < t_dec)
    bs_ref[2] = jnp.where(new_ok, lookup(jnp.maximum(t - tok, 0)), NEG_INF)


def _bias_tiles(table, t_dec):
    rows = 2 * t_dec
    return pl.pallas_call(
        functools.partial(_bias_kernel, t_dec=t_dec),
        grid=(A_HEADS,),
        in_specs=[pl.BlockSpec(memory_space=pltpu.SMEM)],
        out_specs=[pl.BlockSpec((1, 2, ATT_TQ, ATT_TQ), lambda h: (h, 0, 0, 0)),
                   pl.BlockSpec((3, rows, PAGE_SIZE * A_HEADS), lambda h: (0, h, 0))],
        out_shape=[jax.ShapeDtypeStruct((A_HEADS, 2, ATT_TQ, ATT_TQ), F32),
                   jax.ShapeDtypeStruct((3, A_HEADS * rows, PAGE_SIZE * A_HEADS), F32)],
        compiler_params=_cparams(1),
        name="bias_tiles",
    )(table.T)


def _lam(lamv_ref, lam_init):
    s1 = jnp.sum(lamv_ref[0:1, :] * lamv_ref[1:2, :], axis=1, keepdims=True)
    s2 = jnp.sum(lamv_ref[2:3, :] * lamv_ref[3:4, :], axis=1, keepdims=True)
    return jnp.exp(s1) - jnp.exp(s2) + lam_init


def _subln(o, g_ref, lam_init):
    y = o * lax.rsqrt(jnp.mean(o * o, axis=-1, keepdims=True) + EPS)
    return y * g_ref[...] * (1.0 - lam_init)


def _attn_prompt_kernel(tab_ref, q_ref, k_ref, v_ref, bias_ref, lamv_ref, subg_ref, o_ref,
                        kb, vt, m_s, acc_s, *, lam_init):
    hg = pl.program_id(1)
    qi = pl.program_id(2)
    tq = q_ref.shape[1]

    n_hd = kb.shape[0]
    d = A_V_DIM
    chains = [(hh, m) for hh in range(n_hd) for m in range(2)]

    @pl.when(qi == 0)
    def _():
        for hh in range(n_hd):
            hcols = slice(hh * 128, (hh + 1) * 128)
            kb[hh] = k_ref[0, :, hcols].astype(BF16)
            for jj in range(vt.shape[1]):
                vt[hh, jj, :d] = jnp.transpose(v_ref[0, jj * tq:(jj + 1) * tq, hcols]).astype(BF16)
                vt[hh, jj, d:] = jnp.ones((vt.shape[2] - d, tq), BF16)

    sub = lax.broadcasted_iota(jnp.int32, (128, tq), 0)
    qm = {}
    for hh in range(n_hd):
        qt = jnp.transpose(q_ref[0, :, hh * 128:(hh + 1) * 128]) * LOG2E
        qm[hh, 0] = jnp.where(sub < A_QK_DIM, qt, 0.0).astype(BF16)
        qm[hh, 1] = jnp.where(sub >= A_QK_DIM, qt, 0.0).astype(BF16)
    m_s[...] = jnp.full(m_s.shape, NEG_INF, F32)
    acc_s[...] = jnp.zeros(acc_s.shape, F32)

    def step(j, bias):
        start = pl.multiple_of(j * tq, tq)
        m_prev = [m_s[c] for c in range(len(chains))]
        ss = [_dot(kb[hh, pl.ds(start, tq), :], qm[hh, m]) + bias[hh] for hh, m in chains]
        m_new = [jnp.maximum(mp, jnp.max(s, axis=0, keepdims=True)) for mp, s in zip(m_prev, ss)]
        ps = [jnp.exp2(s - mn).astype(BF16) for s, mn in zip(ss, m_new)]
        for c, (hh, m) in enumerate(chains):
            acc_s[c] = jnp.exp2(m_prev[c] - m_new[c]) * acc_s[c] + _dot(vt[hh, j], ps[c])
            m_s[c] = m_new[c]

    far = [tab_ref[hg * n_hd + hh, REL_BUCKETS - 1] * LOG2E for hh in range(n_hd)]

    def far_body(j, carry):
        step(j, far)
        return carry

    lax.fori_loop(0, jnp.maximum(qi - 1, 0), far_body, 0)

    @pl.when(qi >= 1)
    def _():
        step(qi - 1, [bias_ref[hh, 1] for hh in range(n_hd)])

    step(qi, [bias_ref[hh, 0] for hh in range(n_hd)])

    lam = _lam(lamv_ref, lam_init)
    for hh in range(n_hd):
        a0, a1 = acc_s[2 * hh], acc_s[2 * hh + 1]
        o = a0[:d] * (1.0 / a0[d:d + 1]) - lam * (a1[:d] * (1.0 / a1[d:d + 1]))
        y = o * lax.rsqrt(jnp.mean(o * o, axis=0, keepdims=True) + EPS)
        y = y * subg_ref[...] * (1.0 - lam_init)
        o_ref[0, :, hh * 128:(hh + 1) * 128] = jnp.transpose(y).astype(o_ref.dtype)


def _attn_prompt(zr3, bias_p, tab_t, lamv, subg, lam_init):
    b, s, _ = zr3.shape
    tq = ATT_TQ
    n_hd = ATT_HEADS_PER_STEP
    wid = 128 * n_hd
    kern = functools.partial(_attn_prompt_kernel, lam_init=lam_init)
    return pl.pallas_call(
        kern,
        grid=(b, A_HEADS // n_hd, s // tq),
        in_specs=[pl.BlockSpec(memory_space=pltpu.SMEM),
                  pl.BlockSpec((1, tq, wid), lambda bi, hg, qi: (bi, qi, COL_AQ // wid + hg)),
                  pl.BlockSpec((1, s, wid), lambda bi, hg, qi: (bi, 0, COL_AK // wid + hg)),
                  pl.BlockSpec((1, s, wid), lambda bi, hg, qi: (bi, 0, COL_AV // wid + hg)),
                  pl.BlockSpec((n_hd, 2, tq, tq), lambda bi, hg, qi: (hg, 0, 0, 0)),
                  pl.BlockSpec((4, A_QK_DIM), lambda bi, hg, qi: (0, 0)),
                  pl.BlockSpec((A_V_DIM, 1), lambda bi, hg, qi: (0, 0))],
        out_specs=pl.BlockSpec((1, tq, wid), lambda bi, hg, qi: (bi, qi, hg)),
        out_shape=jax.ShapeDtypeStruct((b, s, A_WIDTH), BF16),
        scratch_shapes=[pltpu.VMEM((n_hd, s, 128), BF16),
                        pltpu.VMEM((n_hd, s // tq, A_V_DIM + ONES_ROWS, tq), BF16),
                        pltpu.VMEM((2 * n_hd, 1, tq), F32),
                        pltpu.VMEM((2 * n_hd, A_V_DIM + ONES_ROWS, tq), F32)],
        compiler_params=_cparams(3),
        name="attn_prompt",
    )(tab_t, zr3, zr3, zr3, bias_p, lamv, subg.reshape(A_V_DIM, 1))


def _attn_sample_kernel(pt_ref, q_ref, kn_ref, vn_ref, bias_ref, lamv_ref, subg_ref, *rest,
                        lam_init, n_pg, t_dec):
    k_refs = rest[:n_pg]
    v_refs = rest[n_pg:2 * n_pg]
    o_ref = rest[2 * n_pg]
    q_s, m_s, l_s, acc_s = rest[2 * n_pg + 1:]
    j = pl.program_id(1)
    last = pl.num_programs(1) - 1
    rows = 2 * t_dec

    @pl.when(j == 0)
    def _():
        q = q_ref[0]
        lane = lax.broadcasted_iota(jnp.int32, (t_dec, 128), 1)
        for h in range(A_HEADS):
            qh = q[:, h * 128:(h + 1) * 128]
            q_s[h * rows:h * rows + t_dec, :] = jnp.where(lane < A_QK_DIM, qh, 0.0)
            q_s[h * rows + t_dec:(h + 1) * rows, :] = jnp.where(lane >= A_QK_DIM, qh, 0.0)
        m_s[...] = jnp.full(m_s.shape, NEG_INF, F32)
        l_s[...] = jnp.zeros(l_s.shape, F32)
        acc_s[...] = jnp.zeros(acc_s.shape, F32)

    qb = q_s[...].astype(BF16)

    def update(ks, vs, biases):
        ss = [_dot_nt(qb, kb) + bias for kb, bias in zip(ks, biases)]
        smax = functools.reduce(jnp.maximum, ss)
        m_prev = m_s[...]
        m_new = jnp.maximum(m_prev, jnp.max(smax, axis=1, keepdims=True))
        ps = [jnp.exp(s - m_new) for s in ss]
        alpha = jnp.exp(m_prev - m_new)
        l_s[...] = alpha * l_s[...] + jnp.sum(functools.reduce(jnp.add, ps), axis=1, keepdims=True)
        pv = functools.reduce(jnp.add, [_dot(p.astype(BF16), vb) for p, vb in zip(ps, vs)])
        acc_s[...] = alpha * acc_s[...] + pv
        m_s[...] = m_new

    biases = [bias_ref[0]] * (n_pg - 1) + [jnp.where(j == last, bias_ref[1], bias_ref[0])]
    update([r[0, 0].astype(BF16) for r in k_refs], [r[0, 0].astype(BF16) for r in v_refs], biases)

    @pl.when(j == last)
    def _():
        nk = kn_ref.shape[1]
        update([kn_ref[0].astype(BF16)], [vn_ref[0].astype(BF16)], [bias_ref[2][:, :nk]])
        lam = _lam(lamv_ref, lam_init)
        on = acc_s[...] * (1.0 / l_s[...])
        for h in range(A_HEADS):
            o = on[h * rows:h * rows + t_dec] - lam * on[h * rows + t_dec:(h + 1) * rows]
            o_ref[0, :, h * 128:(h + 1) * 128] = _subln(o, subg_ref, lam_init).astype(o_ref.dtype)


def _attn_sample(zrs3, kn2d, vn2d, cache_k2d, cache_v2d, page_table, bias_s, lamv, subg,
                 layer, lam_init):
    bd, t_dec, _ = zrs3.shape
    n_pages = page_table.shape[1]
    n_pg = min(SAMPLE_PAGES, n_pages)
    rows = 2 * t_dec * A_HEADS
    pcols = PAGE_SIZE * A_HEADS

    def page_spec(p):
        return pl.BlockSpec((1, 1, pcols, 128),
                            lambda b, j, pt: (layer, pt[b, j * n_pg + p], 0, 0))

    kern = functools.partial(_attn_sample_kernel, lam_init=lam_init, n_pg=n_pg, t_dec=t_dec)
    grid_spec = pltpu.PrefetchScalarGridSpec(
        num_scalar_prefetch=1,
        grid=(bd, n_pages // n_pg),
        in_specs=[pl.BlockSpec((1, t_dec, A_WIDTH), lambda b, j, pt: (b, 0, 0)),
                  pl.BlockSpec((1,) + kn2d.shape[1:], lambda b, j, pt: (b, 0, 0)),
                  pl.BlockSpec((1,) + vn2d.shape[1:], lambda b, j, pt: (b, 0, 0)),
                  pl.BlockSpec((3, rows, pcols), lambda b, j, pt: (0, 0, 0)),
                  pl.BlockSpec((4, A_QK_DIM), lambda b, j, pt: (0, 0)),
                  pl.BlockSpec((1, A_V_DIM), lambda b, j, pt: (0, 0))]
                 + [page_spec(p) for p in range(n_pg)] * 2,
        out_specs=pl.BlockSpec((1, t_dec, A_WIDTH), lambda b, j, pt: (b, 0, 0)),
        scratch_shapes=[pltpu.VMEM((rows, 128), F32), pltpu.VMEM((rows, 1), F32),
                        pltpu.VMEM((rows, 1), F32), pltpu.VMEM((rows, 128), F32)],
    )
    return pl.pallas_call(
        kern,
        grid_spec=grid_spec,
        out_shape=jax.ShapeDtypeStruct((bd, t_dec, A_WIDTH), BF16),
        compiler_params=_cparams(2),
        name="attn_sample",
    )(page_table, zrs3, kn2d, vn2d, bias_s, lamv, subg,
      *([cache_k2d] * n_pg), *([cache_v2d] * n_pg))


def _gelu(x):
    return 0.5 * x * (1.0 + jnp.tanh(math.sqrt(2.0 / math.pi) * (x + 0.044715 * (x * x * x))))


def _chunk_mlp_kernel(u_ref, v_ref, w_ref, bias_ref, g_ref, b_ref, y_ref, vb_ref):
    n_chunks = u_ref.shape[0] // B_CHUNK
    lane_grp = lax.broadcasted_iota(jnp.int32, (B_CHUNK, B_WIDTH), 1) // (B_WIDTH // B_GROUPS)
    for c in range(n_chunks):
        sl = slice(c * B_CHUNK, (c + 1) * B_CHUNK)
        gv = _gelu(v_ref[sl, :])
        xc = gv - jnp.mean(gv, axis=-1, keepdims=True)
        vb = xc * lax.rsqrt(jnp.mean(xc * xc, axis=-1, keepdims=True) + EPS) * g_ref[...] + b_ref[...]
        vb_ref[sl, :] = vb
        vbb = vb.astype(BF16)
        mixed = bias_ref[...]
        for g in range(B_GROUPS):
            mixed = mixed + jnp.where(lane_grp == g, _dot(w_ref[g], vbb), 0.0)
        y_ref[sl, :] = (_gelu(u_ref[sl, :]) * mixed).astype(y_ref.dtype)


def _chunk_mlp(zr, w_eff, bias_eff, ln_g, ln_b):
    m = zr.shape[0]
    tm = min(m, 512)
    return pl.pallas_call(
        _chunk_mlp_kernel,
        grid=(m // tm,),
        in_specs=[pl.BlockSpec((tm, B_WIDTH), lambda i: (i, COL_BU // B_WIDTH)),
                  pl.BlockSpec((tm, B_WIDTH), lambda i: (i, COL_BV // B_WIDTH)),
                  pl.BlockSpec((B_GROUPS, B_CHUNK, B_CHUNK), lambda i: (0, 0, 0)),
                  pl.BlockSpec((B_CHUNK, B_WIDTH), lambda i: (0, 0)),
                  pl.BlockSpec((1, B_WIDTH), lambda i: (0, 0)),
                  pl.BlockSpec((1, B_WIDTH), lambda i: (0, 0))],
        out_specs=[pl.BlockSpec((tm, B_WIDTH), lambda i: (i, 0)),
                   pl.BlockSpec((tm, B_WIDTH), lambda i: (i, 0))],
        out_shape=[jax.ShapeDtypeStruct((m, B_WIDTH), BF16),
                   jax.ShapeDtypeStruct((m, B_WIDTH), F32)],
        compiler_params=_cparams(1),
        name="chunk_mlp",
    )(zr, zr, w_eff, bias_eff, ln_g, ln_b)


def _pool_kernel(x_ref, w_ref, scale_ref, y_ref, *, prefix, pos0):
    x = x_ref[0]
    rows = x.shape[0]
    row = lax.broadcasted_iota(jnp.int32, x.shape, 0)
    grp = lax.broadcasted_iota(jnp.int32, x.shape, 1) // C_GROUP_DIM

    def shifted(a, k):
        return jnp.where(row >= k, pltpu.roll(a, k, 0), 0.0)

    sums = []
    acc = x
    for k in (1, 2, 4, 8):
        acc = acc + shifted(acc, k)
        sums.append(acc)
    total = sums[3]
    win = jnp.full(x.shape, POOL_WINDOWS[3], jnp.int32)
    for g in range(3):
        total = jnp.where(grp == g, sums[g], total)
        win = jnp.where(grp == g, POOL_WINDOWS[g], win)
    pos = pos0 + row - prefix
    cnt = jnp.clip(pos + 1, 1, win).astype(F32)
    d = total / cnt - x
    y = _dot(d.astype(BF16), w_ref[...]) * scale_ref[...]
    y_ref[0] = y.astype(y_ref.dtype)
    del rows


def _pool_mix(xx, col_block, w_bd, scale, prefix, pos0):
    b, rows, _ = xx.shape
    kern = functools.partial(_pool_kernel, prefix=prefix, pos0=pos0)
    return pl.pallas_call(
        kern,
        grid=(b,),
        in_specs=[pl.BlockSpec((1, rows, C_WIDTH), lambda i: (i, 0, col_block)),
                  pl.BlockSpec((C_WIDTH, C_WIDTH), lambda i: (0, 0)),
                  pl.BlockSpec((1, C_WIDTH), lambda i: (0, 0))],
        out_specs=pl.BlockSpec((1, rows, C_WIDTH), lambda i: (i, 0, 0)),
        out_shape=jax.ShapeDtypeStruct((b, rows, C_WIDTH), BF16),
        compiler_params=_cparams(1),
        name="pool_mix",
    )(xx, w_bd, scale)


def _log_sigmoid(x):
    return jnp.minimum(x, 0.0) - jnp.log1p(jnp.exp(-jnp.abs(x)))


def _split3(x):
    def top8(a):
        bits = lax.bitcast_convert_type(a, jnp.int32) & jnp.int32(-65536)
        return lax.bitcast_convert_type(bits, F32)

    p1 = top8(x)
    r1 = x - p1
    p2 = top8(r1)
    return p1, p2, r1 - p2


def _mlstm_kernel(q_ref, k_ref, v_ref, o_ref, gi_ref, gf_ref, gb_ref, ng_ref, sel_ref, bd_ref,
                  c0_ref, n0_ref, m0_ref, y_ref, c_out, n_out, m_out, c_s, n_s, m_s, *, t_valid):
    ci = pl.program_id(1)
    nb, chunk = q_ref.shape[0], q_ref.shape[1]
    hd = D_HEAD_DIM
    eye_h = (lax.broadcasted_iota(jnp.int32, (hd, hd), 0)
             == lax.broadcasted_iota(jnp.int32, (hd, hd), 1))

    @pl.when(ci == 0)
    def _():
        c_s[...] = jnp.zeros(c_s.shape, F32)
        n_s[...] = jnp.zeros(n_s.shape, F32)
        m_s[...] = m0_ref[...]
        for bi in range(nb):
            for h in range(D_HEADS):
                hs = slice(h * hd, (h + 1) * hd)
                c_s[bi, hs, hs] = c0_ref[bi, h]
                n_col = jnp.sum(jnp.where(eye_h, n0_ref[bi, h], 0.0), axis=1, keepdims=True)
                n_s[bi, hs, hs] = jnp.broadcast_to(n_col, (hd, hd))

    row = lax.broadcasted_iota(jnp.int32, (chunk, 128), 0)
    lane = lax.broadcasted_iota(jnp.int32, (chunk, 128), 1)
    head_lane = lane < D_HEADS
    grp = lax.broadcasted_iota(jnp.int32, (chunk, D_WIDTH), 1) // hd
    rr = lax.broadcasted_iota(jnp.int32, (chunk, chunk), 0)
    cc = lax.broadcasted_iota(jnp.int32, (chunk, chunk), 1)
    allowed = (cc <= rr) & (cc < t_valid)
    ones_t = jnp.ones((chunk, 128), F32)
    ones_w = jnp.ones((chunk, D_WIDTH), BF16)
    same_head = (lax.broadcasted_iota(jnp.int32, (D_WIDTH, D_WIDTH), 0) // hd
                 == lax.broadcasted_iota(jnp.int32, (D_WIDTH, D_WIDTH), 1) // hd)
    pick = [lane == h for h in range(D_HEADS)]
    in_head = [grp == h for h in range(D_HEADS)]
    pick_one = [jnp.where(pk, 1.0, 0.0) for pk in pick]
    head_one = [jnp.where(ih, 1.0, 0.0).astype(BF16) for ih in in_head]

    for bi in range(nb):
        gi = gi_ref[bi] + gb_ref[0:1, :]
        lf = _log_sigmoid(gf_ref[bi] + gb_ref[1:2, :])
        if t_valid < chunk:
            gi = jnp.where(row < t_valid, gi, NEG_INF)
            lf = jnp.where(row < t_valid, lf, 0.0)
        b = lf
        k = 1
        while k < chunk:
            b = b + jnp.where(row >= k, pltpu.roll(b, k, 0), 0.0)
            k *= 2
        u = gi - b
        cm = u
        k = 1
        while k < chunk:
            cm = jnp.maximum(cm, jnp.where(row >= k, pltpu.roll(cm, k, 0), NEG_INF))
            k *= 2
        m_prev = m_s[bi]
        big_m = jnp.maximum(m_prev, cm)
        m_last = big_m[chunk - 1:chunk, :]
        winter = jnp.exp(m_prev - big_m)
        emt = jnp.exp(jnp.minimum(-(b + big_m), EMT_CAP))
        ws = jnp.exp(u - m_last)
        m_s[bi] = b[chunk - 1:chunk, :] + m_last

        def per_head_lanes(z):
            pieces = _split3(jnp.where(head_lane, z, 0.0))
            return _dot(jnp.concatenate(pieces, axis=1).astype(BF16), sel_ref[...])

        winter_r, emt_r, ws_r = per_head_lanes(winter), per_head_lanes(emt), per_head_lanes(ws)
        decay_r = winter_r[chunk - 1:chunk, :]

        q = q_ref[bi]
        qb = q.astype(BF16)
        kf = k_ref[bi] * (hd ** -0.5)
        kb = kf.astype(BF16)
        vf = v_ref[bi]
        vb = vf.astype(BF16)
        u_fin = jnp.where(head_lane & (row < t_valid), u, 0.0)
        y_side = jnp.concatenate(_split3(u_fin) + (ones_t, ones_t, ones_t), axis=1).astype(BF16)
        m_neg = _split3(jnp.where(head_lane, -big_m, 0.0))
        acc = jnp.zeros((chunk, 2 * D_WIDTH), F32)
        for h in range(D_HEADS):
            x_side = jnp.concatenate(
                [pick_one[h]] * 3 + [jnp.where(pick[h], piece, 0.0) for piece in m_neg],
                axis=1).astype(BF16)
            expo = jnp.where(allowed, _dot_nt(x_side, y_side), NEG_INF)
            qk = _dot_nt(jnp.where(in_head[h], q, 0.0).astype(BF16), kb) * jnp.exp(expo)
            rhs = jnp.concatenate([jnp.where(in_head[h], vf, 0.0).astype(BF16), head_one[h]], axis=1)
            acc = acc + _dot(qk.astype(BF16), rhs)
        state = jnp.concatenate([c_s[bi].astype(BF16), n_s[bi].astype(BF16)], axis=1)
        inter = _dot(qb, state)
        num = winter_r * inter[:, :D_WIDTH] + acc[:, :D_WIDTH]
        den = winter_r * inter[:, D_WIDTH:] + acc[:, D_WIDTH:]
        hh = num / jnp.maximum(jnp.abs(den), emt_r)
        h2 = hh * hh
        hi = h2.astype(BF16)
        lo = (h2 - hi.astype(F32)).astype(BF16)
        ssq = _dot(hi, bd_ref[...]) + _dot(lo, bd_ref[...])
        y = hh * lax.rsqrt(ssq * (1.0 / hd) + EPS) * ng_ref[...]
        y_ref[bi] = (y * jax.nn.sigmoid(o_ref[bi])).astype(y_ref.dtype)
        kw = (ws_r * kf).astype(BF16)
        upd = _dot_tn(kw, jnp.concatenate([vb, ones_w], axis=1))
        c_s[bi] = decay_r * c_s[bi] + jnp.where(same_head, upd[:, :D_WIDTH], 0.0)
        n_s[bi] = decay_r * n_s[bi] + jnp.where(same_head, upd[:, D_WIDTH:], 0.0)

    @pl.when(ci == pl.num_programs(1) - 1)
    def _():
        m_out[...] = m_s[...]
        for bi in range(nb):
            for h in range(D_HEADS):
                hs = slice(h * hd, (h + 1) * hd)
                c_out[bi, h] = c_s[bi, hs, hs]
                n_out[bi, h] = jnp.sum(jnp.where(eye_h, n_s[bi, hs, hs], 0.0), axis=0, keepdims=True)


def _mlstm(src, col0, gate_block, chunk, t_valid, gate_bias, norm_g, c0, n0, m0):
    b, t, _ = src.shape
    nb = math.gcd(b, MLSTM_BATCH)
    hd = D_HEAD_DIM
    kern = functools.partial(_mlstm_kernel, t_valid=t_valid)
    head_of_lane = jnp.arange(D_WIDTH) // hd
    sel = (jnp.arange(128)[:, None] == head_of_lane[None, :]).astype(BF16)
    sel3 = jnp.concatenate([sel, sel, sel], axis=0)
    same_head = (head_of_lane[:, None] == head_of_lane[None, :]).astype(BF16)
    m0p = jnp.pad(m0, ((0, 0), (0, 0), (0, 128 - D_HEADS)))

    def col(cb, width=D_WIDTH):
        return pl.BlockSpec((nb, chunk, width), lambda bi, ci: (bi, ci, cb))

    def const(shape):
        return pl.BlockSpec(shape, lambda bi, ci: (0,) * len(shape))

    def per_seq(shape):
        return pl.BlockSpec((nb,) + shape, lambda bi, ci: (bi,) + (0,) * len(shape))

    y, c_new, n_new, m_new = pl.pallas_call(
        kern,
        grid=(b // nb, t // chunk),
        in_specs=[col(col0), col(col0 + 1), col(col0 + 2), col(col0 + 3),
                  col(gate_block, 128), col(gate_block + 1, 128),
                  const((2, 128)), const((1, D_WIDTH)), const((3 * 128, D_WIDTH)),
                  const((D_WIDTH, D_WIDTH)),
                  per_seq((D_HEADS, hd, hd)), per_seq((D_HEADS, 1, hd)), per_seq((1, 128))],
        out_specs=[col(0), per_seq((D_HEADS, hd, hd)), per_seq((D_HEADS, 1, hd)),
                   per_seq((1, 128))],
        out_shape=[jax.ShapeDtypeStruct((b, t, D_WIDTH), BF16),
                   jax.ShapeDtypeStruct((b, D_HEADS, hd, hd), F32),
                   jax.ShapeDtypeStruct((b, D_HEADS, 1, hd), F32),
                   jax.ShapeDtypeStruct((b, 1, 128), F32)],
        scratch_shapes=[pltpu.VMEM((nb, D_WIDTH, D_WIDTH), F32),
                        pltpu.VMEM((nb, D_WIDTH, D_WIDTH), F32),
                        pltpu.VMEM((nb, 1, 128), F32)],
        compiler_params=_cparams(2),
        name="mlstm",
    )(src, src, src, src, src, src, gate_bias, jnp.tile(norm_g, (1, D_HEADS)), sel3, same_head,
      c0, n0, m0p)
    return y, c_new, n_new, m_new[:, :, :D_HEADS]


def _merge_kernel(x_ref, g_ref, ya_ref, yb_ref, yc_ref, yd_ref, wa_ref, wb_ref, wc_ref, wd_ref,
                  wo_ref, o_ref):
    def gate(i):
        return g_ref[:, i * D_MODEL:(i + 1) * D_MODEL].astype(F32)

    merged = gate(0) * _dot(ya_ref[...], wa_ref[...])
    merged = merged + gate(1) * _dot(yb_ref[...], wb_ref[...])
    merged = merged + gate(2) * _dot(yc_ref[...], wc_ref[...])
    merged = merged + gate(3) * _dot(yd_ref[...], wd_ref[...])
    o_ref[...] = x_ref[...] + _dot(merged.astype(BF16), wo_ref[...])


def _merge(x, gates, ya, yb, yc, yd, wa, wb, wc, wd, wo):
    m = x.shape[0]
    tm = min(m, 512)

    def rows(width):
        return pl.BlockSpec((tm, width), lambda i: (i, 0))

    def full(arr):
        return pl.BlockSpec(arr.shape, lambda i: (0, 0))

    return pl.pallas_call(
        _merge_kernel,
        grid=(m // tm,),
        in_specs=[rows(D_MODEL), rows(GZ_WIDTH), rows(A_WIDTH), rows(B_WIDTH), rows(C_WIDTH),
                  rows(D_WIDTH), full(wa), full(wb), full(wc), full(wd), full(wo)],
        out_specs=rows(D_MODEL),
        out_shape=jax.ShapeDtypeStruct((m, D_MODEL), F32),
        compiler_params=_cparams(1),
        name="merge",
    )(x, gates, ya, yb, yc, yd, wa, wb, wc, wd, wo)


def _ffn_kernel(x_ref, g_ref, w1_ref, w2_ref, o_ref, h_s, acc_s):
    j = pl.program_id(1)

    @pl.when(j == 0)
    def _():
        x = x_ref[...]
        y = x * lax.rsqrt(jnp.mean(x * x, axis=-1, keepdims=True) + EPS)
        h_s[...] = (y * g_ref[...]).astype(BF16)
        acc_s[...] = jnp.zeros(acc_s.shape, F32)

    a = jnp.maximum(_dot(h_s[...], w1_ref[...]), 0.0)
    acc_s[...] += _dot((a * a).astype(BF16), w2_ref[...])

    @pl.when(j == pl.num_programs(1) - 1)
    def _():
        o_ref[...] = x_ref[...] + acc_s[...]


def _ffn(x, g, w1, w2):
    m = x.shape[0]
    tm, tf = min(m, 1024), 1024
    return pl.pallas_call(
        _ffn_kernel,
        grid=(m // tm, D_FF // tf),
        in_specs=[pl.BlockSpec((tm, D_MODEL), lambda i, j: (i, 0)),
                  pl.BlockSpec((1, D_MODEL), lambda i, j: (0, 0)),
                  pl.BlockSpec((D_MODEL, tf), lambda i, j: (0, j)),
                  pl.BlockSpec((tf, D_MODEL), lambda i, j: (j, 0))],
        out_specs=pl.BlockSpec((tm, D_MODEL), lambda i, j: (i, 0)),
        out_shape=jax.ShapeDtypeStruct((m, D_MODEL), F32),
        scratch_shapes=[pltpu.VMEM((tm, D_MODEL), BF16), pltpu.VMEM((tm, D_MODEL), F32)],
        compiler_params=_cparams(2),
        name="ffn",
    )(x, g.reshape(1, D_MODEL), w1, w2)


def _layer_weights(p, l):
    w_in = p["w_in"][l]
    gate_pad = jnp.zeros((D_MODEL, 128 - D_HEADS), F32)
    wr = jnp.concatenate(
        [w_in[:, :3072], w_in[:, 3080:3336], w_in[:, 3072:3076], gate_pad,
         w_in[:, 3076:3080], gate_pad], axis=1).astype(BF16)
    wg = w_in[:, 3336:].astype(BF16)
    reps = A_WIDTH // A_QK_DIM
    gains = jnp.stack([jnp.tile(p["q_norm_g"][l], reps) * (A_QK_DIM ** -0.5),
                       jnp.tile(p["k_norm_g"][l], reps)]).reshape(2, 1, A_WIDTH)
    grp = jnp.arange(A_WIDTH) // A_QK_DIM
    ones_bd = (grp[:, None] == grp[None, :]).astype(BF16)
    lamv = jnp.stack([p["lam_q1"][l], p["lam_k1"][l], p["lam_q2"][l], p["lam_k2"][l]])
    gate_bias = jnp.pad(jnp.stack([p["d_i_bias"][l], p["d_f_bias"][l]]),
                        ((0, 0), (0, 128 - D_HEADS)))
    c_bd = jnp.zeros((C_WIDTH, C_WIDTH), F32)
    for g in range(4):
        sl = slice(g * C_GROUP_DIM, (g + 1) * C_GROUP_DIM)
        c_bd = c_bd.at[sl, sl].set(p["c_lin"][l][g])
    return dict(
        norm1_g=p["norm1_g"][l], norm2_g=p["norm2_g"][l], wr=wr, wg=wg, gains=gains,
        ones_bd=ones_bd, lamv=lamv, subg=p["subln_g"][l].reshape(1, A_V_DIM),
        b_ln_g=p["b_ln_g"][l].reshape(1, B_WIDTH), b_ln_b=p["b_ln_b"][l].reshape(1, B_WIDTH),
        b_ws=p["b_ws"][l], b_bias=p["b_bias"][l],
        c_bd=c_bd.astype(BF16), c_scale=p["c_scale"][l].reshape(1, C_WIDTH),
        gate_bias=gate_bias, d_norm_g=p["d_norm_g"][l].reshape(1, D_HEAD_DIM),
        w_pa=p["w_pa"][l].astype(BF16), w_pb=p["w_pb"][l].astype(BF16),
        w_pc=p["w_pc"][l].astype(BF16), w_pd=p["w_pd"][l].astype(BF16),
        w_out=p["w_out"][l].astype(BF16), w_ff1=p["w_ff1"][l].astype(BF16),
        w_ff2=p["w_ff2"][l].astype(BF16),
        lam_init=0.8 - 0.6 * math.exp(-0.3 * l),
    )


def _chunk_weights(w, t):
    length = min(t, B_CHUNK)
    ws = jnp.tril(w["b_ws"][:, :length, :length])
    bias = jnp.transpose(w["b_bias"][:, :length])
    reps = B_CHUNK // length
    if reps > 1:
        eye = jnp.eye(reps, dtype=F32)
        ws = jax.vmap(lambda a: jnp.kron(eye, a))(ws)
        bias = jnp.tile(bias, (reps, 1))
    return ws.astype(BF16), jnp.repeat(bias, B_WIDTH // B_GROUPS, axis=1)


def _finish_layer(x2, w, gates, zr, ya, yc, yd, t):
    w_eff, bias_eff = _chunk_weights(w, t)
    yb, vb = _chunk_mlp(zr, w_eff, bias_eff, w["b_ln_g"], w["b_ln_b"])
    x2 = _merge(x2, gates, ya, yb, yc, yd, w["w_pa"], w["w_pb"], w["w_pc"], w["w_pd"], w["w_out"])
    return _ffn(x2, w["norm2_g"], w["w_ff1"], w["w_ff2"]), vb


def _prompt_layer(x2, w, b, s, bias_p, tab_t):
    m = b * s
    gates, zr, k_rows, v_rows = _in_proj(x2, w["norm1_g"], w["wg"], w["wr"], w["gains"],
                                         w["ones_bd"])
    zr3 = zr.reshape(b, s, ZR_WIDTH)
    ya = _attn_prompt(zr3, bias_p, tab_t, w["lamv"], w["subg"], w["lam_init"]).reshape(m, A_WIDTH)
    yc = _pool_mix(zr3, COL_CX // C_WIDTH, w["c_bd"], w["c_scale"], 0, 0).reshape(m, C_WIDTH)
    hd = D_HEAD_DIM
    yd, c_new, n_new, m_new = _mlstm(
        zr3, COL_DQ // D_WIDTH, COL_DGI // 128, min(s, 128), min(s, 128), w["gate_bias"],
        w["d_norm_g"], jnp.zeros((b, D_HEADS, hd, hd), F32), jnp.zeros((b, D_HEADS, 1, hd), F32),
        jnp.zeros((b, 1, D_HEADS), F32))
    x2, _ = _finish_layer(x2, w, gates, zr, ya, yc, yd.reshape(m, D_WIDTH), s)
    outs = (k_rows.reshape(b, s, A_HEADS, 2 * A_QK_DIM), v_rows.reshape(b, s, A_HEADS, A_V_DIM),
            zr3[:, s - POOL_BUF:, COL_CX:COL_CX + C_WIDTH],
            c_new, n_new.reshape(b, D_HEADS, hd), m_new.reshape(b, D_HEADS))
    return x2, outs


def _sample_layer(x2, w, l, bd, t, cache_k2d, cache_v2d, page_table, bias_s, pool0, c0, n0, m0):
    m = bd * t
    past = page_table.shape[1] * PAGE_SIZE
    gates, zr, kn, vn = _in_proj(x2, w["norm1_g"], w["wg"], w["wr"], w["gains"], w["ones_bd"])
    zr3 = zr.reshape(bd, t, ZR_WIDTH)
    ya = _attn_sample(zr3, kn.reshape(bd, t * A_HEADS, 128), vn.reshape(bd, t * A_HEADS, 128),
                      cache_k2d, cache_v2d, page_table, bias_s, w["lamv"], w["subg"], l,
                      w["lam_init"]).reshape(m, A_WIDTH)
    cx = zr3[:, :, COL_CX:COL_CX + C_WIDTH]
    prefix = POOL_BUF + 1
    rows = -(-(prefix + t) // 8) * 8
    xx = jnp.concatenate([jnp.zeros((bd, 1, C_WIDTH), F32), pool0, cx,
                          jnp.zeros((bd, rows - prefix - t, C_WIDTH), F32)], axis=1)
    yc = _pool_mix(xx, 0, w["c_bd"], w["c_scale"], prefix, past)[:, prefix:prefix + t]
    yc = yc.reshape(m, C_WIDTH)
    chunk = -(-t // SAMPLE_MLSTM_CHUNK) * SAMPLE_MLSTM_CHUNK
    dsrc = jnp.pad(zr3[:, :, COL_DQ:COL_DGF + 128], ((0, 0), (0, chunk - t), (0, 0)))
    hd = D_HEAD_DIM
    yd, c_new, n_new, m_new = _mlstm(
        dsrc, 0, (COL_DGI - COL_DQ) // 128, chunk, t, w["gate_bias"], w["d_norm_g"],
        c0, n0.reshape(bd, D_HEADS, 1, hd), m0.reshape(bd, 1, D_HEADS))
    yd = yd[:, :t].reshape(m, D_WIDTH)
    x2, vb = _finish_layer(x2, w, gates, zr, ya, yc, yd, t)
    outs = (kn.reshape(bd, t, A_HEADS, 2 * A_QK_DIM), vn.reshape(bd, t, A_HEADS, A_V_DIM),
            vb.reshape(bd, t, B_WIDTH), jnp.concatenate([pool0, cx], axis=1)[:, -POOL_BUF:],
            c_new, n_new.reshape(bd, D_HEADS, hd), m_new.reshape(bd, D_HEADS))
    return x2, outs


def kernel(x_prompt, x_sample, cache_k, cache_v, page_table, state_pool, state_C, state_n, state_m, rel_bias, norm1_g, norm2_g, w_in, q_norm_g, k_norm_g, lam_q1, lam_k1, lam_q2, lam_k2, subln_g, b_ln_g, b_ln_b, b_ws, b_bias, c_lin, c_scale, d_i_bias, d_f_bias, d_norm_g, w_pa, w_pb, w_pc, w_pd, w_out, w_ff1, w_ff2):
    p = dict(norm1_g=norm1_g, norm2_g=norm2_g, w_in=w_in, q_norm_g=q_norm_g, k_norm_g=k_norm_g,
             lam_q1=lam_q1, lam_k1=lam_k1, lam_q2=lam_q2, lam_k2=lam_k2, subln_g=subln_g,
             b_ln_g=b_ln_g, b_ln_b=b_ln_b, b_ws=b_ws, b_bias=b_bias, c_lin=c_lin, c_scale=c_scale,
             d_i_bias=d_i_bias, d_f_bias=d_f_bias, d_norm_g=d_norm_g, w_pa=w_pa, w_pb=w_pb,
             w_pc=w_pc, w_pd=w_pd, w_out=w_out, w_ff1=w_ff1, w_ff2=w_ff2)
    depth = w_in.shape[0]
    bp, sp, _ = x_prompt.shape
    bd, td, _ = x_sample.shape
    n_phys = cache_k.shape[1]
    cache_k2d = cache_k.reshape(depth, n_phys, PAGE_SIZE * A_HEADS, 128)
    cache_v2d = cache_v.reshape(depth, n_phys, PAGE_SIZE * A_HEADS, 128)
    bias_p, bias_s = _bias_tiles(rel_bias, td)
    tab_t = rel_bias.T

    xp = x_prompt.reshape(bp * sp, D_MODEL)
    xs = x_sample.reshape(bd * td, D_MODEL)
    prompt_outs, sample_outs = [], []
    for l in range(depth):
        w = _layer_weights(p, l)
        xp, po = _prompt_layer(xp, w, bp, sp, bias_p, tab_t)
        xs, so = _sample_layer(xs, w, l, bd, td, cache_k2d, cache_v2d, page_table, bias_s,
                               state_pool[l], state_C[l], state_n[l], state_m[l])
        prompt_outs.append(po)
        sample_outs.append(so)

    def stack(outs, i):
        return jnp.stack([o[i] for o in outs])

    return (xp.reshape(bp, sp, D_MODEL), xs.reshape(bd, td, D_MODEL),
            stack(prompt_outs, 0), stack(prompt_outs, 1), stack(sample_outs, 0), stack(sample_outs, 1),
            stack(sample_outs, 2), stack(prompt_outs, 2), stack(sample_outs, 3),
            stack(prompt_outs, 3), stack(prompt_outs, 4), stack(prompt_outs, 5),
            stack(sample_outs, 4), stack(sample_outs, 5), stack(sample_outs, 6))
```

```python
import functools
import math

import jax
import jax.numpy as jnp
from jax import lax
from jax.experimental import pallas as pl
from jax.experimental.pallas import tpu as pltpu

F32 = jnp.float32
BF16 = jnp.bfloat16
NEG_INF = float("-inf")

D_MODEL = 1024
A_HEADS = 4
A_QK_DIM = 64
A_V_DIM = 128
A_WIDTH = 512
REL_BUCKETS = 32
REL_MAX_DIST = 128
PAGE_SIZE = 128
B_GROUPS = 4
B_WIDTH = 256
B_CHUNK = 128
C_WIDTH = 256
C_GROUP_DIM = 64
POOL_WINDOWS = (2, 4, 8, 16)
POOL_BUF = 15
D_HEADS = 4
D_WIDTH = 256
D_HEAD_DIM = 64
D_FF = 4096
N_BRANCH = 4
EPS = 1e-6

ZR_WIDTH = 3584
COL_AQ, COL_AK, COL_AV = 0, 512, 1024
COL_BU, COL_BV, COL_CX = 1536, 1792, 2048
COL_DQ, COL_DK, COL_DV, COL_DO = 2304, 2560, 2816, 3072
COL_DGI, COL_DGF = 3328, 3456
GZ_WIDTH = N_BRANCH * D_MODEL
EMT_CAP = 80.0
LOG2E = math.log2(math.e)
ONES_ROWS = 16

IN_PROJ_TM = 512
IN_PROJ_TN = 512
CHUNK_MLP_TM = 1024
ATT_TQ = 512
ATT_HEADS_PER_STEP = 2
SAMPLE_PAGES = 32
MLSTM_BATCH = 4
MLSTM_CHUNK = 256
SAMPLE_MLSTM_CHUNK = 32
VMEM_LIMIT = 56 * 1024 * 1024


def _cparams(n_axes):
    return pltpu.CompilerParams(dimension_semantics=("arbitrary",) * n_axes,
                                vmem_limit_bytes=VMEM_LIMIT)


def _dot(a, b):
    return jnp.dot(a, b, preferred_element_type=F32)


def _dot_nt(a, b):
    return lax.dot_general(a, b, (((1,), (1,)), ((), ())), preferred_element_type=F32)


def _dot_tn(a, b):
    return lax.dot_general(a, b, (((0,), (0,)), ((), ())), preferred_element_type=F32)


def _in_proj_kernel(x_ref, g_ref, wg_ref, wr_ref, gain_ref, ones_ref,
                    gates_ref, zr_ref, kout_ref, vout_ref):
    tm = x_ref.shape[0]
    tn = IN_PROJ_TN
    x = x_ref[...]
    h = (x * lax.rsqrt(jnp.mean(x * x, axis=-1, keepdims=True) + EPS) * g_ref[...]).astype(BF16)

    def head_rows(dst_ref, val):
        for hh in range(A_HEADS):
            dst_ref[pl.ds(hh, tm, stride=A_HEADS), :] = val[:, hh * 128:(hh + 1) * 128]

    for c in range(ZR_WIDTH // tn):
        cols = slice(c * tn, (c + 1) * tn)
        z = _dot(h, wr_ref[:, cols])
        if c * tn in (COL_AQ, COL_AK):
            z2 = z * z
            hi = z2.astype(BF16)
            lo = (z2 - hi.astype(F32)).astype(BF16)
            ssq = _dot(hi, ones_ref[...]) + _dot(lo, ones_ref[...])
            z = z * lax.rsqrt(ssq * (1.0 / A_QK_DIM) + EPS) * gain_ref[c]
        zr_ref[:, cols] = z
        if c * tn == COL_AK:
            head_rows(kout_ref, z)
        if c * tn == COL_AV:
            head_rows(vout_ref, z)

    for c in range(GZ_WIDTH // tn):
        cols = slice(c * tn, (c + 1) * tn)
        gates_ref[:, cols] = jax.nn.sigmoid(_dot(h, wg_ref[:, cols])).astype(gates_ref.dtype)


def _in_proj(x, g, wg, wr, gains, ones_bd):
    m = x.shape[0]
    tm = min(m, IN_PROJ_TM)

    def rows(width):
        return pl.BlockSpec((tm, width), lambda i: (i, 0))

    def resident(arr):
        return pl.BlockSpec(arr.shape, lambda i: (0,) * arr.ndim, pipeline_mode=pl.Buffered(1))

    kv_spec = pl.BlockSpec((tm * A_HEADS, 128), lambda i: (i, 0))
    kv_shape = jax.ShapeDtypeStruct((m * A_HEADS, 128), F32)
    g = g.reshape(1, D_MODEL)
    return pl.pallas_call(
        _in_proj_kernel,
        grid=(m // tm,),
        in_specs=[rows(D_MODEL), resident(g), resident(wg), resident(wr), resident(gains),
                  resident(ones_bd)],
        out_specs=[rows(GZ_WIDTH), rows(ZR_WIDTH), kv_spec, kv_spec],
        out_shape=[jax.ShapeDtypeStruct((m, GZ_WIDTH), BF16),
                   jax.ShapeDtypeStruct((m, ZR_WIDTH), F32), kv_shape, kv_shape],
        compiler_params=_cparams(1),
        name="in_proj",
    )(x, g, wg, wr, gains, ones_bd)


def _bucket(n):
    max_exact = REL_BUCKETS // 2
    large = max_exact + (jnp.log(jnp.maximum(n, 1).astype(F32) / max_exact)
                         / math.log(REL_MAX_DIST / max_exact)
                         * (REL_BUCKETS - max_exact)).astype(jnp.int32)
    return jnp.where(n < max_exact, n, jnp.minimum(large, REL_BUCKETS - 1))


def _bias_kernel(tab_ref, bp_ref, bs_ref, *, t_dec):
    h = pl.program_id(0)

    def lookup(n):
        bucket = _bucket(n)
        val = jnp.full(n.shape, tab_ref[h, REL_BUCKETS - 1], F32)
        for b in range(REL_BUCKETS - 1):
            val = jnp.where(bucket == b, tab_ref[h, b], val)
        return val

    tq = bp_ref.shape[2]
    key = lax.broadcasted_iota(jnp.int32, (tq, tq), 0)
    qry = lax.broadcasted_iota(jnp.int32, (tq, tq), 1)
    bp_ref[0, 0] = jnp.where(key <= qry, lookup(jnp.maximum(qry - key, 0)) * LOG2E, NEG_INF)
    bp_ref[0, 1] = lookup(qry - key + tq) * LOG2E

    rows, cols = bs_ref.shape[1], bs_ref.shape[2]
    r = lax.broadcasted_iota(jnp.int32, (rows, cols), 0)
    c = lax.broadcasted_iota(jnp.int32, (rows, cols), 1)
    t = r % t_dec
    tok = c // A_HEADS
    valid = (c % A_HEADS) == h
    far = jnp.full((rows, cols), tab_ref[h, REL_BUCKETS - 1], F32)
    bs_ref[0] = jnp.where(valid, far, NEG_INF)
    bs_ref[1] = jnp.where(valid, lookup(PAGE_SIZE + t - tok), NEG_INF)
    new_ok = valid & (tok <= t) & (tok < t_dec)
    bs_ref[2] = jnp.where(new_ok, lookup(jnp.maximum(t - tok, 0)), NEG_INF)


def _bias_tiles(table, t_dec):
    rows = 2 * t_dec
    return pl.pallas_call(
        functools.partial(_bias_kernel, t_dec=t_dec),
        grid=(A_HEADS,),
        in_specs=[pl.BlockSpec(memory_space=pltpu.SMEM)],
        out_specs=[pl.BlockSpec((1, 2, ATT_TQ, ATT_TQ), lambda h: (h, 0, 0, 0)),
                   pl.BlockSpec((3, rows, PAGE_SIZE * A_HEADS), lambda h: (0, h, 0))],
        out_shape=[jax.ShapeDtypeStruct((A_HEADS, 2, ATT_TQ, ATT_TQ), F32),
                   jax.ShapeDtypeStruct((3, A_HEADS * rows, PAGE_SIZE * A_HEADS), F32)],
        compiler_params=_cparams(1),
        name="bias_tiles",
    )(table.T)


def _lam(lamv_ref, lam_init):
    s1 = jnp.sum(lamv_ref[0:1, :] * lamv_ref[1:2, :], axis=1, keepdims=True)
    s2 = jnp.sum(lamv_ref[2:3, :] * lamv_ref[3:4, :], axis=1, keepdims=True)
    return jnp.exp(s1) - jnp.exp(s2) + lam_init


def _subln(o, g_ref, lam_init):
    y = o * lax.rsqrt(jnp.mean(o * o, axis=-1, keepdims=True) + EPS)
    return y * g_ref[...] * (1.0 - lam_init)


def _attn_prompt_kernel(tab_ref, q_ref, k_ref, v_ref, bias_ref, lamv_ref, subg_ref, o_ref,
                        kb, vt, m_s, acc_s, *, lam_init):
    hg = pl.program_id(1)
    qi = pl.program_id(2)
    tq = q_ref.shape[1]

    n_hd = kb.shape[0]
    d = A_V_DIM
    chains = [(hh, m) for hh in range(n_hd) for m in range(2)]

    @pl.when(qi == 0)
    def _():
        for hh in range(n_hd):
            hcols = slice(hh * 128, (hh + 1) * 128)
            kb[hh] = k_ref[0, :, hcols].astype(BF16)
            for jj in range(vt.shape[1]):
                vt[hh, jj, :d] = jnp.transpose(v_ref[0, jj * tq:(jj + 1) * tq, hcols]).astype(BF16)
                vt[hh, jj, d:] = jnp.ones((vt.shape[2] - d, tq), BF16)

    sub = lax.broadcasted_iota(jnp.int32, (128, tq), 0)
    qm = {}
    for hh in range(n_hd):
        qt = jnp.transpose(q_ref[0, :, hh * 128:(hh + 1) * 128]) * LOG2E
        qm[hh, 0] = jnp.where(sub < A_QK_DIM, qt, 0.0).astype(BF16)
        qm[hh, 1] = jnp.where(sub >= A_QK_DIM, qt, 0.0).astype(BF16)
    m_s[...] = jnp.full(m_s.shape, NEG_INF, F32)
    acc_s[...] = jnp.zeros(acc_s.shape, F32)

    def step(j, bias):
        start = pl.multiple_of(j * tq, tq)
        m_prev = [m_s[c] for c in range(len(chains))]
        ss = [_dot(kb[hh, pl.ds(start, tq), :], qm[hh, m]) + bias[hh] for hh, m in chains]
        m_new = [jnp.maximum(mp, jnp.max(s, axis=0, keepdims=True)) for mp, s in zip(m_prev, ss)]
        ps = [jnp.exp2(s - mn).astype(BF16) for s, mn in zip(ss, m_new)]
        for c, (hh, m) in enumerate(chains):
            acc_s[c] = jnp.exp2(m_prev[c] - m_new[c]) * acc_s[c] + _dot(vt[hh, j], ps[c])
            m_s[c] = m_new[c]

    far = [tab_ref[hg * n_hd + hh, REL_BUCKETS - 1] * LOG2E for hh in range(n_hd)]

    def far_body(j, carry):
        step(j, far)
        return carry

    lax.fori_loop(0, jnp.maximum(qi - 1, 0), far_body, 0)

    @pl.when(qi >= 1)
    def _():
        step(qi - 1, [bias_ref[hh, 1] for hh in range(n_hd)])

    step(qi, [bias_ref[hh, 0] for hh in range(n_hd)])

    lam = _lam(lamv_ref, lam_init)
    for hh in range(n_hd):
        a0, a1 = acc_s[2 * hh], acc_s[2 * hh + 1]
        o = a0[:d] * (1.0 / a0[d:d + 1]) - lam * (a1[:d] * (1.0 / a1[d:d + 1]))
        y = o * lax.rsqrt(jnp.mean(o * o, axis=0, keepdims=True) + EPS)
        y = y * subg_ref[...] * (1.0 - lam_init)
        o_ref[0, :, hh * 128:(hh + 1) * 128] = jnp.transpose(y).astype(o_ref.dtype)


def _attn_prompt(zr3, bias_p, tab_t, lamv, subg, lam_init):
    b, s, _ = zr3.shape
    tq = ATT_TQ
    n_hd = ATT_HEADS_PER_STEP
    wid = 128 * n_hd
    kern = functools.partial(_attn_prompt_kernel, lam_init=lam_init)
    return pl.pallas_call(
        kern,
        grid=(b, A_HEADS // n_hd, s // tq),
        in_specs=[pl.BlockSpec(memory_space=pltpu.SMEM),
                  pl.BlockSpec((1, tq, wid), lambda bi, hg, qi: (bi, qi, COL_AQ // wid + hg)),
                  pl.BlockSpec((1, s, wid), lambda bi, hg, qi: (bi, 0, COL_AK // wid + hg)),
                  pl.BlockSpec((1, s, wid), lambda bi, hg, qi: (bi, 0, COL_AV // wid + hg)),
                  pl.BlockSpec((n_hd, 2, tq, tq), lambda bi, hg, qi: (hg, 0, 0, 0)),
                  pl.BlockSpec((4, A_QK_DIM), lambda bi, hg, qi: (0, 0)),
                  pl.BlockSpec((A_V_DIM, 1), lambda bi, hg, qi: (0, 0))],
        out_specs=pl.BlockSpec((1, tq, wid), lambda bi, hg, qi: (bi, qi, hg)),
        out_shape=jax.ShapeDtypeStruct((b, s, A_WIDTH), BF16),
        scratch_shapes=[pltpu.VMEM((n_hd, s, 128), BF16),
                        pltpu.VMEM((n_hd, s // tq, A_V_DIM + ONES_ROWS, tq), BF16),
                        pltpu.VMEM((2 * n_hd, 1, tq), F32),
                        pltpu.VMEM((2 * n_hd, A_V_DIM + ONES_ROWS, tq), F32)],
        compiler_params=_cparams(3),
        name="attn_prompt",
    )(tab_t, zr3, zr3, zr3, bias_p, lamv, subg.reshape(A_V_DIM, 1))


def _attn_sample_kernel(pt_ref, q_ref, kn_ref, vn_ref, bias_ref, lamv_ref, subg_ref, *rest,
                        lam_init, n_pg, t_dec):
    k_refs = rest[:n_pg]
    v_refs = rest[n_pg:2 * n_pg]
    o_ref = rest[2 * n_pg]
    q_s, m_s, l_s, acc_s = rest[2 * n_pg + 1:]
    j = pl.program_id(1)
    last = pl.num_programs(1) - 1
    rows = 2 * t_dec

    @pl.when(j == 0)
    def _():
        q = q_ref[0]
        lane = lax.broadcasted_iota(jnp.int32, (t_dec, 128), 1)
        for h in range(A_HEADS):
            qh = q[:, h * 128:(h + 1) * 128]
            q_s[h * rows:h * rows + t_dec, :] = jnp.where(lane < A_QK_DIM, qh, 0.0)
            q_s[h * rows + t_dec:(h + 1) * rows, :] = jnp.where(lane >= A_QK_DIM, qh, 0.0)
        m_s[...] = jnp.full(m_s.shape, NEG_INF, F32)
        l_s[...] = jnp.zeros(l_s.shape, F32)
        acc_s[...] = jnp.zeros(acc_s.shape, F32)

    qb = q_s[...].astype(BF16)

    def update(ks, vs, biases):
        ss = [_dot_nt(qb, kb) + bias for kb, bias in zip(ks, biases)]
        smax = functools.reduce(jnp.maximum, ss)
        m_prev = m_s[...]
        m_new = jnp.maximum(m_prev, jnp.max(smax, axis=1, keepdims=True))
        ps = [jnp.exp(s - m_new) for s in ss]
        alpha = jnp.exp(m_prev - m_new)
        l_s[...] = alpha * l_s[...] + jnp.sum(functools.reduce(jnp.add, ps), axis=1, keepdims=True)
        pv = functools.reduce(jnp.add, [_dot(p.astype(BF16), vb) for p, vb in zip(ps, vs)])
        acc_s[...] = alpha * acc_s[...] + pv
        m_s[...] = m_new

    biases = [bias_ref[0]] * (n_pg - 1) + [jnp.where(j == last, bias_ref[1], bias_ref[0])]
    update([r[0, 0].astype(BF16) for r in k_refs], [r[0, 0].astype(BF16) for r in v_refs], biases)

    @pl.when(j == last)
    def _():
        nk = kn_ref.shape[1]
        update([kn_ref[0].astype(BF16)], [vn_ref[0].astype(BF16)], [bias_ref[2][:, :nk]])
        lam = _lam(lamv_ref, lam_init)
        on = acc_s[...] * (1.0 / l_s[...])
        for h in range(A_HEADS):
            o = on[h * rows:h * rows + t_dec] - lam * on[h * rows + t_dec:(h + 1) * rows]
            o_ref[0, :, h * 128:(h + 1) * 128] = _subln(o, subg_ref, lam_init).astype(o_ref.dtype)


def _attn_sample(zrs3, kn2d, vn2d, cache_k2d, cache_v2d, page_table, bias_s, lamv, subg,
                 layer, lam_init):
    bd, t_dec, _ = zrs3.shape
    n_pages = page_table.shape[1]
    n_pg = min(SAMPLE_PAGES, n_pages)
    rows = 2 * t_dec * A_HEADS
    pcols = PAGE_SIZE * A_HEADS

    def page_spec(p):
        return pl.BlockSpec((1, 1, pcols, 128),
                            lambda b, j, pt: (layer, pt[b, j * n_pg + p], 0, 0))

    kern = functools.partial(_attn_sample_kernel, lam_init=lam_init, n_pg=n_pg, t_dec=t_dec)
    grid_spec = pltpu.PrefetchScalarGridSpec(
        num_scalar_prefetch=1,
        grid=(bd, n_pages // n_pg),
        in_specs=[pl.BlockSpec((1, t_dec, A_WIDTH), lambda b, j, pt: (b, 0, 0)),
                  pl.BlockSpec((1,) + kn2d.shape[1:], lambda b, j, pt: (b, 0, 0)),
                  pl.BlockSpec((1,) + vn2d.shape[1:], lambda b, j, pt: (b, 0, 0)),
                  pl.BlockSpec((3, rows, pcols), lambda b, j, pt: (0, 0, 0)),
                  pl.BlockSpec((4, A_QK_DIM), lambda b, j, pt: (0, 0)),
                  pl.BlockSpec((1, A_V_DIM), lambda b, j, pt: (0, 0))]
                 + [page_spec(p) for p in range(n_pg)] * 2,
        out_specs=pl.BlockSpec((1, t_dec, A_WIDTH), lambda b, j, pt: (b, 0, 0)),
        scratch_shapes=[pltpu.VMEM((rows, 128), F32), pltpu.VMEM((rows, 1), F32),
                        pltpu.VMEM((rows, 1), F32), pltpu.VMEM((rows, 128), F32)],
    )
    return pl.pallas_call(
        kern,
        grid_spec=grid_spec,
        out_shape=jax.ShapeDtypeStruct((bd, t_dec, A_WIDTH), BF16),
        compiler_params=_cparams(2),
        name="attn_sample",
    )(page_table, zrs3, kn2d, vn2d, bias_s, lamv, subg,
      *([cache_k2d] * n_pg), *([cache_v2d] * n_pg))


def _gelu(x):
    return 0.5 * x * (1.0 + jnp.tanh(math.sqrt(2.0 / math.pi) * (x + 0.044715 * (x * x * x))))


def _chunk_mlp_kernel(u_ref, v_ref, w_ref, bias_ref, g_ref, b_ref, y_ref, vb_ref):
    n_chunks = u_ref.shape[0] // B_CHUNK
    lane_grp = lax.broadcasted_iota(jnp.int32, (B_CHUNK, B_WIDTH), 1) // (B_WIDTH // B_GROUPS)
    for c in range(n_chunks):
        sl = slice(c * B_CHUNK, (c + 1) * B_CHUNK)
        gv = _gelu(v_ref[sl, :])
        xc = gv - jnp.mean(gv, axis=-1, keepdims=True)
        vb = xc * lax.rsqrt(jnp.mean(xc * xc, axis=-1, keepdims=True) + EPS) * g_ref[...] + b_ref[...]
        vb_ref[sl, :] = vb
        vbb = vb.astype(BF16)
        mixed = bias_ref[...]
        for g in range(B_GROUPS):
            mixed = mixed + jnp.where(lane_grp == g, _dot(w_ref[g], vbb), 0.0)
        y_ref[sl, :] = (_gelu(u_ref[sl, :]) * mixed).astype(y_ref.dtype)


def _chunk_mlp(zr, w_eff, bias_eff, ln_g, ln_b):
    m = zr.shape[0]
    tm = min(m, CHUNK_MLP_TM)
    return pl.pallas_call(
        _chunk_mlp_kernel,
        grid=(m // tm,),
        in_specs=[pl.BlockSpec((tm, B_WIDTH), lambda i: (i, COL_BU // B_WIDTH)),
                  pl.BlockSpec((tm, B_WIDTH), lambda i: (i, COL_BV // B_WIDTH)),
                  pl.BlockSpec((B_GROUPS, B_CHUNK, B_CHUNK), lambda i: (0, 0, 0)),
                  pl.BlockSpec((B_CHUNK, B_WIDTH), lambda i: (0, 0)),
                  pl.BlockSpec((1, B_WIDTH), lambda i: (0, 0)),
                  pl.BlockSpec((1, B_WIDTH), lambda i: (0, 0))],
        out_specs=[pl.BlockSpec((tm, B_WIDTH), lambda i: (i, 0)),
                   pl.BlockSpec((tm, B_WIDTH), lambda i: (i, 0))],
        out_shape=[jax.ShapeDtypeStruct((m, B_WIDTH), BF16),
                   jax.ShapeDtypeStruct((m, B_WIDTH), F32)],
        compiler_params=_cparams(1),
        name="chunk_mlp",
    )(zr, zr, w_eff, bias_eff, ln_g, ln_b)


def _pool_kernel(x_ref, w_ref, scale_ref, y_ref, *, prefix, pos0, seq_rows):
    x = x_ref[0]
    row = lax.broadcasted_iota(jnp.int32, x.shape, 0) % seq_rows
    grp = lax.broadcasted_iota(jnp.int32, x.shape, 1) // C_GROUP_DIM

    def shifted(a, k):
        return jnp.where(row >= k, pltpu.roll(a, k, 0), 0.0)

    sums = []
    acc = x
    for k in (1, 2, 4, 8):
        acc = acc + shifted(acc, k)
        sums.append(acc)
    total = sums[3]
    win = jnp.full(x.shape, POOL_WINDOWS[3], jnp.int32)
    for g in range(3):
        total = jnp.where(grp == g, sums[g], total)
        win = jnp.where(grp == g, POOL_WINDOWS[g], win)
    pos = pos0 + row - prefix
    cnt = jnp.clip(pos + 1, 1, win).astype(F32)
    d = total / cnt - x
    y = _dot(d.astype(BF16), w_ref[...]) * scale_ref[...]
    y_ref[0] = y.astype(y_ref.dtype)


def _pool_mix(xx, col_block, w_bd, scale, prefix, pos0, seq_rows):
    b, rows, _ = xx.shape
    kern = functools.partial(_pool_kernel, prefix=prefix, pos0=pos0, seq_rows=seq_rows)
    return pl.pallas_call(
        kern,
        grid=(b,),
        in_specs=[pl.BlockSpec((1, rows, C_WIDTH), lambda i: (i, 0, col_block)),
                  pl.BlockSpec((C_WIDTH, C_WIDTH), lambda i: (0, 0)),
                  pl.BlockSpec((1, C_WIDTH), lambda i: (0, 0))],
        out_specs=pl.BlockSpec((1, rows, C_WIDTH), lambda i: (i, 0, 0)),
        out_shape=jax.ShapeDtypeStruct((b, rows, C_WIDTH), BF16),
        compiler_params=_cparams(1),
        name="pool_mix",
    )(xx, w_bd, scale)


def _log_sigmoid(x):
    return jnp.minimum(x, 0.0) - jnp.log1p(jnp.exp(-jnp.abs(x)))


def _split3(x):
    def top8(a):
        bits = lax.bitcast_convert_type(a, jnp.int32) & jnp.int32(-65536)
        return lax.bitcast_convert_type(bits, F32)

    p1 = top8(x)
    r1 = x - p1
    p2 = top8(r1)
    return p1, p2, r1 - p2


def _pack3(pieces):
    return (pieces[0] + pltpu.roll(pieces[1], D_HEADS, 1) + pltpu.roll(pieces[2], 2 * D_HEADS, 1))


def _mlstm_kernel(q_ref, k_ref, v_ref, o_ref, gi_ref, gf_ref, gb_ref, ng_ref, sel_ref, bd_ref,
                  c0_ref, n0_ref, m0_ref, y_ref, c_out, n_out, m_out, c_s, n_s, m_s, *, t_valid):
    ci = pl.program_id(1)
    nb, chunk = q_ref.shape[0], q_ref.shape[1]
    hd = D_HEAD_DIM
    eye_h = (lax.broadcasted_iota(jnp.int32, (hd, hd), 0)
             == lax.broadcasted_iota(jnp.int32, (hd, hd), 1))

    @pl.when(ci == 0)
    def _():
        c_s[...] = jnp.zeros(c_s.shape, F32)
        n_s[...] = jnp.zeros(n_s.shape, F32)
        m_s[...] = m0_ref[...]
        for bi in range(nb):
            for h in range(D_HEADS):
                hs = slice(h * hd, (h + 1) * hd)
                c_s[bi, hs, hs] = c0_ref[bi, h]
                n_col = jnp.sum(jnp.where(eye_h, n0_ref[bi, h], 0.0), axis=1, keepdims=True)
                n_s[bi, hs, hs] = jnp.broadcast_to(n_col, (hd, hd))

    row = lax.broadcasted_iota(jnp.int32, (chunk, 128), 0)
    lane = lax.broadcasted_iota(jnp.int32, (chunk, 128), 1)
    head_lane = lane < D_HEADS
    grp = lax.broadcasted_iota(jnp.int32, (chunk, D_WIDTH), 1) // hd
    rr = lax.broadcasted_iota(jnp.int32, (chunk, chunk), 0)
    cc = lax.broadcasted_iota(jnp.int32, (chunk, chunk), 1)
    allowed = (cc <= rr) & (cc < t_valid)
    ones_w = jnp.ones((chunk, D_WIDTH), BF16)
    same_head = (lax.broadcasted_iota(jnp.int32, (D_WIDTH, D_WIDTH), 0) // hd
                 == lax.broadcasted_iota(jnp.int32, (D_WIDTH, D_WIDTH), 1) // hd)
    pick = [(lane % D_HEADS == h) & (lane < 6 * D_HEADS) for h in range(D_HEADS)]
    in_head = [grp == h for h in range(D_HEADS)]
    lower_ones = jnp.where(lane < 3 * D_HEADS, 1.0, 0.0)
    upper_ones = jnp.where((lane >= 3 * D_HEADS) & (lane < 6 * D_HEADS), 1.0, 0.0)
    head_one = [jnp.where(ih, 1.0, 0.0).astype(BF16) for ih in in_head]

    for bi in range(nb):
        gi = gi_ref[bi] + gb_ref[0:1, :]
        lf = _log_sigmoid(gf_ref[bi] + gb_ref[1:2, :])
        if t_valid < chunk:
            gi = jnp.where(row < t_valid, gi, NEG_INF)
            lf = jnp.where(row < t_valid, lf, 0.0)
        b = lf
        k = 1
        while k < chunk:
            b = b + jnp.where(row >= k, pltpu.roll(b, k, 0), 0.0)
            k *= 2
        u = gi - b
        cm = u
        k = 1
        while k < chunk:
            cm = jnp.maximum(cm, jnp.where(row >= k, pltpu.roll(cm, k, 0), NEG_INF))
            k *= 2
        m_prev = m_s[bi]
        big_m = jnp.maximum(m_prev, cm)
        m_last = big_m[chunk - 1:chunk, :]
        winter = jnp.exp(m_prev - big_m)
        emt = jnp.exp(jnp.minimum(-(b + big_m), EMT_CAP))
        ws = jnp.exp(u - m_last)
        m_s[bi] = b[chunk - 1:chunk, :] + m_last

        def per_head_lanes(z):
            packed = _pack3(_split3(jnp.where(head_lane, z, 0.0)))
            return _dot(packed.astype(BF16), sel_ref[...])

        winter_r, emt_r, ws_r = per_head_lanes(winter), per_head_lanes(emt), per_head_lanes(ws)
        decay_r = winter_r[chunk - 1:chunk, :]

        q = q_ref[bi]
        qb = q.astype(BF16)
        kf = k_ref[bi] * (hd ** -0.5)
        kb = kf.astype(BF16)
        vf = v_ref[bi]
        vb = vf.astype(BF16)
        u_fin = jnp.where(head_lane & (row < t_valid), u, 0.0)
        y_side = (_pack3(_split3(u_fin)) + upper_ones).astype(BF16)
        x_all = lower_ones + pltpu.roll(_pack3(_split3(jnp.where(head_lane, -big_m, 0.0))),
                                        3 * D_HEADS, 1)
        acc = jnp.zeros((chunk, 2 * D_WIDTH), F32)
        for h in range(D_HEADS):
            x_side = jnp.where(pick[h], x_all, 0.0).astype(BF16)
            expo = jnp.where(allowed, _dot_nt(x_side, y_side), NEG_INF)
            qk = _dot_nt(jnp.where(in_head[h], q, 0.0).astype(BF16), kb) * jnp.exp(expo)
            rhs = jnp.concatenate([jnp.where(in_head[h], vf, 0.0).astype(BF16), head_one[h]], axis=1)
            acc = acc + _dot(qk.astype(BF16), rhs)
        state = jnp.concatenate([c_s[bi].astype(BF16), n_s[bi].astype(BF16)], axis=1)
        inter = _dot(qb, state)
        num = winter_r * inter[:, :D_WIDTH] + acc[:, :D_WIDTH]
        den = winter_r * inter[:, D_WIDTH:] + acc[:, D_WIDTH:]
        hh = num / jnp.maximum(jnp.abs(den), emt_r)
        h2 = hh * hh
        hi = h2.astype(BF16)
        lo = (h2 - hi.astype(F32)).astype(BF16)
        ssq = _dot(hi, bd_ref[...]) + _dot(lo, bd_ref[...])
        y = hh * lax.rsqrt(ssq * (1.0 / hd) + EPS) * ng_ref[...]
        y_ref[bi] = (y * jax.nn.sigmoid(o_ref[bi])).astype(y_ref.dtype)
        kw = (ws_r * kf).astype(BF16)
        upd = _dot_tn(kw, jnp.concatenate([vb, ones_w], axis=1))
        c_s[bi] = decay_r * c_s[bi] + jnp.where(same_head, upd[:, :D_WIDTH], 0.0)
        n_s[bi] = decay_r * n_s[bi] + jnp.where(same_head, upd[:, D_WIDTH:], 0.0)

    @pl.when(ci == pl.num_programs(1) - 1)
    def _():
        m_out[...] = m_s[...]
        for bi in range(nb):
            for h in range(D_HEADS):
                hs = slice(h * hd, (h + 1) * hd)
                c_out[bi, h] = c_s[bi, hs, hs]
                n_out[bi, h] = jnp.sum(jnp.where(eye_h, n_s[bi, hs, hs], 0.0), axis=0, keepdims=True)


def _mlstm(src, col0, gate_block, chunk, t_valid, gate_bias, norm_g, c0, n0, m0):
    b, t, _ = src.shape
    nb = math.gcd(b, MLSTM_BATCH)
    hd = D_HEAD_DIM
    kern = functools.partial(_mlstm_kernel, t_valid=t_valid)
    head_of_lane = jnp.arange(D_WIDTH) // hd
    src_lane = jnp.arange(128)[:, None]
    sel3 = ((src_lane % D_HEADS == head_of_lane[None, :]) & (src_lane < 3 * D_HEADS)).astype(BF16)
    same_head = (head_of_lane[:, None] == head_of_lane[None, :]).astype(BF16)
    m0p = jnp.pad(m0, ((0, 0), (0, 0), (0, 128 - D_HEADS)))

    def col(cb, width=D_WIDTH):
        return pl.BlockSpec((nb, chunk, width), lambda bi, ci: (bi, ci, cb))

    def const(shape):
        return pl.BlockSpec(shape, lambda bi, ci: (0,) * len(shape))

    def per_seq(shape):
        return pl.BlockSpec((nb,) + shape, lambda bi, ci: (bi,) + (0,) * len(shape))

    y, c_new, n_new, m_new = pl.pallas_call(
        kern,
        grid=(b // nb, t // chunk),
        in_specs=[col(col0), col(col0 + 1), col(col0 + 2), col(col0 + 3),
                  col(gate_block, 128), col(gate_block + 1, 128),
                  const((2, 128)), const((1, D_WIDTH)), const((128, D_WIDTH)),
                  const((D_WIDTH, D_WIDTH)),
                  per_seq((D_HEADS, hd, hd)), per_seq((D_HEADS, 1, hd)), per_seq((1, 128))],
        out_specs=[col(0), per_seq((D_HEADS, hd, hd)), per_seq((D_HEADS, 1, hd)),
                   per_seq((1, 128))],
        out_shape=[jax.ShapeDtypeStruct((b, t, D_WIDTH), BF16),
                   jax.ShapeDtypeStruct((b, D_HEADS, hd, hd), F32),
                   jax.ShapeDtypeStruct((b, D_HEADS, 1, hd), F32),
                   jax.ShapeDtypeStruct((b, 1, 128), F32)],
        scratch_shapes=[pltpu.VMEM((nb, D_WIDTH, D_WIDTH), F32),
                        pltpu.VMEM((nb, D_WIDTH, D_WIDTH), F32),
                        pltpu.VMEM((nb, 1, 128), F32)],
        compiler_params=_cparams(2),
        name="mlstm",
    )(src, src, src, src, src, src, gate_bias, jnp.tile(norm_g, (1, D_HEADS)), sel3, same_head,
      c0, n0, m0p)
    return y, c_new, n_new, m_new[:, :, :D_HEADS]


def _merge_kernel(x_ref, g_ref, ya_ref, yb_ref, yc_ref, yd_ref, wa_ref, wb_ref, wc_ref, wd_ref,
                  wo_ref, o_ref):
    def gate(i):
        return g_ref[:, i * D_MODEL:(i + 1) * D_MODEL].astype(F32)

    merged = gate(0) * _dot(ya_ref[...], wa_ref[...])
    merged = merged + gate(1) * _dot(yb_ref[...], wb_ref[...])
    merged = merged + gate(2) * _dot(yc_ref[...], wc_ref[...])
    merged = merged + gate(3) * _dot(yd_ref[...], wd_ref[...])
    o_ref[...] = x_ref[...] + _dot(merged.astype(BF16), wo_ref[...])


def _merge(x, gates, ya, yb, yc, yd, wa, wb, wc, wd, wo):
    m = x.shape[0]
    tm = min(m, 512)

    def rows(width):
        return pl.BlockSpec((tm, width), lambda i: (i, 0))

    def full(arr):
        return pl.BlockSpec(arr.shape, lambda i: (0, 0))

    return pl.pallas_call(
        _merge_kernel,
        grid=(m // tm,),
        in_specs=[rows(D_MODEL), rows(GZ_WIDTH), rows(A_WIDTH), rows(B_WIDTH), rows(C_WIDTH),
                  rows(D_WIDTH), full(wa), full(wb), full(wc), full(wd), full(wo)],
        out_specs=rows(D_MODEL),
        out_shape=jax.ShapeDtypeStruct((m, D_MODEL), F32),
        compiler_params=_cparams(1),
        name="merge",
    )(x, gates, ya, yb, yc, yd, wa, wb, wc, wd, wo)


def _ffn_kernel(x_ref, g_ref, w1_ref, w2_ref, o_ref, h_s, acc_s):
    j = pl.program_id(1)

    @pl.when(j == 0)
    def _():
        x = x_ref[...]
        y = x * lax.rsqrt(jnp.mean(x * x, axis=-1, keepdims=True) + EPS)
        h_s[...] = (y * g_ref[...]).astype(BF16)
        acc_s[...] = jnp.zeros(acc_s.shape, F32)

    a = jnp.maximum(_dot(h_s[...], w1_ref[...]), 0.0)
    acc_s[...] += _dot((a * a).astype(BF16), w2_ref[...])

    @pl.when(j == pl.num_programs(1) - 1)
    def _():
        o_ref[...] = x_ref[...] + acc_s[...]


def _ffn(x, g, w1, w2):
    m = x.shape[0]
    tm, tf = min(m, 1024), 1024
    return pl.pallas_call(
        _ffn_kernel,
        grid=(m // tm, D_FF // tf),
        in_specs=[pl.BlockSpec((tm, D_MODEL), lambda i, j: (i, 0)),
                  pl.BlockSpec((1, D_MODEL), lambda i, j: (0, 0)),
                  pl.BlockSpec((D_MODEL, tf), lambda i, j: (0, j)),
                  pl.BlockSpec((tf, D_MODEL), lambda i, j: (j, 0))],
        out_specs=pl.BlockSpec((tm, D_MODEL), lambda i, j: (i, 0)),
        out_shape=jax.ShapeDtypeStruct((m, D_MODEL), F32),
        scratch_shapes=[pltpu.VMEM((tm, D_MODEL), BF16), pltpu.VMEM((tm, D_MODEL), F32)],
        compiler_params=_cparams(2),
        name="ffn",
    )(x, g.reshape(1, D_MODEL), w1, w2)


def _layer_weights(p, l):
    w_in = p["w_in"][l]
    gate_pad = jnp.zeros((D_MODEL, 128 - D_HEADS), F32)
    wr = jnp.concatenate(
        [w_in[:, :3072], w_in[:, 3080:3336], w_in[:, 3072:3076], gate_pad,
         w_in[:, 3076:3080], gate_pad], axis=1).astype(BF16)
    wg = w_in[:, 3336:].astype(BF16)
    reps = A_WIDTH // A_QK_DIM
    gains = jnp.stack([jnp.tile(p["q_norm_g"][l], reps) * (A_QK_DIM ** -0.5),
                       jnp.tile(p["k_norm_g"][l], reps)]).reshape(2, 1, A_WIDTH)
    grp = jnp.arange(A_WIDTH) // A_QK_DIM
    ones_bd = (grp[:, None] == grp[None, :]).astype(BF16)
    lamv = jnp.stack([p["lam_q1"][l], p["lam_k1"][l], p["lam_q2"][l], p["lam_k2"][l]])
    gate_bias = jnp.pad(jnp.stack([p["d_i_bias"][l], p["d_f_bias"][l]]),
                        ((0, 0), (0, 128 - D_HEADS)))
    c_bd = jnp.zeros((C_WIDTH, C_WIDTH), F32)
    for g in range(4):
        sl = slice(g * C_GROUP_DIM, (g + 1) * C_GROUP_DIM)
        c_bd = c_bd.at[sl, sl].set(p["c_lin"][l][g])
    return dict(
        norm1_g=p["norm1_g"][l], norm2_g=p["norm2_g"][l], wr=wr, wg=wg, gains=gains,
        ones_bd=ones_bd, lamv=lamv, subg=p["subln_g"][l].reshape(1, A_V_DIM),
        b_ln_g=p["b_ln_g"][l].reshape(1, B_WIDTH), b_ln_b=p["b_ln_b"][l].reshape(1, B_WIDTH),
        b_ws=p["b_ws"][l], b_bias=p["b_bias"][l],
        c_bd=c_bd.astype(BF16), c_scale=p["c_scale"][l].reshape(1, C_WIDTH),
        gate_bias=gate_bias, d_norm_g=p["d_norm_g"][l].reshape(1, D_HEAD_DIM),
        w_pa=p["w_pa"][l].astype(BF16), w_pb=p["w_pb"][l].astype(BF16),
        w_pc=p["w_pc"][l].astype(BF16), w_pd=p["w_pd"][l].astype(BF16),
        w_out=p["w_out"][l].astype(BF16), w_ff1=p["w_ff1"][l].astype(BF16),
        w_ff2=p["w_ff2"][l].astype(BF16),
        lam_init=0.8 - 0.6 * math.exp(-0.3 * l),
    )


def _chunk_weights(w, t):
    length = min(t, B_CHUNK)
    ws = jnp.tril(w["b_ws"][:, :length, :length])
    bias = jnp.transpose(w["b_bias"][:, :length])
    reps = B_CHUNK // length
    if reps > 1:
        eye = jnp.eye(reps, dtype=F32)
        ws = jax.vmap(lambda a: jnp.kron(eye, a))(ws)
        bias = jnp.tile(bias, (reps, 1))
    return ws.astype(BF16), jnp.repeat(bias, B_WIDTH // B_GROUPS, axis=1)


def _finish_layer(x2, w, gates, zr, ya, yc, yd, t):
    w_eff, bias_eff = _chunk_weights(w, t)
    yb, vb = _chunk_mlp(zr, w_eff, bias_eff, w["b_ln_g"], w["b_ln_b"])
    x2 = _merge(x2, gates, ya, yb, yc, yd, w["w_pa"], w["w_pb"], w["w_pc"], w["w_pd"], w["w_out"])
    return _ffn(x2, w["norm2_g"], w["w_ff1"], w["w_ff2"]), vb


def _prompt_layer(x2, w, b, s, bias_p, tab_t):
    m = b * s
    gates, zr, k_rows, v_rows = _in_proj(x2, w["norm1_g"], w["wg"], w["wr"], w["gains"],
                                         w["ones_bd"])
    zr3 = zr.reshape(b, s, ZR_WIDTH)
    ya = _attn_prompt(zr3, bias_p, tab_t, w["lamv"], w["subg"], w["lam_init"]).reshape(m, A_WIDTH)
    yc = _pool_mix(zr3, COL_CX // C_WIDTH, w["c_bd"], w["c_scale"], 0, 0, s).reshape(m, C_WIDTH)
    hd = D_HEAD_DIM
    yd, c_new, n_new, m_new = _mlstm(
        zr3, COL_DQ // D_WIDTH, COL_DGI // 128, min(s, MLSTM_CHUNK), min(s, MLSTM_CHUNK),
        w["gate_bias"],
        w["d_norm_g"], jnp.zeros((b, D_HEADS, hd, hd), F32), jnp.zeros((b, D_HEADS, 1, hd), F32),
        jnp.zeros((b, 1, D_HEADS), F32))
    x2, _ = _finish_layer(x2, w, gates, zr, ya, yc, yd.reshape(m, D_WIDTH), s)
    outs = (k_rows.reshape(b, s, A_HEADS, 2 * A_QK_DIM), v_rows.reshape(b, s, A_HEADS, A_V_DIM),
            zr3[:, s - POOL_BUF:, COL_CX:COL_CX + C_WIDTH],
            c_new, n_new.reshape(b, D_HEADS, hd), m_new.reshape(b, D_HEADS))
    return x2, outs


def _sample_layer(x2, w, l, bd, t, cache_k2d, cache_v2d, page_table, bias_s, pool0, c0, n0, m0):
    m = bd * t
    past = page_table.shape[1] * PAGE_SIZE
    gates, zr, kn, vn = _in_proj(x2, w["norm1_g"], w["wg"], w["wr"], w["gains"], w["ones_bd"])
    zr3 = zr.reshape(bd, t, ZR_WIDTH)
    ya = _attn_sample(zr3, kn.reshape(bd, t * A_HEADS, 128), vn.reshape(bd, t * A_HEADS, 128),
                      cache_k2d, cache_v2d, page_table, bias_s, w["lamv"], w["subg"], l,
                      w["lam_init"]).reshape(m, A_WIDTH)
    cx = zr3[:, :, COL_CX:COL_CX + C_WIDTH]
    prefix = POOL_BUF + 1
    rows = -(-(prefix + t) // 8) * 8
    xx = jnp.concatenate([jnp.zeros((bd, 1, C_WIDTH), F32), pool0, cx,
                          jnp.zeros((bd, rows - prefix - t, C_WIDTH), F32)], axis=1)
    yc = _pool_mix(xx.reshape(1, bd * rows, C_WIDTH), 0, w["c_bd"], w["c_scale"], prefix, past, rows)
    yc = yc.reshape(bd, rows, C_WIDTH)[:, prefix:prefix + t].reshape(m, C_WIDTH)
    chunk = -(-t // SAMPLE_MLSTM_CHUNK) * SAMPLE_MLSTM_CHUNK
    dsrc = jnp.pad(zr3[:, :, COL_DQ:COL_DGF + 128], ((0, 0), (0, chunk - t), (0, 0)))
    hd = D_HEAD_DIM
    yd, c_new, n_new, m_new = _mlstm(
        dsrc, 0, (COL_DGI - COL_DQ) // 128, chunk, t, w["gate_bias"], w["d_norm_g"],
        c0, n0.reshape(bd, D_HEADS, 1, hd), m0.reshape(bd, 1, D_HEADS))
    yd = yd[:, :t].reshape(m, D_WIDTH)
    x2, vb = _finish_layer(x2, w, gates, zr, ya, yc, yd, t)
    outs = (kn.reshape(bd, t, A_HEADS, 2 * A_QK_DIM), vn.reshape(bd, t, A_HEADS, A_V_DIM),
            vb.reshape(bd, t, B_WIDTH), jnp.concatenate([pool0, cx], axis=1)[:, -POOL_BUF:],
            c_new, n_new.reshape(bd, D_HEADS, hd), m_new.reshape(bd, D_HEADS))
    return x2, outs


def kernel(x_prompt, x_sample, cache_k, cache_v, page_table, state_pool, state_C, state_n, state_m, rel_bias, norm1_g, norm2_g, w_in, q_norm_g, k_norm_g, lam_q1, lam_k1, lam_q2, lam_k2, subln_g, b_ln_g, b_ln_b, b_ws, b_bias, c_lin, c_scale, d_i_bias, d_f_bias, d_norm_g, w_pa, w_pb, w_pc, w_pd, w_out, w_ff1, w_ff2):
    p = dict(norm1_g=norm1_g, norm2_g=norm2_g, w_in=w_in, q_norm_g=q_norm_g, k_norm_g=k_norm_g,
             lam_q1=lam_q1, lam_k1=lam_k1, lam_q2=lam_q2, lam_k2=lam_k2, subln_g=subln_g,
             b_ln_g=b_ln_g, b_ln_b=b_ln_b, b_ws=b_ws, b_bias=b_bias, c_lin=c_lin, c_scale=c_scale,
             d_i_bias=d_i_bias, d_f_bias=d_f_bias, d_norm_g=d_norm_g, w_pa=w_pa, w_pb=w_pb,
             w_pc=w_pc, w_pd=w_pd, w_out=w_out, w_ff1=w_ff1, w_ff2=w_ff2)
    depth = w_in.shape[0]
    bp, sp, _ = x_prompt.shape
    bd, td, _ = x_sample.shape
    n_phys = cache_k.shape[1]
    cache_k2d = cache_k.reshape(depth, n_phys, PAGE_SIZE * A_HEADS, 128)
    cache_v2d = cache_v.reshape(depth, n_phys, PAGE_SIZE * A_HEADS, 128)
    bias_p, bias_s = _bias_tiles(rel_bias, td)
    tab_t = rel_bias.T

    xp = x_prompt.reshape(bp * sp, D_MODEL)
    xs = x_sample.reshape(bd * td, D_MODEL)
    prompt_outs, sample_outs = [], []
    for l in range(depth):
        w = _layer_weights(p, l)
        xp, po = _prompt_layer(xp, w, bp, sp, bias_p, tab_t)
        xs, so = _sample_layer(xs, w, l, bd, td, cache_k2d, cache_v2d, page_table, bias_s,
                               state_pool[l], state_C[l], state_n[l], state_m[l])
        prompt_outs.append(po)
        sample_outs.append(so)

    def stack(outs, i):
        return jnp.stack([o[i] for o in outs])

    return (xp.reshape(bp, sp, D_MODEL), xs.reshape(bd, td, D_MODEL),
            stack(prompt_outs, 0), stack(prompt_outs, 1), stack(sample_outs, 0), stack(sample_outs, 1),
            stack(sample_outs, 2), stack(prompt_outs, 2), stack(sample_outs, 3),
            stack(prompt_outs, 3), stack(prompt_outs, 4), stack(prompt_outs, 5),
            stack(sample_outs, 4), stack(sample_outs, 5), stack(sample_outs, 6))
```

```python
import functools
import math

import jax
import jax.numpy as jnp
from jax import lax
from jax.experimental import pallas as pl
from jax.experimental.pallas import tpu as pltpu

F32 = jnp.float32
BF16 = jnp.bfloat16
NEG_INF = float("-inf")

D_MODEL = 1024
A_HEADS = 4
A_QK_DIM = 64
A_V_DIM = 128
A_WIDTH = 512
REL_BUCKETS = 32
REL_MAX_DIST = 128
PAGE_SIZE = 128
B_GROUPS = 4
B_WIDTH = 256
B_CHUNK = 128
C_WIDTH = 256
C_GROUP_DIM = 64
POOL_WINDOWS = (2, 4, 8, 16)
POOL_BUF = 15
D_HEADS = 4
D_WIDTH = 256
D_HEAD_DIM = 64
D_FF = 4096
N_BRANCH = 4
EPS = 1e-6

ZR_WIDTH = 3584
COL_AQ, COL_AK, COL_AV = 0, 512, 1024
COL_BU, COL_BV, COL_CX = 1536, 1792, 2048
COL_DQ, COL_DK, COL_DV, COL_DO = 2304, 2560, 2816, 3072
COL_DGI, COL_DGF = 3328, 3456
GZ_WIDTH = N_BRANCH * D_MODEL
EMT_CAP = 80.0
LOG2E = math.log2(math.e)
ONES_ROWS = 16

IN_PROJ_TM = 512
IN_PROJ_TN = 512
CHUNK_MLP_TM = 1024
ATT_TQ = 512
ATT_HEADS_PER_STEP = 2
FFN_ATTN_STEPS = 8
MLSTM_BATCH = 4
MLSTM_CHUNK = 256
SAMPLE_MLSTM_CHUNK = 32
VMEM_LIMIT = 56 * 1024 * 1024


def _cparams(n_axes):
    return pltpu.CompilerParams(dimension_semantics=("arbitrary",) * n_axes,
                                vmem_limit_bytes=VMEM_LIMIT)


def _dot(a, b):
    return jnp.dot(a, b, preferred_element_type=F32)


def _dot_nt(a, b):
    return lax.dot_general(a, b, (((1,), (1,)), ((), ())), preferred_element_type=F32)


def _dot_tn(a, b):
    return lax.dot_general(a, b, (((0,), (0,)), ((), ())), preferred_element_type=F32)


def _in_proj_kernel(x_ref, g_ref, wg_ref, wr_ref, gain_ref, ones_ref,
                    gates_ref, zr_ref, kout_ref, vout_ref):
    tm = x_ref.shape[0]
    tn = IN_PROJ_TN
    x = x_ref[...]
    h = (x * lax.rsqrt(jnp.mean(x * x, axis=-1, keepdims=True) + EPS) * g_ref[...]).astype(BF16)

    def head_rows(dst_ref, val):
        for hh in range(A_HEADS):
            dst_ref[pl.ds(hh, tm, stride=A_HEADS), :] = val[:, hh * 128:(hh + 1) * 128]

    for c in range(ZR_WIDTH // tn):
        cols = slice(c * tn, (c + 1) * tn)
        z = _dot(h, wr_ref[:, cols])
        if c * tn in (COL_AQ, COL_AK):
            z2 = z * z
            hi = z2.astype(BF16)
            lo = (z2 - hi.astype(F32)).astype(BF16)
            ssq = _dot(hi, ones_ref[...]) + _dot(lo, ones_ref[...])
            z = z * lax.rsqrt(ssq * (1.0 / A_QK_DIM) + EPS) * gain_ref[c]
        zr_ref[:, cols] = z
        if c * tn == COL_AK:
            head_rows(kout_ref, z)
        if c * tn == COL_AV:
            head_rows(vout_ref, z)

    for c in range(GZ_WIDTH // tn):
        cols = slice(c * tn, (c + 1) * tn)
        gates_ref[:, cols] = jax.nn.sigmoid(_dot(h, wg_ref[:, cols])).astype(gates_ref.dtype)


def _in_proj(x, g, wg, wr, gains, ones_bd):
    m = x.shape[0]
    tm = min(m, IN_PROJ_TM)

    def rows(width):
        return pl.BlockSpec((tm, width), lambda i: (i, 0))

    def resident(arr):
        return pl.BlockSpec(arr.shape, lambda i: (0,) * arr.ndim, pipeline_mode=pl.Buffered(1))

    kv_spec = pl.BlockSpec((tm * A_HEADS, 128), lambda i: (i, 0))
    kv_shape = jax.ShapeDtypeStruct((m * A_HEADS, 128), F32)
    g = g.reshape(1, D_MODEL)
    return pl.pallas_call(
        _in_proj_kernel,
        grid=(m // tm,),
        in_specs=[rows(D_MODEL), resident(g), resident(wg), resident(wr), resident(gains),
                  resident(ones_bd)],
        out_specs=[rows(GZ_WIDTH), rows(ZR_WIDTH), kv_spec, kv_spec],
        out_shape=[jax.ShapeDtypeStruct((m, GZ_WIDTH), BF16),
                   jax.ShapeDtypeStruct((m, ZR_WIDTH), F32), kv_shape, kv_shape],
        compiler_params=_cparams(1),
        name="in_proj",
    )(x, g, wg, wr, gains, ones_bd)


def _bucket(n):
    max_exact = REL_BUCKETS // 2
    large = max_exact + (jnp.log(jnp.maximum(n, 1).astype(F32) / max_exact)
                         / math.log(REL_MAX_DIST / max_exact)
                         * (REL_BUCKETS - max_exact)).astype(jnp.int32)
    return jnp.where(n < max_exact, n, jnp.minimum(large, REL_BUCKETS - 1))


def _bias_kernel(tab_ref, bp_ref, bs_ref, *, t_dec):
    h = pl.program_id(0)

    def lookup(n):
        bucket = _bucket(n)
        val = jnp.full(n.shape, tab_ref[h, REL_BUCKETS - 1], F32)
        for b in range(REL_BUCKETS - 1):
            val = jnp.where(bucket == b, tab_ref[h, b], val)
        return val

    tq = bp_ref.shape[2]
    key = lax.broadcasted_iota(jnp.int32, (tq, tq), 0)
    qry = lax.broadcasted_iota(jnp.int32, (tq, tq), 1)
    bp_ref[0, 0] = jnp.where(key <= qry, lookup(jnp.maximum(qry - key, 0)) * LOG2E, NEG_INF)
    bp_ref[0, 1] = lookup(qry - key + tq) * LOG2E

    rows, cols = bs_ref.shape[1], bs_ref.shape[2]
    r = lax.broadcasted_iota(jnp.int32, (rows, cols), 0)
    c = lax.broadcasted_iota(jnp.int32, (rows, cols), 1)
    t = r % t_dec
    tok = c // A_HEADS
    valid = (c % A_HEADS) == h
    far = jnp.full((rows, cols), tab_ref[h, REL_BUCKETS - 1], F32)
    bs_ref[0] = jnp.where(valid, far, NEG_INF)
    bs_ref[1] = jnp.where(valid, lookup(PAGE_SIZE + t - tok), NEG_INF)
    new_ok = valid & (tok <= t) & (tok < t_dec)
    bs_ref[2] = jnp.where(new_ok, lookup(jnp.maximum(t - tok, 0)), NEG_INF)


def _bias_tiles(table, t_dec):
    rows = 2 * t_dec
    return pl.pallas_call(
        functools.partial(_bias_kernel, t_dec=t_dec),
        grid=(A_HEADS,),
        in_specs=[pl.BlockSpec(memory_space=pltpu.SMEM)],
        out_specs=[pl.BlockSpec((1, 2, ATT_TQ, ATT_TQ), lambda h: (h, 0, 0, 0)),
                   pl.BlockSpec((3, rows, PAGE_SIZE * A_HEADS), lambda h: (0, h, 0))],
        out_shape=[jax.ShapeDtypeStruct((A_HEADS, 2, ATT_TQ, ATT_TQ), F32),
                   jax.ShapeDtypeStruct((3, A_HEADS * rows, PAGE_SIZE * A_HEADS), F32)],
        compiler_params=_cparams(1),
        name="bias_tiles",
    )(table.T)


def _lam(lamv_ref, lam_init):
    s1 = jnp.sum(lamv_ref[0:1, :] * lamv_ref[1:2, :], axis=1, keepdims=True)
    s2 = jnp.sum(lamv_ref[2:3, :] * lamv_ref[3:4, :], axis=1, keepdims=True)
    return jnp.exp(s1) - jnp.exp(s2) + lam_init


def _subln(o, g_ref, lam_init):
    y = o * lax.rsqrt(jnp.mean(o * o, axis=-1, keepdims=True) + EPS)
    return y * g_ref[...] * (1.0 - lam_init)


def _attn_prompt_kernel(tab_ref, q_ref, k_ref, v_ref, bias_ref, lamv_ref, subg_ref, o_ref,
                        kb, vt, m_s, acc_s, *, lam_init):
    hg = pl.program_id(1)
    qi = pl.program_id(2)
    tq = q_ref.shape[1]

    n_hd = kb.shape[0]
    d = A_V_DIM
    chains = [(hh, m) for hh in range(n_hd) for m in range(2)]

    @pl.when(qi == 0)
    def _():
        for hh in range(n_hd):
            hcols = slice(hh * 128, (hh + 1) * 128)
            kb[hh] = k_ref[0, :, hcols].astype(BF16)
            for jj in range(vt.shape[1]):
                vt[hh, jj, :d] = jnp.transpose(v_ref[0, jj * tq:(jj + 1) * tq, hcols]).astype(BF16)
                vt[hh, jj, d:] = jnp.ones((vt.shape[2] - d, tq), BF16)

    sub = lax.broadcasted_iota(jnp.int32, (128, tq), 0)
    qm = {}
    for hh in range(n_hd):
        qt = jnp.transpose(q_ref[0, :, hh * 128:(hh + 1) * 128]) * LOG2E
        qm[hh, 0] = jnp.where(sub < A_QK_DIM, qt, 0.0).astype(BF16)
        qm[hh, 1] = jnp.where(sub >= A_QK_DIM, qt, 0.0).astype(BF16)
    m_s[...] = jnp.full(m_s.shape, NEG_INF, F32)
    acc_s[...] = jnp.zeros(acc_s.shape, F32)

    def step(j, bias):
        start = pl.multiple_of(j * tq, tq)
        m_prev = [m_s[c] for c in range(len(chains))]
        ss = [_dot(kb[hh, pl.ds(start, tq), :], qm[hh, m]) + bias[hh] for hh, m in chains]
        m_new = [jnp.maximum(mp, jnp.max(s, axis=0, keepdims=True)) for mp, s in zip(m_prev, ss)]
        ps = [jnp.exp2(s - mn).astype(BF16) for s, mn in zip(ss, m_new)]
        for c, (hh, m) in enumerate(chains):
            acc_s[c] = jnp.exp2(m_prev[c] - m_new[c]) * acc_s[c] + _dot(vt[hh, j], ps[c])
            m_s[c] = m_new[c]

    far = [tab_ref[hg * n_hd + hh, REL_BUCKETS - 1] * LOG2E for hh in range(n_hd)]

    def far_body(j, carry):
        step(j, far)
        return carry

    lax.fori_loop(0, jnp.maximum(qi - 1, 0), far_body, 0)

    @pl.when(qi >= 1)
    def _():
        step(qi - 1, [bias_ref[hh, 1] for hh in range(n_hd)])

    step(qi, [bias_ref[hh, 0] for hh in range(n_hd)])

    lam = _lam(lamv_ref, lam_init)
    for hh in range(n_hd):
        a0, a1 = acc_s[2 * hh], acc_s[2 * hh + 1]
        o = a0[:d] * (1.0 / a0[d:d + 1]) - lam * (a1[:d] * (1.0 / a1[d:d + 1]))
        y = o * lax.rsqrt(jnp.mean(o * o, axis=0, keepdims=True) + EPS)
        y = y * subg_ref[...] * (1.0 - lam_init)
        o_ref[0, :, hh * 128:(hh + 1) * 128] = jnp.transpose(y).astype(o_ref.dtype)


def _attn_prompt(zr3, bias_p, tab_t, lamv, subg, lam_init):
    b, s, _ = zr3.shape
    tq = ATT_TQ
    n_hd = ATT_HEADS_PER_STEP
    wid = 128 * n_hd
    kern = functools.partial(_attn_prompt_kernel, lam_init=lam_init)
    return pl.pallas_call(
        kern,
        grid=(b, A_HEADS // n_hd, s // tq),
        in_specs=[pl.BlockSpec(memory_space=pltpu.SMEM),
                  pl.BlockSpec((1, tq, wid), lambda bi, hg, qi: (bi, qi, COL_AQ // wid + hg)),
                  pl.BlockSpec((1, s, wid), lambda bi, hg, qi: (bi, 0, COL_AK // wid + hg)),
                  pl.BlockSpec((1, s, wid), lambda bi, hg, qi: (bi, 0, COL_AV // wid + hg)),
                  pl.BlockSpec((n_hd, 2, tq, tq), lambda bi, hg, qi: (hg, 0, 0, 0)),
                  pl.BlockSpec((4, A_QK_DIM), lambda bi, hg, qi: (0, 0)),
                  pl.BlockSpec((A_V_DIM, 1), lambda bi, hg, qi: (0, 0))],
        out_specs=pl.BlockSpec((1, tq, wid), lambda bi, hg, qi: (bi, qi, hg)),
        out_shape=jax.ShapeDtypeStruct((b, s, A_WIDTH), BF16),
        scratch_shapes=[pltpu.VMEM((n_hd, s, 128), BF16),
                        pltpu.VMEM((n_hd, s // tq, A_V_DIM + ONES_ROWS, tq), BF16),
                        pltpu.VMEM((2 * n_hd, 1, tq), F32),
                        pltpu.VMEM((2 * n_hd, A_V_DIM + ONES_ROWS, tq), F32)],
        compiler_params=_cparams(3),
        name="attn_prompt",
    )(tab_t, zr3, zr3, zr3, bias_p, lamv, subg.reshape(A_V_DIM, 1))


def _attn_sample_init(q_ref, q_s, m_s, l_s, acc_s, t_dec):
    rows = 2 * t_dec
    q = q_ref[0]
    lane = lax.broadcasted_iota(jnp.int32, (t_dec, 128), 1)
    for h in range(A_HEADS):
        qh = q[:, h * 128:(h + 1) * 128]
        q_s[h * rows:h * rows + t_dec, :] = jnp.where(lane < A_QK_DIM, qh, 0.0)
        q_s[h * rows + t_dec:(h + 1) * rows, :] = jnp.where(lane >= A_QK_DIM, qh, 0.0)
    m_s[...] = jnp.full(m_s.shape, NEG_INF, F32)
    l_s[...] = jnp.zeros(l_s.shape, F32)
    acc_s[...] = jnp.zeros(acc_s.shape, F32)


def _attn_sample_update(q_s, m_s, l_s, acc_s, ks, vs, biases):
    qb = q_s[...].astype(BF16)
    ss = [_dot_nt(qb, kb) + bias for kb, bias in zip(ks, biases)]
    smax = functools.reduce(jnp.maximum, ss)
    m_prev = m_s[...]
    m_new = jnp.maximum(m_prev, jnp.max(smax, axis=1, keepdims=True))
    ps = [jnp.exp(s - m_new) for s in ss]
    alpha = jnp.exp(m_prev - m_new)
    l_s[...] = alpha * l_s[...] + jnp.sum(functools.reduce(jnp.add, ps), axis=1, keepdims=True)
    pv = functools.reduce(jnp.add, [_dot(p.astype(BF16), vb) for p, vb in zip(ps, vs)])
    acc_s[...] = alpha * acc_s[...] + pv
    m_s[...] = m_new


def _attn_sample_final(kn_ref, vn_ref, bias_ref, lamv_ref, subg_ref, o_ref, q_s, m_s, l_s, acc_s,
                       lam_init, t_dec):
    rows = 2 * t_dec
    nk = kn_ref.shape[1]
    _attn_sample_update(q_s, m_s, l_s, acc_s, [kn_ref[0].astype(BF16)], [vn_ref[0].astype(BF16)],
                        [bias_ref[2][:, :nk]])
    lam = _lam(lamv_ref, lam_init)
    on = acc_s[...] * (1.0 / l_s[...])
    for h in range(A_HEADS):
        o = on[h * rows:h * rows + t_dec] - lam * on[h * rows + t_dec:(h + 1) * rows]
        o_ref[0, :, h * 128:(h + 1) * 128] = _subln(o, subg_ref, lam_init).astype(o_ref.dtype)


def _ffn_attn_kernel(pt_ref, x_ref, g_ref, w1_ref, w2_ref, q_ref, kn_ref, vn_ref, bias_ref,
                     lamv_ref, subg_ref, *rest, lam_init, n_pg, t_dec):
    del pt_ref
    k_refs, v_refs = rest[:n_pg], rest[n_pg:2 * n_pg]
    o_ffn, o_att = rest[2 * n_pg:2 * n_pg + 2]
    h_s, acc_s, q_s, m_s, l_s, att_s = rest[2 * n_pg + 2:]
    j = pl.program_id(1)
    last = pl.num_programs(1) - 1

    @pl.when(j == 0)
    def _():
        _attn_sample_init(q_ref, q_s, m_s, l_s, att_s, t_dec)
        _ffn_init(x_ref, g_ref, h_s, acc_s)

    biases = [bias_ref[0]] * (n_pg - 1) + [jnp.where(j == last, bias_ref[1], bias_ref[0])]
    _attn_sample_update(q_s, m_s, l_s, att_s, [r[0, 0].astype(BF16) for r in k_refs],
                        [r[0, 0].astype(BF16) for r in v_refs], biases)
    _ffn_accumulate(w1_ref, w2_ref, h_s, acc_s)

    @pl.when(j == last)
    def _():
        _attn_sample_final(kn_ref, vn_ref, bias_ref, lamv_ref, subg_ref, o_att, q_s, m_s, l_s,
                           att_s, lam_init, t_dec)
        o_ffn[...] = x_ref[...] + acc_s[...]


def _ffn_attn(x, g, w1, w2, zrs3, kn2d, vn2d, cache_k2d, cache_v2d, page_table, bias_s, lamv,
              subg, layer, lam_init):
    m = x.shape[0]
    bd, t_dec, _ = zrs3.shape
    n_pages = page_table.shape[1]
    n_j = FFN_ATTN_STEPS
    assert m % bd == 0 and (m // bd) % 8 == 0 and n_pages % n_j == 0 and D_FF % n_j == 0
    tm, tf, n_pg = m // bd, D_FF // n_j, n_pages // n_j
    rows = 2 * t_dec * A_HEADS
    pcols = PAGE_SIZE * A_HEADS

    def page_spec(p):
        return pl.BlockSpec((1, 1, pcols, 128),
                            lambda i, j, pt: (layer, pt[i, j * n_pg + p], 0, 0))

    def per_seq(arr):
        return pl.BlockSpec((1,) + arr.shape[1:], lambda i, j, pt: (i, 0, 0))

    def const(shape):
        return pl.BlockSpec(shape, lambda i, j, pt: (0,) * len(shape))

    kern = functools.partial(_ffn_attn_kernel, lam_init=lam_init, n_pg=n_pg, t_dec=t_dec)
    grid_spec = pltpu.PrefetchScalarGridSpec(
        num_scalar_prefetch=1,
        grid=(bd, n_j),
        in_specs=[pl.BlockSpec((tm, D_MODEL), lambda i, j, pt: (i, 0)),
                  const((1, D_MODEL)),
                  pl.BlockSpec((D_MODEL, tf), lambda i, j, pt: (0, j)),
                  pl.BlockSpec((tf, D_MODEL), lambda i, j, pt: (j, 0)),
                  pl.BlockSpec((1, t_dec, A_WIDTH), lambda i, j, pt: (i, 0, 0)),
                  per_seq(kn2d), per_seq(vn2d),
                  const((3, rows, pcols)), const((4, A_QK_DIM)), const((1, A_V_DIM))]
                 + [page_spec(p) for p in range(n_pg)] * 2,
        out_specs=[pl.BlockSpec((tm, D_MODEL), lambda i, j, pt: (i, 0)),
                   pl.BlockSpec((1, t_dec, A_WIDTH), lambda i, j, pt: (i, 0, 0))],
        scratch_shapes=[pltpu.VMEM((tm, D_MODEL), BF16), pltpu.VMEM((tm, D_MODEL), F32),
                        pltpu.VMEM((rows, 128), F32), pltpu.VMEM((rows, 1), F32),
                        pltpu.VMEM((rows, 1), F32), pltpu.VMEM((rows, 128), F32)],
    )
    return pl.pallas_call(
        kern,
        grid_spec=grid_spec,
        out_shape=[jax.ShapeDtypeStruct((m, D_MODEL), F32),
                   jax.ShapeDtypeStruct((bd, t_dec, A_WIDTH), BF16)],
        compiler_params=_cparams(2),
        name="ffn_attn",
    )(page_table, x, g.reshape(1, D_MODEL), w1, w2, zrs3, kn2d, vn2d, bias_s, lamv, subg,
      *([cache_k2d] * n_pg), *([cache_v2d] * n_pg))


def _gelu(x):
    return 0.5 * x * (1.0 + jnp.tanh(math.sqrt(2.0 / math.pi) * (x + 0.044715 * (x * x * x))))


def _chunk_mlp_kernel(u_ref, v_ref, w_ref, bias_ref, g_ref, b_ref, y_ref, vb_ref):
    n_chunks = u_ref.shape[0] // B_CHUNK
    lane_grp = lax.broadcasted_iota(jnp.int32, (B_CHUNK, B_WIDTH), 1) // (B_WIDTH // B_GROUPS)
    for c in range(n_chunks):
        sl = slice(c * B_CHUNK, (c + 1) * B_CHUNK)
        gv = _gelu(v_ref[sl, :])
        xc = gv - jnp.mean(gv, axis=-1, keepdims=True)
        vb = xc * lax.rsqrt(jnp.mean(xc * xc, axis=-1, keepdims=True) + EPS) * g_ref[...] + b_ref[...]
        vb_ref[sl, :] = vb
        vbb = vb.astype(BF16)
        mixed = bias_ref[...]
        for g in range(B_GROUPS):
            mixed = mixed + jnp.where(lane_grp == g, _dot(w_ref[g], vbb), 0.0)
        y_ref[sl, :] = (_gelu(u_ref[sl, :]) * mixed).astype(y_ref.dtype)


def _chunk_mlp(zr, w_eff, bias_eff, ln_g, ln_b):
    m = zr.shape[0]
    tm = min(m, CHUNK_MLP_TM)
    return pl.pallas_call(
        _chunk_mlp_kernel,
        grid=(m // tm,),
        in_specs=[pl.BlockSpec((tm, B_WIDTH), lambda i: (i, COL_BU // B_WIDTH)),
                  pl.BlockSpec((tm, B_WIDTH), lambda i: (i, COL_BV // B_WIDTH)),
                  pl.BlockSpec((B_GROUPS, B_CHUNK, B_CHUNK), lambda i: (0, 0, 0)),
                  pl.BlockSpec((B_CHUNK, B_WIDTH), lambda i: (0, 0)),
                  pl.BlockSpec((1, B_WIDTH), lambda i: (0, 0)),
                  pl.BlockSpec((1, B_WIDTH), lambda i: (0, 0))],
        out_specs=[pl.BlockSpec((tm, B_WIDTH), lambda i: (i, 0)),
                   pl.BlockSpec((tm, B_WIDTH), lambda i: (i, 0))],
        out_shape=[jax.ShapeDtypeStruct((m, B_WIDTH), BF16),
                   jax.ShapeDtypeStruct((m, B_WIDTH), F32)],
        compiler_params=_cparams(1),
        name="chunk_mlp",
    )(zr, zr, w_eff, bias_eff, ln_g, ln_b)


def _pool_kernel(x_ref, w_ref, scale_ref, y_ref, *, prefix, pos0, seq_rows):
    x = x_ref[0]
    row = lax.broadcasted_iota(jnp.int32, x.shape, 0) % seq_rows
    grp = lax.broadcasted_iota(jnp.int32, x.shape, 1) // C_GROUP_DIM

    def shifted(a, k):
        return jnp.where(row >= k, pltpu.roll(a, k, 0), 0.0)

    sums = []
    acc = x
    for k in (1, 2, 4, 8):
        acc = acc + shifted(acc, k)
        sums.append(acc)
    total = sums[3]
    win = jnp.full(x.shape, POOL_WINDOWS[3], jnp.int32)
    for g in range(3):
        total = jnp.where(grp == g, sums[g], total)
        win = jnp.where(grp == g, POOL_WINDOWS[g], win)
    pos = pos0 + row - prefix
    cnt = jnp.clip(pos + 1, 1, win).astype(F32)
    d = total / cnt - x
    y = _dot(d.astype(BF16), w_ref[...]) * scale_ref[...]
    y_ref[0] = y.astype(y_ref.dtype)


def _pool_mix(xx, col_block, w_bd, scale, prefix, pos0, seq_rows):
    b, rows, _ = xx.shape
    kern = functools.partial(_pool_kernel, prefix=prefix, pos0=pos0, seq_rows=seq_rows)
    return pl.pallas_call(
        kern,
        grid=(b,),
        in_specs=[pl.BlockSpec((1, rows, C_WIDTH), lambda i: (i, 0, col_block)),
                  pl.BlockSpec((C_WIDTH, C_WIDTH), lambda i: (0, 0)),
                  pl.BlockSpec((1, C_WIDTH), lambda i: (0, 0))],
        out_specs=pl.BlockSpec((1, rows, C_WIDTH), lambda i: (i, 0, 0)),
        out_shape=jax.ShapeDtypeStruct((b, rows, C_WIDTH), BF16),
        compiler_params=_cparams(1),
        name="pool_mix",
    )(xx, w_bd, scale)


def _log_sigmoid(x):
    return jnp.minimum(x, 0.0) - jnp.log1p(jnp.exp(-jnp.abs(x)))


def _split3(x):
    def top8(a):
        bits = lax.bitcast_convert_type(a, jnp.int32) & jnp.int32(-65536)
        return lax.bitcast_convert_type(bits, F32)

    p1 = top8(x)
    r1 = x - p1
    p2 = top8(r1)
    return p1, p2, r1 - p2


def _pack3(pieces):
    return (pieces[0] + pltpu.roll(pieces[1], D_HEADS, 1) + pltpu.roll(pieces[2], 2 * D_HEADS, 1))


def _mlstm_kernel(q_ref, k_ref, v_ref, o_ref, gi_ref, gf_ref, gb_ref, ng_ref, sel_ref, bd_ref,
                  c0_ref, n0_ref, m0_ref, y_ref, c_out, n_out, m_out, c_s, n_s, m_s, *, t_valid):
    ci = pl.program_id(1)
    nb, chunk = q_ref.shape[0], q_ref.shape[1]
    hd = D_HEAD_DIM
    eye_h = (lax.broadcasted_iota(jnp.int32, (hd, hd), 0)
             == lax.broadcasted_iota(jnp.int32, (hd, hd), 1))

    @pl.when(ci == 0)
    def _():
        c_s[...] = jnp.zeros(c_s.shape, F32)
        n_s[...] = jnp.zeros(n_s.shape, F32)
        m_s[...] = m0_ref[...]
        for bi in range(nb):
            for h in range(D_HEADS):
                hs = slice(h * hd, (h + 1) * hd)
                c_s[bi, hs, hs] = c0_ref[bi, h]
                n_col = jnp.sum(jnp.where(eye_h, n0_ref[bi, h], 0.0), axis=1, keepdims=True)
                n_s[bi, hs, hs] = jnp.broadcast_to(n_col, (hd, hd))

    row = lax.broadcasted_iota(jnp.int32, (chunk, 128), 0)
    lane = lax.broadcasted_iota(jnp.int32, (chunk, 128), 1)
    head_lane = lane < D_HEADS
    grp = lax.broadcasted_iota(jnp.int32, (chunk, D_WIDTH), 1) // hd
    rr = lax.broadcasted_iota(jnp.int32, (chunk, chunk), 0)
    cc = lax.broadcasted_iota(jnp.int32, (chunk, chunk), 1)
    allowed = (cc <= rr) & (cc < t_valid)
    ones_w = jnp.ones((chunk, D_WIDTH), BF16)
    same_head = (lax.broadcasted_iota(jnp.int32, (D_WIDTH, D_WIDTH), 0) // hd
                 == lax.broadcasted_iota(jnp.int32, (D_WIDTH, D_WIDTH), 1) // hd)
    pick = [(lane % D_HEADS == h) & (lane < 6 * D_HEADS) for h in range(D_HEADS)]
    in_head = [grp == h for h in range(D_HEADS)]
    lower_ones = jnp.where(lane < 3 * D_HEADS, 1.0, 0.0)
    upper_ones = jnp.where((lane >= 3 * D_HEADS) & (lane < 6 * D_HEADS), 1.0, 0.0)
    head_one = [jnp.where(ih, 1.0, 0.0).astype(BF16) for ih in in_head]

    for bi in range(nb):
        gi = gi_ref[bi] + gb_ref[0:1, :]
        lf = _log_sigmoid(gf_ref[bi] + gb_ref[1:2, :])
        if t_valid < chunk:
            gi = jnp.where(row < t_valid, gi, NEG_INF)
            lf = jnp.where(row < t_valid, lf, 0.0)
        b = lf
        k = 1
        while k < chunk:
            b = b + jnp.where(row >= k, pltpu.roll(b, k, 0), 0.0)
            k *= 2
        u = gi - b
        cm = u
        k = 1
        while k < chunk:
            cm = jnp.maximum(cm, jnp.where(row >= k, pltpu.roll(cm, k, 0), NEG_INF))
            k *= 2
        m_prev = m_s[bi]
        big_m = jnp.maximum(m_prev, cm)
        m_last = big_m[chunk - 1:chunk, :]
        winter = jnp.exp(m_prev - big_m)
        emt = jnp.exp(jnp.minimum(-(b + big_m), EMT_CAP))
        ws = jnp.exp(u - m_last)
        m_s[bi] = b[chunk - 1:chunk, :] + m_last

        def per_head_lanes(z):
            packed = _pack3(_split3(jnp.where(head_lane, z, 0.0)))
            return _dot(packed.astype(BF16), sel_ref[...])

        winter_r, emt_r, ws_r = per_head_lanes(winter), per_head_lanes(emt), per_head_lanes(ws)
        decay_r = winter_r[chunk - 1:chunk, :]

        q = q_ref[bi]
        qb = q.astype(BF16)
        kf = k_ref[bi] * (hd ** -0.5)
        kb = kf.astype(BF16)
        vf = v_ref[bi]
        vb = vf.astype(BF16)
        u_fin = jnp.where(head_lane & (row < t_valid), u, 0.0)
        y_side = (_pack3(_split3(u_fin)) + upper_ones).astype(BF16)
        x_all = lower_ones + pltpu.roll(_pack3(_split3(jnp.where(head_lane, -big_m, 0.0))),
                                        3 * D_HEADS, 1)
        acc = jnp.zeros((chunk, 2 * D_WIDTH), F32)
        for h in range(D_HEADS):
            x_side = jnp.where(pick[h], x_all, 0.0).astype(BF16)
            expo = jnp.where(allowed, _dot_nt(x_side, y_side), NEG_INF)
            qk = _dot_nt(jnp.where(in_head[h], q, 0.0).astype(BF16), kb) * jnp.exp(expo)
            rhs = jnp.concatenate([jnp.where(in_head[h], vf, 0.0).astype(BF16), head_one[h]], axis=1)
            acc = acc + _dot(qk.astype(BF16), rhs)
        state = jnp.concatenate([c_s[bi].astype(BF16), n_s[bi].astype(BF16)], axis=1)
        inter = _dot(qb, state)
        num = winter_r * inter[:, :D_WIDTH] + acc[:, :D_WIDTH]
        den = winter_r * inter[:, D_WIDTH:] + acc[:, D_WIDTH:]
        hh = num / jnp.maximum(jnp.abs(den), emt_r)
        h2 = hh * hh
        hi = h2.astype(BF16)
        lo = (h2 - hi.astype(F32)).astype(BF16)
        ssq = _dot(hi, bd_ref[...]) + _dot(lo, bd_ref[...])
        y = hh * lax.rsqrt(ssq * (1.0 / hd) + EPS) * ng_ref[...]
        y_ref[bi] = (y * jax.nn.sigmoid(o_ref[bi])).astype(y_ref.dtype)
        kw = (ws_r * kf).astype(BF16)
        upd = _dot_tn(kw, jnp.concatenate([vb, ones_w], axis=1))
        c_s[bi] = decay_r * c_s[bi] + jnp.where(same_head, upd[:, :D_WIDTH], 0.0)
        n_s[bi] = decay_r * n_s[bi] + jnp.where(same_head, upd[:, D_WIDTH:], 0.0)

    @pl.when(ci == pl.num_programs(1) - 1)
    def _():
        m_out[...] = m_s[...]
        for bi in range(nb):
            for h in range(D_HEADS):
                hs = slice(h * hd, (h + 1) * hd)
                c_out[bi, h] = c_s[bi, hs, hs]
                n_out[bi, h] = jnp.sum(jnp.where(eye_h, n_s[bi, hs, hs], 0.0), axis=0, keepdims=True)


def _mlstm(src, col0, gate_block, chunk, t_valid, gate_bias, norm_g, c0, n0, m0):
    b, t, _ = src.shape
    nb = math.gcd(b, MLSTM_BATCH)
    hd = D_HEAD_DIM
    kern = functools.partial(_mlstm_kernel, t_valid=t_valid)
    head_of_lane = jnp.arange(D_WIDTH) // hd
    src_lane = jnp.arange(128)[:, None]
    sel3 = ((src_lane % D_HEADS == head_of_lane[None, :]) & (src_lane < 3 * D_HEADS)).astype(BF16)
    same_head = (head_of_lane[:, None] == head_of_lane[None, :]).astype(BF16)
    m0p = jnp.pad(m0, ((0, 0), (0, 0), (0, 128 - D_HEADS)))

    def col(cb, width=D_WIDTH):
        return pl.BlockSpec((nb, chunk, width), lambda bi, ci: (bi, ci, cb))

    def const(shape):
        return pl.BlockSpec(shape, lambda bi, ci: (0,) * len(shape))

    def per_seq(shape):
        return pl.BlockSpec((nb,) + shape, lambda bi, ci: (bi,) + (0,) * len(shape))

    y, c_new, n_new, m_new = pl.pallas_call(
        kern,
        grid=(b // nb, t // chunk),
        in_specs=[col(col0), col(col0 + 1), col(col0 + 2), col(col0 + 3),
                  col(gate_block, 128), col(gate_block + 1, 128),
                  const((2, 128)), const((1, D_WIDTH)), const((128, D_WIDTH)),
                  const((D_WIDTH, D_WIDTH)),
                  per_seq((D_HEADS, hd, hd)), per_seq((D_HEADS, 1, hd)), per_seq((1, 128))],
        out_specs=[col(0), per_seq((D_HEADS, hd, hd)), per_seq((D_HEADS, 1, hd)),
                   per_seq((1, 128))],
        out_shape=[jax.ShapeDtypeStruct((b, t, D_WIDTH), BF16),
                   jax.ShapeDtypeStruct((b, D_HEADS, hd, hd), F32),
                   jax.ShapeDtypeStruct((b, D_HEADS, 1, hd), F32),
                   jax.ShapeDtypeStruct((b, 1, 128), F32)],
        scratch_shapes=[pltpu.VMEM((nb, D_WIDTH, D_WIDTH), F32),
                        pltpu.VMEM((nb, D_WIDTH, D_WIDTH), F32),
                        pltpu.VMEM((nb, 1, 128), F32)],
        compiler_params=_cparams(2),
        name="mlstm",
    )(src, src, src, src, src, src, gate_bias, jnp.tile(norm_g, (1, D_HEADS)), sel3, same_head,
      c0, n0, m0p)
    return y, c_new, n_new, m_new[:, :, :D_HEADS]


def _merge_kernel(x_ref, g_ref, ya_ref, yb_ref, yc_ref, yd_ref, wa_ref, wb_ref, wc_ref, wd_ref,
                  wo_ref, o_ref):
    def gate(i):
        return g_ref[:, i * D_MODEL:(i + 1) * D_MODEL].astype(F32)

    merged = gate(0) * _dot(ya_ref[...], wa_ref[...])
    merged = merged + gate(1) * _dot(yb_ref[...], wb_ref[...])
    merged = merged + gate(2) * _dot(yc_ref[...], wc_ref[...])
    merged = merged + gate(3) * _dot(yd_ref[...], wd_ref[...])
    o_ref[...] = x_ref[...] + _dot(merged.astype(BF16), wo_ref[...])


def _merge(x, gates, ya, yb, yc, yd, wa, wb, wc, wd, wo):
    m = x.shape[0]
    tm = min(m, 512)

    def rows(width):
        return pl.BlockSpec((tm, width), lambda i: (i, 0))

    def full(arr):
        return pl.BlockSpec(arr.shape, lambda i: (0, 0))

    return pl.pallas_call(
        _merge_kernel,
        grid=(m // tm,),
        in_specs=[rows(D_MODEL), rows(GZ_WIDTH), rows(A_WIDTH), rows(B_WIDTH), rows(C_WIDTH),
                  rows(D_WIDTH), full(wa), full(wb), full(wc), full(wd), full(wo)],
        out_specs=rows(D_MODEL),
        out_shape=jax.ShapeDtypeStruct((m, D_MODEL), F32),
        compiler_params=_cparams(1),
        name="merge",
    )(x, gates, ya, yb, yc, yd, wa, wb, wc, wd, wo)


def _ffn_init(x_ref, g_ref, h_s, acc_s):
    x = x_ref[...]
    y = x * lax.rsqrt(jnp.mean(x * x, axis=-1, keepdims=True) + EPS)
    h_s[...] = (y * g_ref[...]).astype(BF16)
    acc_s[...] = jnp.zeros(acc_s.shape, F32)


def _ffn_accumulate(w1_ref, w2_ref, h_s, acc_s):
    a = jnp.maximum(_dot(h_s[...], w1_ref[...]), 0.0)
    acc_s[...] += _dot((a * a).astype(BF16), w2_ref[...])


def _ffn_kernel(x_ref, g_ref, w1_ref, w2_ref, o_ref, h_s, acc_s):
    j = pl.program_id(1)

    @pl.when(j == 0)
    def _():
        _ffn_init(x_ref, g_ref, h_s, acc_s)

    _ffn_accumulate(w1_ref, w2_ref, h_s, acc_s)

    @pl.when(j == pl.num_programs(1) - 1)
    def _():
        o_ref[...] = x_ref[...] + acc_s[...]


def _ffn(x, g, w1, w2):
    m = x.shape[0]
    tm, tf = min(m, 1024), 1024
    return pl.pallas_call(
        _ffn_kernel,
        grid=(m // tm, D_FF // tf),
        in_specs=[pl.BlockSpec((tm, D_MODEL), lambda i, j: (i, 0)),
                  pl.BlockSpec((1, D_MODEL), lambda i, j: (0, 0)),
                  pl.BlockSpec((D_MODEL, tf), lambda i, j: (0, j)),
                  pl.BlockSpec((tf, D_MODEL), lambda i, j: (j, 0))],
        out_specs=pl.BlockSpec((tm, D_MODEL), lambda i, j: (i, 0)),
        out_shape=jax.ShapeDtypeStruct((m, D_MODEL), F32),
        scratch_shapes=[pltpu.VMEM((tm, D_MODEL), BF16), pltpu.VMEM((tm, D_MODEL), F32)],
        compiler_params=_cparams(2),
        name="ffn",
    )(x, g.reshape(1, D_MODEL), w1, w2)


def _layer_weights(p, l):
    w_in = p["w_in"][l]
    gate_pad = jnp.zeros((D_MODEL, 128 - D_HEADS), F32)
    wr = jnp.concatenate(
        [w_in[:, :3072], w_in[:, 3080:3336], w_in[:, 3072:3076], gate_pad,
         w_in[:, 3076:3080], gate_pad], axis=1).astype(BF16)
    wg = w_in[:, 3336:].astype(BF16)
    reps = A_WIDTH // A_QK_DIM
    gains = jnp.stack([jnp.tile(p["q_norm_g"][l], reps) * (A_QK_DIM ** -0.5),
                       jnp.tile(p["k_norm_g"][l], reps)]).reshape(2, 1, A_WIDTH)
    grp = jnp.arange(A_WIDTH) // A_QK_DIM
    ones_bd = (grp[:, None] == grp[None, :]).astype(BF16)
    lamv = jnp.stack([p["lam_q1"][l], p["lam_k1"][l], p["lam_q2"][l], p["lam_k2"][l]])
    gate_bias = jnp.pad(jnp.stack([p["d_i_bias"][l], p["d_f_bias"][l]]),
                        ((0, 0), (0, 128 - D_HEADS)))
    c_bd = jnp.zeros((C_WIDTH, C_WIDTH), F32)
    for g in range(4):
        sl = slice(g * C_GROUP_DIM, (g + 1) * C_GROUP_DIM)
        c_bd = c_bd.at[sl, sl].set(p["c_lin"][l][g])
    return dict(
        norm1_g=p["norm1_g"][l], norm2_g=p["norm2_g"][l], wr=wr, wg=wg, gains=gains,
        ones_bd=ones_bd, lamv=lamv, subg=p["subln_g"][l].reshape(1, A_V_DIM),
        b_ln_g=p["b_ln_g"][l].reshape(1, B_WIDTH), b_ln_b=p["b_ln_b"][l].reshape(1, B_WIDTH),
        b_ws=p["b_ws"][l], b_bias=p["b_bias"][l],
        c_bd=c_bd.astype(BF16), c_scale=p["c_scale"][l].reshape(1, C_WIDTH),
        gate_bias=gate_bias, d_norm_g=p["d_norm_g"][l].reshape(1, D_HEAD_DIM),
        w_pa=p["w_pa"][l].astype(BF16), w_pb=p["w_pb"][l].astype(BF16),
        w_pc=p["w_pc"][l].astype(BF16), w_pd=p["w_pd"][l].astype(BF16),
        w_out=p["w_out"][l].astype(BF16), w_ff1=p["w_ff1"][l].astype(BF16),
        w_ff2=p["w_ff2"][l].astype(BF16),
        lam_init=0.8 - 0.6 * math.exp(-0.3 * l), layer=l,
    )


def _chunk_weights(w, t):
    length = min(t, B_CHUNK)
    ws = jnp.tril(w["b_ws"][:, :length, :length])
    bias = jnp.transpose(w["b_bias"][:, :length])
    reps = B_CHUNK // length
    if reps > 1:
        eye = jnp.eye(reps, dtype=F32)
        ws = jax.vmap(lambda a: jnp.kron(eye, a))(ws)
        bias = jnp.tile(bias, (reps, 1))
    return ws.astype(BF16), jnp.repeat(bias, B_WIDTH // B_GROUPS, axis=1)


def _mix_and_merge(x2, w, gates, zr, ya, yc, yd, t):
    w_eff, bias_eff = _chunk_weights(w, t)
    yb, vb = _chunk_mlp(zr, w_eff, bias_eff, w["b_ln_g"], w["b_ln_b"])
    x2 = _merge(x2, gates, ya, yb, yc, yd, w["w_pa"], w["w_pb"], w["w_pc"], w["w_pd"], w["w_out"])
    return x2, vb


def _prompt_layer(x2, w, b, s, bias_p, tab_t, sample_attn):
    m = b * s
    gates, zr, k_rows, v_rows = _in_proj(x2, w["norm1_g"], w["wg"], w["wr"], w["gains"],
                                         w["ones_bd"])
    zr3 = zr.reshape(b, s, ZR_WIDTH)
    ya = _attn_prompt(zr3, bias_p, tab_t, w["lamv"], w["subg"], w["lam_init"]).reshape(m, A_WIDTH)
    yc = _pool_mix(zr3, COL_CX // C_WIDTH, w["c_bd"], w["c_scale"], 0, 0, s).reshape(m, C_WIDTH)
    hd = D_HEAD_DIM
    yd, c_new, n_new, m_new = _mlstm(
        zr3, COL_DQ // D_WIDTH, COL_DGI // 128, min(s, MLSTM_CHUNK), min(s, MLSTM_CHUNK),
        w["gate_bias"],
        w["d_norm_g"], jnp.zeros((b, D_HEADS, hd, hd), F32), jnp.zeros((b, D_HEADS, 1, hd), F32),
        jnp.zeros((b, 1, D_HEADS), F32))
    x2, _ = _mix_and_merge(x2, w, gates, zr, ya, yc, yd.reshape(m, D_WIDTH), s)
    x2, ya_sample = _ffn_attn(x2, w["norm2_g"], w["w_ff1"], w["w_ff2"], *sample_attn,
                              w["lamv"], w["subg"], w["layer"], w["lam_init"])
    outs = (k_rows.reshape(b, s, A_HEADS, 2 * A_QK_DIM), v_rows.reshape(b, s, A_HEADS, A_V_DIM),
            zr3[:, s - POOL_BUF:, COL_CX:COL_CX + C_WIDTH],
            c_new, n_new.reshape(b, D_HEADS, hd), m_new.reshape(b, D_HEADS))
    return x2, outs, ya_sample


def _sample_layer(x2, w, bd, t, proj, ya, past, pool0, c0, n0, m0):
    m = bd * t
    gates, zr, kn, vn = proj
    zr3 = zr.reshape(bd, t, ZR_WIDTH)
    ya = ya.reshape(m, A_WIDTH)
    cx = zr3[:, :, COL_CX:COL_CX + C_WIDTH]
    prefix = POOL_BUF + 1
    rows = -(-(prefix + t) // 8) * 8
    xx = jnp.concatenate([jnp.zeros((bd, 1, C_WIDTH), F32), pool0, cx,
                          jnp.zeros((bd, rows - prefix - t, C_WIDTH), F32)], axis=1)
    yc = _pool_mix(xx.reshape(1, bd * rows, C_WIDTH), 0, w["c_bd"], w["c_scale"], prefix, past, rows)
    yc = yc.reshape(bd, rows, C_WIDTH)[:, prefix:prefix + t].reshape(m, C_WIDTH)
    chunk = -(-t // SAMPLE_MLSTM_CHUNK) * SAMPLE_MLSTM_CHUNK
    dsrc = jnp.pad(zr3[:, :, COL_DQ:COL_DGF + 128], ((0, 0), (0, chunk - t), (0, 0)))
    hd = D_HEAD_DIM
    yd, c_new, n_new, m_new = _mlstm(
        dsrc, 0, (COL_DGI - COL_DQ) // 128, chunk, t, w["gate_bias"], w["d_norm_g"],
        c0, n0.reshape(bd, D_HEADS, 1, hd), m0.reshape(bd, 1, D_HEADS))
    yd = yd[:, :t].reshape(m, D_WIDTH)
    x2, vb = _mix_and_merge(x2, w, gates, zr, ya, yc, yd, t)
    x2 = _ffn(x2, w["norm2_g"], w["w_ff1"], w["w_ff2"])
    outs = (kn.reshape(bd, t, A_HEADS, 2 * A_QK_DIM), vn.reshape(bd, t, A_HEADS, A_V_DIM),
            vb.reshape(bd, t, B_WIDTH), jnp.concatenate([pool0, cx], axis=1)[:, -POOL_BUF:],
            c_new, n_new.reshape(bd, D_HEADS, hd), m_new.reshape(bd, D_HEADS))
    return x2, outs


def kernel(x_prompt, x_sample, cache_k, cache_v, page_table, state_pool, state_C, state_n, state_m, rel_bias, norm1_g, norm2_g, w_in, q_norm_g, k_norm_g, lam_q1, lam_k1, lam_q2, lam_k2, subln_g, b_ln_g, b_ln_b, b_ws, b_bias, c_lin, c_scale, d_i_bias, d_f_bias, d_norm_g, w_pa, w_pb, w_pc, w_pd, w_out, w_ff1, w_ff2):
    p = dict(norm1_g=norm1_g, norm2_g=norm2_g, w_in=w_in, q_norm_g=q_norm_g, k_norm_g=k_norm_g,
             lam_q1=lam_q1, lam_k1=lam_k1, lam_q2=lam_q2, lam_k2=lam_k2, subln_g=subln_g,
             b_ln_g=b_ln_g, b_ln_b=b_ln_b, b_ws=b_ws, b_bias=b_bias, c_lin=c_lin, c_scale=c_scale,
             d_i_bias=d_i_bias, d_f_bias=d_f_bias, d_norm_g=d_norm_g, w_pa=w_pa, w_pb=w_pb,
             w_pc=w_pc, w_pd=w_pd, w_out=w_out, w_ff1=w_ff1, w_ff2=w_ff2)
    depth = w_in.shape[0]
    bp, sp, _ = x_prompt.shape
    bd, td, _ = x_sample.shape
    n_phys = cache_k.shape[1]
    cache_k2d = cache_k.reshape(depth, n_phys, PAGE_SIZE * A_HEADS, 128)
    cache_v2d = cache_v.reshape(depth, n_phys, PAGE_SIZE * A_HEADS, 128)
    bias_p, bias_s = _bias_tiles(rel_bias, td)
    tab_t = rel_bias.T

    xp = x_prompt.reshape(bp * sp, D_MODEL)
    xs = x_sample.reshape(bd * td, D_MODEL)
    prompt_outs, sample_outs = [], []
    for l in range(depth):
        w = _layer_weights(p, l)
        proj = _in_proj(xs, w["norm1_g"], w["wg"], w["wr"], w["gains"], w["ones_bd"])
        sample_attn = (proj[1].reshape(bd, td, ZR_WIDTH),
                       proj[2].reshape(bd, td * A_HEADS, 128), proj[3].reshape(bd, td * A_HEADS, 128),
                       cache_k2d, cache_v2d, page_table, bias_s)
        xp, po, ya_sample = _prompt_layer(xp, w, bp, sp, bias_p, tab_t, sample_attn)
        xs, so = _sample_layer(xs, w, bd, td, proj, ya_sample, page_table.shape[1] * PAGE_SIZE,
                               state_pool[l], state_C[l], state_n[l], state_m[l])
        prompt_outs.append(po)
        sample_outs.append(so)

    def stack(outs, i):
        return jnp.stack([o[i] for o in outs])

    return (xp.reshape(bp, sp, D_MODEL), xs.reshape(bd, td, D_MODEL),
            stack(prompt_outs, 0), stack(prompt_outs, 1), stack(sample_outs, 0), stack(sample_outs, 1),
            stack(sample_outs, 2), stack(prompt_outs, 2), stack(sample_outs, 3),
            stack(prompt_outs, 3), stack(prompt_outs, 4), stack(prompt_outs, 5),
            stack(sample_outs, 4), stack(sample_outs, 5), stack(sample_outs, 6))
```

```python
import functools
import math

import jax
import jax.numpy as jnp
from jax import lax
from jax.experimental import pallas as pl
from jax.experimental.pallas import tpu as pltpu

F32 = jnp.float32
BF16 = jnp.bfloat16
NEG_INF = float("-inf")

D_MODEL = 1024
A_HEADS = 4
A_QK_DIM = 64
A_V_DIM = 128
A_WIDTH = 512
REL_BUCKETS = 32
REL_MAX_DIST = 128
PAGE_SIZE = 128
B_GROUPS = 4
B_WIDTH = 256
B_CHUNK = 128
C_WIDTH = 256
C_GROUP_DIM = 64
POOL_WINDOWS = (2, 4, 8, 16)
POOL_BUF = 15
D_HEADS = 4
D_WIDTH = 256
D_HEAD_DIM = 64
D_FF = 4096
N_BRANCH = 4
EPS = 1e-6

ZR_WIDTH = 3584
COL_AQ, COL_AK, COL_AV = 0, 512, 1024
COL_BU, COL_BV, COL_CX = 1536, 1792, 2048
COL_DQ, COL_DK, COL_DV, COL_DO = 2304, 2560, 2816, 3072
COL_DGI, COL_DGF = 3328, 3456
GZ_WIDTH = N_BRANCH * D_MODEL
EMT_CAP = 80.0
LOG2E = math.log2(math.e)
ONES_ROWS = 16

IN_PROJ_TM = 512
IN_PROJ_TN = 512
CHUNK_MLP_TM = 1024
ATT_TQ = 512
ATT_HEADS_PER_STEP = 2
FFN_ATTN_STEPS = 4
MLSTM_BATCH = 4
MLSTM_CHUNK = 256
SAMPLE_MLSTM_CHUNK = 32
VMEM_LIMIT = 56 * 1024 * 1024


def _cparams(n_axes):
    return pltpu.CompilerParams(dimension_semantics=("arbitrary",) * n_axes,
                                vmem_limit_bytes=VMEM_LIMIT)


def _dot(a, b):
    return jnp.dot(a, b, preferred_element_type=F32)


def _dot_nt(a, b):
    return lax.dot_general(a, b, (((1,), (1,)), ((), ())), preferred_element_type=F32)


def _dot_tn(a, b):
    return lax.dot_general(a, b, (((0,), (0,)), ((), ())), preferred_element_type=F32)


def _in_proj_kernel(x_ref, g_ref, wg_ref, wr_ref, gain_ref, ones_ref,
                    gates_ref, zr_ref, kout_ref, vout_ref):
    tm = x_ref.shape[0]
    tn = IN_PROJ_TN
    x = x_ref[...]
    h = (x * lax.rsqrt(jnp.mean(x * x, axis=-1, keepdims=True) + EPS) * g_ref[...]).astype(BF16)

    def head_rows(dst_ref, val):
        for hh in range(A_HEADS):
            dst_ref[pl.ds(hh, tm, stride=A_HEADS), :] = val[:, hh * 128:(hh + 1) * 128]

    for c in range(ZR_WIDTH // tn):
        cols = slice(c * tn, (c + 1) * tn)
        z = _dot(h, wr_ref[:, cols])
        if c * tn in (COL_AQ, COL_AK):
            z2 = z * z
            hi = z2.astype(BF16)
            lo = (z2 - hi.astype(F32)).astype(BF16)
            ssq = _dot(hi, ones_ref[...]) + _dot(lo, ones_ref[...])
            z = z * lax.rsqrt(ssq * (1.0 / A_QK_DIM) + EPS) * gain_ref[c]
        zr_ref[:, cols] = z
        if c * tn == COL_AK:
            head_rows(kout_ref, z)
        if c * tn == COL_AV:
            head_rows(vout_ref, z)

    for c in range(GZ_WIDTH // tn):
        cols = slice(c * tn, (c + 1) * tn)
        gates_ref[:, cols] = jax.nn.sigmoid(_dot(h, wg_ref[:, cols])).astype(gates_ref.dtype)


def _in_proj(x, g, wg, wr, gains, ones_bd):
    m = x.shape[0]
    tm = min(m, IN_PROJ_TM)

    def rows(width):
        return pl.BlockSpec((tm, width), lambda i: (i, 0))

    def resident(arr):
        return pl.BlockSpec(arr.shape, lambda i: (0,) * arr.ndim, pipeline_mode=pl.Buffered(1))

    kv_spec = pl.BlockSpec((tm * A_HEADS, 128), lambda i: (i, 0))
    kv_shape = jax.ShapeDtypeStruct((m * A_HEADS, 128), F32)
    g = g.reshape(1, D_MODEL)
    return pl.pallas_call(
        _in_proj_kernel,
        grid=(m // tm,),
        in_specs=[rows(D_MODEL), resident(g), resident(wg), resident(wr), resident(gains),
                  resident(ones_bd)],
        out_specs=[rows(GZ_WIDTH), rows(ZR_WIDTH), kv_spec, kv_spec],
        out_shape=[jax.ShapeDtypeStruct((m, GZ_WIDTH), BF16),
                   jax.ShapeDtypeStruct((m, ZR_WIDTH), F32), kv_shape, kv_shape],
        compiler_params=_cparams(1),
        name="in_proj",
    )(x, g, wg, wr, gains, ones_bd)


def _bucket(n):
    max_exact = REL_BUCKETS // 2
    large = max_exact + (jnp.log(jnp.maximum(n, 1).astype(F32) / max_exact)
                         / math.log(REL_MAX_DIST / max_exact)
                         * (REL_BUCKETS - max_exact)).astype(jnp.int32)
    return jnp.where(n < max_exact, n, jnp.minimum(large, REL_BUCKETS - 1))


def _bias_kernel(tab_ref, bp_ref, bs_ref, *, t_dec):
    h = pl.program_id(0)

    def lookup(n):
        bucket = _bucket(n)
        val = jnp.full(n.shape, tab_ref[h, REL_BUCKETS - 1], F32)
        for b in range(REL_BUCKETS - 1):
            val = jnp.where(bucket == b, tab_ref[h, b], val)
        return val

    tq = bp_ref.shape[2]
    key = lax.broadcasted_iota(jnp.int32, (tq, tq), 0)
    qry = lax.broadcasted_iota(jnp.int32, (tq, tq), 1)
    bp_ref[0, 0] = jnp.where(key <= qry, lookup(jnp.maximum(qry - key, 0)) * LOG2E, NEG_INF)
    bp_ref[0, 1] = lookup(qry - key + tq) * LOG2E

    rows, cols = bs_ref.shape[1], bs_ref.shape[2]
    r = lax.broadcasted_iota(jnp.int32, (rows, cols), 0)
    c = lax.broadcasted_iota(jnp.int32, (rows, cols), 1)
    t = r % t_dec
    tok = c // A_HEADS
    valid = (c % A_HEADS) == h
    far = jnp.full((rows, cols), tab_ref[h, REL_BUCKETS - 1], F32)
    bs_ref[0] = jnp.where(valid, far, NEG_INF)
    bs_ref[1] = jnp.where(valid, lookup(PAGE_SIZE + t - tok), NEG_INF)
    new_ok = valid & (tok <= t) & (tok < t_dec)
    bs_ref[2] = jnp.where(new_ok, lookup(jnp.maximum(t - tok, 0)), NEG_INF)


def _bias_tiles(table, t_dec):
    rows = 2 * t_dec
    return pl.pallas_call(
        functools.partial(_bias_kernel, t_dec=t_dec),
        grid=(A_HEADS,),
        in_specs=[pl.BlockSpec(memory_space=pltpu.SMEM)],
        out_specs=[pl.BlockSpec((1, 2, ATT_TQ, ATT_TQ), lambda h: (h, 0, 0, 0)),
                   pl.BlockSpec((3, rows, PAGE_SIZE * A_HEADS), lambda h: (0, h, 0))],
        out_shape=[jax.ShapeDtypeStruct((A_HEADS, 2, ATT_TQ, ATT_TQ), F32),
                   jax.ShapeDtypeStruct((3, A_HEADS * rows, PAGE_SIZE * A_HEADS), F32)],
        compiler_params=_cparams(1),
        name="bias_tiles",
    )(table.T)


def _lam(lamv_ref, lam_init):
    s1 = jnp.sum(lamv_ref[0:1, :] * lamv_ref[1:2, :], axis=1, keepdims=True)
    s2 = jnp.sum(lamv_ref[2:3, :] * lamv_ref[3:4, :], axis=1, keepdims=True)
    return jnp.exp(s1) - jnp.exp(s2) + lam_init


def _subln(o, g_ref, lam_init):
    y = o * lax.rsqrt(jnp.mean(o * o, axis=-1, keepdims=True) + EPS)
    return y * g_ref[...] * (1.0 - lam_init)


def _attn_prompt_kernel(tab_ref, q_ref, k_ref, v_ref, bias_ref, lamv_ref, subg_ref, o_ref,
                        kb, vt, m_s, acc_s, *, lam_init):
    hg = pl.program_id(1)
    qi = pl.program_id(2)
    tq = q_ref.shape[1]

    n_hd = kb.shape[0]
    d = A_V_DIM
    chains = [(hh, m) for hh in range(n_hd) for m in range(2)]

    @pl.when(qi == 0)
    def _():
        for hh in range(n_hd):
            hcols = slice(hh * 128, (hh + 1) * 128)
            kb[hh] = k_ref[0, :, hcols].astype(BF16)
            for jj in range(vt.shape[1]):
                vt[hh, jj, :d] = jnp.transpose(v_ref[0, jj * tq:(jj + 1) * tq, hcols]).astype(BF16)
                vt[hh, jj, d:] = jnp.ones((vt.shape[2] - d, tq), BF16)

    sub = lax.broadcasted_iota(jnp.int32, (128, tq), 0)
    qm = {}
    for hh in range(n_hd):
        qt = jnp.transpose(q_ref[0, :, hh * 128:(hh + 1) * 128]) * LOG2E
        qm[hh, 0] = jnp.where(sub < A_QK_DIM, qt, 0.0).astype(BF16)
        qm[hh, 1] = jnp.where(sub >= A_QK_DIM, qt, 0.0).astype(BF16)
    m_s[...] = jnp.full(m_s.shape, NEG_INF, F32)
    acc_s[...] = jnp.zeros(acc_s.shape, F32)

    def step(j, bias):
        start = pl.multiple_of(j * tq, tq)
        m_prev = [m_s[c] for c in range(len(chains))]
        ss = [_dot(kb[hh, pl.ds(start, tq), :], qm[hh, m]) + bias[hh] for hh, m in chains]
        m_new = [jnp.maximum(mp, jnp.max(s, axis=0, keepdims=True)) for mp, s in zip(m_prev, ss)]
        ps = [jnp.exp2(s - mn).astype(BF16) for s, mn in zip(ss, m_new)]
        for c, (hh, m) in enumerate(chains):
            acc_s[c] = jnp.exp2(m_prev[c] - m_new[c]) * acc_s[c] + _dot(vt[hh, j], ps[c])
            m_s[c] = m_new[c]

    far = [tab_ref[hg * n_hd + hh, REL_BUCKETS - 1] * LOG2E for hh in range(n_hd)]

    def far_body(j, carry):
        step(j, far)
        return carry

    lax.fori_loop(0, jnp.maximum(qi - 1, 0), far_body, 0)

    @pl.when(qi >= 1)
    def _():
        step(qi - 1, [bias_ref[hh, 1] for hh in range(n_hd)])

    step(qi, [bias_ref[hh, 0] for hh in range(n_hd)])

    lam = _lam(lamv_ref, lam_init)
    for hh in range(n_hd):
        a0, a1 = acc_s[2 * hh], acc_s[2 * hh + 1]
        o = a0[:d] * (1.0 / a0[d:d + 1]) - lam * (a1[:d] * (1.0 / a1[d:d + 1]))
        y = o * lax.rsqrt(jnp.mean(o * o, axis=0, keepdims=True) + EPS)
        y = y * subg_ref[...] * (1.0 - lam_init)
        o_ref[0, :, hh * 128:(hh + 1) * 128] = jnp.transpose(y).astype(o_ref.dtype)


def _attn_prompt(zr3, bias_p, tab_t, lamv, subg, lam_init):
    b, s, _ = zr3.shape
    tq = ATT_TQ
    n_hd = ATT_HEADS_PER_STEP
    wid = 128 * n_hd
    kern = functools.partial(_attn_prompt_kernel, lam_init=lam_init)
    return pl.pallas_call(
        kern,
        grid=(b, A_HEADS // n_hd, s // tq),
        in_specs=[pl.BlockSpec(memory_space=pltpu.SMEM),
                  pl.BlockSpec((1, tq, wid), lambda bi, hg, qi: (bi, qi, COL_AQ // wid + hg)),
                  pl.BlockSpec((1, s, wid), lambda bi, hg, qi: (bi, 0, COL_AK // wid + hg)),
                  pl.BlockSpec((1, s, wid), lambda bi, hg, qi: (bi, 0, COL_AV // wid + hg)),
                  pl.BlockSpec((n_hd, 2, tq, tq), lambda bi, hg, qi: (hg, 0, 0, 0)),
                  pl.BlockSpec((4, A_QK_DIM), lambda bi, hg, qi: (0, 0)),
                  pl.BlockSpec((A_V_DIM, 1), lambda bi, hg, qi: (0, 0))],
        out_specs=pl.BlockSpec((1, tq, wid), lambda bi, hg, qi: (bi, qi, hg)),
        out_shape=jax.ShapeDtypeStruct((b, s, A_WIDTH), BF16),
        scratch_shapes=[pltpu.VMEM((n_hd, s, 128), BF16),
                        pltpu.VMEM((n_hd, s // tq, A_V_DIM + ONES_ROWS, tq), BF16),
                        pltpu.VMEM((2 * n_hd, 1, tq), F32),
                        pltpu.VMEM((2 * n_hd, A_V_DIM + ONES_ROWS, tq), F32)],
        compiler_params=_cparams(3),
        name="attn_prompt",
    )(tab_t, zr3, zr3, zr3, bias_p, lamv, subg.reshape(A_V_DIM, 1))


def _attn_sample_init(q_ref, q_s, m_s, l_s, acc_s, t_dec):
    rows = 2 * t_dec
    q = q_ref[0]
    lane = lax.broadcasted_iota(jnp.int32, (t_dec, 128), 1)
    for h in range(A_HEADS):
        qh = q[:, h * 128:(h + 1) * 128]
        q_s[h * rows:h * rows + t_dec, :] = jnp.where(lane < A_QK_DIM, qh, 0.0)
        q_s[h * rows + t_dec:(h + 1) * rows, :] = jnp.where(lane >= A_QK_DIM, qh, 0.0)
    m_s[...] = jnp.full(m_s.shape, NEG_INF, F32)
    l_s[...] = jnp.zeros(l_s.shape, F32)
    acc_s[...] = jnp.zeros(acc_s.shape, F32)


def _attn_sample_update(q_s, m_s, l_s, acc_s, ks, vs, biases):
    qb = q_s[...].astype(BF16)
    ss = [_dot_nt(qb, kb) + bias for kb, bias in zip(ks, biases)]
    smax = functools.reduce(jnp.maximum, ss)
    m_prev = m_s[...]
    m_new = jnp.maximum(m_prev, jnp.max(smax, axis=1, keepdims=True))
    ps = [jnp.exp(s - m_new) for s in ss]
    alpha = jnp.exp(m_prev - m_new)
    l_s[...] = alpha * l_s[...] + jnp.sum(functools.reduce(jnp.add, ps), axis=1, keepdims=True)
    pv = functools.reduce(jnp.add, [_dot(p.astype(BF16), vb) for p, vb in zip(ps, vs)])
    acc_s[...] = alpha * acc_s[...] + pv
    m_s[...] = m_new


def _attn_sample_final(kn_ref, vn_ref, bias_ref, lamv_ref, subg_ref, o_ref, q_s, m_s, l_s, acc_s,
                       lam_init, t_dec):
    rows = 2 * t_dec
    nk = kn_ref.shape[1]
    _attn_sample_update(q_s, m_s, l_s, acc_s, [kn_ref[0].astype(BF16)], [vn_ref[0].astype(BF16)],
                        [bias_ref[2][:, :nk]])
    lam = _lam(lamv_ref, lam_init)
    on = acc_s[...] * (1.0 / l_s[...])
    for h in range(A_HEADS):
        o = on[h * rows:h * rows + t_dec] - lam * on[h * rows + t_dec:(h + 1) * rows]
        o_ref[0, :, h * 128:(h + 1) * 128] = _subln(o, subg_ref, lam_init).astype(o_ref.dtype)


def _ffn_attn_kernel(pt_ref, x_ref, g_ref, w1_ref, w2_ref, q_ref, kn_ref, vn_ref, bias_ref,
                     lamv_ref, subg_ref, *rest, lam_init, n_pg, t_dec):
    del pt_ref
    k_refs, v_refs = rest[:n_pg], rest[n_pg:2 * n_pg]
    o_ffn, o_att = rest[2 * n_pg:2 * n_pg + 2]
    h_s, acc_s, q_s, m_s, l_s, att_s = rest[2 * n_pg + 2:]
    j = pl.program_id(1)
    last = pl.num_programs(1) - 1

    @pl.when(j == 0)
    def _():
        _attn_sample_init(q_ref, q_s, m_s, l_s, att_s, t_dec)
        _ffn_init(x_ref, g_ref, h_s, acc_s)

    biases = [bias_ref[0]] * (n_pg - 1) + [jnp.where(j == last, bias_ref[1], bias_ref[0])]
    _attn_sample_update(q_s, m_s, l_s, att_s, [r[0, 0].astype(BF16) for r in k_refs],
                        [r[0, 0].astype(BF16) for r in v_refs], biases)
    _ffn_accumulate(w1_ref, w2_ref, h_s, acc_s)

    @pl.when(j == last)
    def _():
        _attn_sample_final(kn_ref, vn_ref, bias_ref, lamv_ref, subg_ref, o_att, q_s, m_s, l_s,
                           att_s, lam_init, t_dec)
        o_ffn[...] = x_ref[...] + acc_s[...]


def _ffn_attn(x, g, w1, w2, zrs3, kn2d, vn2d, cache_k2d, cache_v2d, page_table, bias_s, lamv,
              subg, layer, lam_init):
    m = x.shape[0]
    bd, t_dec, _ = zrs3.shape
    n_pages = page_table.shape[1]
    n_j = FFN_ATTN_STEPS
    assert m % bd == 0 and (m // bd) % 8 == 0 and n_pages % n_j == 0 and D_FF % n_j == 0
    tm, tf, n_pg = m // bd, D_FF // n_j, n_pages // n_j
    rows = 2 * t_dec * A_HEADS
    pcols = PAGE_SIZE * A_HEADS

    def page_spec(p):
        return pl.BlockSpec((1, 1, pcols, 128),
                            lambda i, j, pt: (layer, pt[i, j * n_pg + p], 0, 0))

    def per_seq(arr):
        return pl.BlockSpec((1,) + arr.shape[1:], lambda i, j, pt: (i, 0, 0))

    def const(shape):
        return pl.BlockSpec(shape, lambda i, j, pt: (0,) * len(shape))

    kern = functools.partial(_ffn_attn_kernel, lam_init=lam_init, n_pg=n_pg, t_dec=t_dec)
    grid_spec = pltpu.PrefetchScalarGridSpec(
        num_scalar_prefetch=1,
        grid=(bd, n_j),
        in_specs=[pl.BlockSpec((tm, D_MODEL), lambda i, j, pt: (i, 0)),
                  const((1, D_MODEL)),
                  pl.BlockSpec((D_MODEL, tf), lambda i, j, pt: (0, j)),
                  pl.BlockSpec((tf, D_MODEL), lambda i, j, pt: (j, 0)),
                  pl.BlockSpec((1, t_dec, A_WIDTH), lambda i, j, pt: (i, 0, 0)),
                  per_seq(kn2d), per_seq(vn2d),
                  const((3, rows, pcols)), const((4, A_QK_DIM)), const((1, A_V_DIM))]
                 + [page_spec(p) for p in range(n_pg)] * 2,
        out_specs=[pl.BlockSpec((tm, D_MODEL), lambda i, j, pt: (i, 0)),
                   pl.BlockSpec((1, t_dec, A_WIDTH), lambda i, j, pt: (i, 0, 0))],
        scratch_shapes=[pltpu.VMEM((tm, D_MODEL), BF16), pltpu.VMEM((tm, D_MODEL), F32),
                        pltpu.VMEM((rows, 128), F32), pltpu.VMEM((rows, 1), F32),
                        pltpu.VMEM((rows, 1), F32), pltpu.VMEM((rows, 128), F32)],
    )
    return pl.pallas_call(
        kern,
        grid_spec=grid_spec,
        out_shape=[jax.ShapeDtypeStruct((m, D_MODEL), F32),
                   jax.ShapeDtypeStruct((bd, t_dec, A_WIDTH), BF16)],
        compiler_params=_cparams(2),
        name="ffn_attn",
    )(page_table, x, g.reshape(1, D_MODEL), w1, w2, zrs3, kn2d, vn2d, bias_s, lamv, subg,
      *([cache_k2d] * n_pg), *([cache_v2d] * n_pg))


def _gelu(x):
    return 0.5 * x * (1.0 + jnp.tanh(math.sqrt(2.0 / math.pi) * (x + 0.044715 * (x * x * x))))


def _chunk_mlp_kernel(u_ref, v_ref, w_ref, bias_ref, g_ref, b_ref, y_ref, vb_ref):
    n_chunks = u_ref.shape[0] // B_CHUNK
    lane_grp = lax.broadcasted_iota(jnp.int32, (B_CHUNK, B_WIDTH), 1) // (B_WIDTH // B_GROUPS)
    for c in range(n_chunks):
        sl = slice(c * B_CHUNK, (c + 1) * B_CHUNK)
        gv = _gelu(v_ref[sl, :])
        xc = gv - jnp.mean(gv, axis=-1, keepdims=True)
        vb = xc * lax.rsqrt(jnp.mean(xc * xc, axis=-1, keepdims=True) + EPS) * g_ref[...] + b_ref[...]
        vb_ref[sl, :] = vb
        vbb = vb.astype(BF16)
        mixed = bias_ref[...]
        for g in range(B_GROUPS):
            mixed = mixed + jnp.where(lane_grp == g, _dot(w_ref[g], vbb), 0.0)
        y_ref[sl, :] = (_gelu(u_ref[sl, :]) * mixed).astype(y_ref.dtype)


def _chunk_mlp(zr, w_eff, bias_eff, ln_g, ln_b):
    m = zr.shape[0]
    tm = min(m, CHUNK_MLP_TM)
    return pl.pallas_call(
        _chunk_mlp_kernel,
        grid=(m // tm,),
        in_specs=[pl.BlockSpec((tm, B_WIDTH), lambda i: (i, COL_BU // B_WIDTH)),
                  pl.BlockSpec((tm, B_WIDTH), lambda i: (i, COL_BV // B_WIDTH)),
                  pl.BlockSpec((B_GROUPS, B_CHUNK, B_CHUNK), lambda i: (0, 0, 0)),
                  pl.BlockSpec((B_CHUNK, B_WIDTH), lambda i: (0, 0)),
                  pl.BlockSpec((1, B_WIDTH), lambda i: (0, 0)),
                  pl.BlockSpec((1, B_WIDTH), lambda i: (0, 0))],
        out_specs=[pl.BlockSpec((tm, B_WIDTH), lambda i: (i, 0)),
                   pl.BlockSpec((tm, B_WIDTH), lambda i: (i, 0))],
        out_shape=[jax.ShapeDtypeStruct((m, B_WIDTH), BF16),
                   jax.ShapeDtypeStruct((m, B_WIDTH), F32)],
        compiler_params=_cparams(1),
        name="chunk_mlp",
    )(zr, zr, w_eff, bias_eff, ln_g, ln_b)


def _pool_kernel(x_ref, w_ref, scale_ref, y_ref, *, prefix, pos0, seq_rows):
    x = x_ref[0]
    row = lax.broadcasted_iota(jnp.int32, x.shape, 0) % seq_rows
    grp = lax.broadcasted_iota(jnp.int32, x.shape, 1) // C_GROUP_DIM

    def shifted(a, k):
        return jnp.where(row >= k, pltpu.roll(a, k, 0), 0.0)

    sums = []
    acc = x
    for k in (1, 2, 4, 8):
        acc = acc + shifted(acc, k)
        sums.append(acc)
    total = sums[3]
    win = jnp.full(x.shape, POOL_WINDOWS[3], jnp.int32)
    for g in range(3):
        total = jnp.where(grp == g, sums[g], total)
        win = jnp.where(grp == g, POOL_WINDOWS[g], win)
    pos = pos0 + row - prefix
    cnt = jnp.clip(pos + 1, 1, win).astype(F32)
    d = total / cnt - x
    y = _dot(d.astype(BF16), w_ref[...]) * scale_ref[...]
    y_ref[0] = y.astype(y_ref.dtype)


def _pool_mix(xx, col_block, w_bd, scale, prefix, pos0, seq_rows):
    b, rows, _ = xx.shape
    kern = functools.partial(_pool_kernel, prefix=prefix, pos0=pos0, seq_rows=seq_rows)
    return pl.pallas_call(
        kern,
        grid=(b,),
        in_specs=[pl.BlockSpec((1, rows, C_WIDTH), lambda i: (i, 0, col_block)),
                  pl.BlockSpec((C_WIDTH, C_WIDTH), lambda i: (0, 0)),
                  pl.BlockSpec((1, C_WIDTH), lambda i: (0, 0))],
        out_specs=pl.BlockSpec((1, rows, C_WIDTH), lambda i: (i, 0, 0)),
        out_shape=jax.ShapeDtypeStruct((b, rows, C_WIDTH), BF16),
        compiler_params=_cparams(1),
        name="pool_mix",
    )(xx, w_bd, scale)


def _log_sigmoid(x):
    return jnp.minimum(x, 0.0) - jnp.log1p(jnp.exp(-jnp.abs(x)))


def _split3(x):
    def top8(a):
        bits = lax.bitcast_convert_type(a, jnp.int32) & jnp.int32(-65536)
        return lax.bitcast_convert_type(bits, F32)

    p1 = top8(x)
    r1 = x - p1
    p2 = top8(r1)
    return p1, p2, r1 - p2


def _pack3(pieces):
    return (pieces[0] + pltpu.roll(pieces[1], D_HEADS, 1) + pltpu.roll(pieces[2], 2 * D_HEADS, 1))


def _mlstm_kernel(q_ref, k_ref, v_ref, o_ref, gi_ref, gf_ref, gb_ref, ng_ref, sel_ref, bd_ref,
                  c0_ref, n0_ref, m0_ref, y_ref, c_out, n_out, m_out, c_s, n_s, m_s, *, t_valid):
    ci = pl.program_id(1)
    nb, chunk = q_ref.shape[0], q_ref.shape[1]
    hd = D_HEAD_DIM
    eye_h = (lax.broadcasted_iota(jnp.int32, (hd, hd), 0)
             == lax.broadcasted_iota(jnp.int32, (hd, hd), 1))

    @pl.when(ci == 0)
    def _():
        c_s[...] = jnp.zeros(c_s.shape, F32)
        n_s[...] = jnp.zeros(n_s.shape, F32)
        m_s[...] = m0_ref[...]
        for bi in range(nb):
            for h in range(D_HEADS):
                hs = slice(h * hd, (h + 1) * hd)
                c_s[bi, hs, hs] = c0_ref[bi, h]
                n_col = jnp.sum(jnp.where(eye_h, n0_ref[bi, h], 0.0), axis=1, keepdims=True)
                n_s[bi, hs, hs] = jnp.broadcast_to(n_col, (hd, hd))

    row = lax.broadcasted_iota(jnp.int32, (chunk, 128), 0)
    lane = lax.broadcasted_iota(jnp.int32, (chunk, 128), 1)
    head_lane = lane < D_HEADS
    grp = lax.broadcasted_iota(jnp.int32, (chunk, D_WIDTH), 1) // hd
    rr = lax.broadcasted_iota(jnp.int32, (chunk, chunk), 0)
    cc = lax.broadcasted_iota(jnp.int32, (chunk, chunk), 1)
    allowed = (cc <= rr) & (cc < t_valid)
    ones_w = jnp.ones((chunk, D_WIDTH), BF16)
    same_head = (lax.broadcasted_iota(jnp.int32, (D_WIDTH, D_WIDTH), 0) // hd
                 == lax.broadcasted_iota(jnp.int32, (D_WIDTH, D_WIDTH), 1) // hd)
    pick = [(lane % D_HEADS == h) & (lane < 6 * D_HEADS) for h in range(D_HEADS)]
    in_head = [grp == h for h in range(D_HEADS)]
    lower_ones = jnp.where(lane < 3 * D_HEADS, 1.0, 0.0)
    upper_ones = jnp.where((lane >= 3 * D_HEADS) & (lane < 6 * D_HEADS), 1.0, 0.0)
    head_one = [jnp.where(ih, 1.0, 0.0).astype(BF16) for ih in in_head]

    for bi in range(nb):
        gi = gi_ref[bi] + gb_ref[0:1, :]
        lf = _log_sigmoid(gf_ref[bi] + gb_ref[1:2, :])
        if t_valid < chunk:
            gi = jnp.where(row < t_valid, gi, NEG_INF)
            lf = jnp.where(row < t_valid, lf, 0.0)
        b = lf
        k = 1
        while k < chunk:
            b = b + jnp.where(row >= k, pltpu.roll(b, k, 0), 0.0)
            k *= 2
        u = gi - b
        cm = u
        k = 1
        while k < chunk:
            cm = jnp.maximum(cm, jnp.where(row >= k, pltpu.roll(cm, k, 0), NEG_INF))
            k *= 2
        m_prev = m_s[bi]
        big_m = jnp.maximum(m_prev, cm)
        m_last = big_m[chunk - 1:chunk, :]
        winter = jnp.exp(m_prev - big_m)
        emt = jnp.exp(jnp.minimum(-(b + big_m), EMT_CAP))
        ws = jnp.exp(u - m_last)
        m_s[bi] = b[chunk - 1:chunk, :] + m_last

        def per_head_lanes(z):
            packed = _pack3(_split3(jnp.where(head_lane, z, 0.0)))
            return _dot(packed.astype(BF16), sel_ref[...])

        winter_r, emt_r, ws_r = per_head_lanes(winter), per_head_lanes(emt), per_head_lanes(ws)
        decay_r = winter_r[chunk - 1:chunk, :]

        q = q_ref[bi]
        qb = q.astype(BF16)
        kf = k_ref[bi] * (hd ** -0.5)
        kb = kf.astype(BF16)
        vf = v_ref[bi]
        vb = vf.astype(BF16)
        u_fin = jnp.where(head_lane & (row < t_valid), u, 0.0)
        y_side = (_pack3(_split3(u_fin)) + upper_ones).astype(BF16)
        x_all = lower_ones + pltpu.roll(_pack3(_split3(jnp.where(head_lane, -big_m, 0.0))),
                                        3 * D_HEADS, 1)
        acc = jnp.zeros((chunk, 2 * D_WIDTH), F32)
        for h in range(D_HEADS):
            x_side = jnp.where(pick[h], x_all, 0.0).astype(BF16)
            expo = jnp.where(allowed, _dot_nt(x_side, y_side), NEG_INF)
            qk = _dot_nt(jnp.where(in_head[h], q, 0.0).astype(BF16), kb) * jnp.exp(expo)
            rhs = jnp.concatenate([jnp.where(in_head[h], vf, 0.0).astype(BF16), head_one[h]], axis=1)
            acc = acc + _dot(qk.astype(BF16), rhs)
        state = jnp.concatenate([c_s[bi].astype(BF16), n_s[bi].astype(BF16)], axis=1)
        inter = _dot(qb, state)
        num = winter_r * inter[:, :D_WIDTH] + acc[:, :D_WIDTH]
        den = winter_r * inter[:, D_WIDTH:] + acc[:, D_WIDTH:]
        hh = num / jnp.maximum(jnp.abs(den), emt_r)
        h2 = hh * hh
        hi = h2.astype(BF16)
        lo = (h2 - hi.astype(F32)).astype(BF16)
        ssq = _dot(hi, bd_ref[...]) + _dot(lo, bd_ref[...])
        y = hh * lax.rsqrt(ssq * (1.0 / hd) + EPS) * ng_ref[...]
        y_ref[bi] = (y * jax.nn.sigmoid(o_ref[bi])).astype(y_ref.dtype)
        kw = (ws_r * kf).astype(BF16)
        upd = _dot_tn(kw, jnp.concatenate([vb, ones_w], axis=1))
        c_s[bi] = decay_r * c_s[bi] + jnp.where(same_head, upd[:, :D_WIDTH], 0.0)
        n_s[bi] = decay_r * n_s[bi] + jnp.where(same_head, upd[:, D_WIDTH:], 0.0)

    @pl.when(ci == pl.num_programs(1) - 1)
    def _():
        m_out[...] = m_s[...]
        for bi in range(nb):
            for h in range(D_HEADS):
                hs = slice(h * hd, (h + 1) * hd)
                c_out[bi, h] = c_s[bi, hs, hs]
                n_out[bi, h] = jnp.sum(jnp.where(eye_h, n_s[bi, hs, hs], 0.0), axis=0, keepdims=True)


def _mlstm(src, col0, gate_block, chunk, t_valid, gate_bias, norm_g, c0, n0, m0):
    b, t, _ = src.shape
    nb = math.gcd(b, MLSTM_BATCH)
    hd = D_HEAD_DIM
    kern = functools.partial(_mlstm_kernel, t_valid=t_valid)
    head_of_lane = jnp.arange(D_WIDTH) // hd
    src_lane = jnp.arange(128)[:, None]
    sel3 = ((src_lane % D_HEADS == head_of_lane[None, :]) & (src_lane < 3 * D_HEADS)).astype(BF16)
    same_head = (head_of_lane[:, None] == head_of_lane[None, :]).astype(BF16)
    m0p = jnp.pad(m0, ((0, 0), (0, 0), (0, 128 - D_HEADS)))

    def col(cb, width=D_WIDTH):
        return pl.BlockSpec((nb, chunk, width), lambda bi, ci: (bi, ci, cb))

    def const(shape):
        return pl.BlockSpec(shape, lambda bi, ci: (0,) * len(shape))

    def per_seq(shape):
        return pl.BlockSpec((nb,) + shape, lambda bi, ci: (bi,) + (0,) * len(shape))

    y, c_new, n_new, m_new = pl.pallas_call(
        kern,
        grid=(b // nb, t // chunk),
        in_specs=[col(col0), col(col0 + 1), col(col0 + 2), col(col0 + 3),
                  col(gate_block, 128), col(gate_block + 1, 128),
                  const((2, 128)), const((1, D_WIDTH)), const((128, D_WIDTH)),
                  const((D_WIDTH, D_WIDTH)),
                  per_seq((D_HEADS, hd, hd)), per_seq((D_HEADS, 1, hd)), per_seq((1, 128))],
        out_specs=[col(0), per_seq((D_HEADS, hd, hd)), per_seq((D_HEADS, 1, hd)),
                   per_seq((1, 128))],
        out_shape=[jax.ShapeDtypeStruct((b, t, D_WIDTH), BF16),
                   jax.ShapeDtypeStruct((b, D_HEADS, hd, hd), F32),
                   jax.ShapeDtypeStruct((b, D_HEADS, 1, hd), F32),
                   jax.ShapeDtypeStruct((b, 1, 128), F32)],
        scratch_shapes=[pltpu.VMEM((nb, D_WIDTH, D_WIDTH), F32),
                        pltpu.VMEM((nb, D_WIDTH, D_WIDTH), F32),
                        pltpu.VMEM((nb, 1, 128), F32)],
        compiler_params=_cparams(2),
        name="mlstm",
    )(src, src, src, src, src, src, gate_bias, jnp.tile(norm_g, (1, D_HEADS)), sel3, same_head,
      c0, n0, m0p)
    return y, c_new, n_new, m_new[:, :, :D_HEADS]


def _merge_kernel(x_ref, g_ref, ya_ref, yb_ref, yc_ref, yd_ref, wa_ref, wb_ref, wc_ref, wd_ref,
                  wo_ref, o_ref):
    def gate(i):
        return g_ref[:, i * D_MODEL:(i + 1) * D_MODEL].astype(F32)

    merged = gate(0) * _dot(ya_ref[...], wa_ref[...])
    merged = merged + gate(1) * _dot(yb_ref[...], wb_ref[...])
    merged = merged + gate(2) * _dot(yc_ref[...], wc_ref[...])
    merged = merged + gate(3) * _dot(yd_ref[...], wd_ref[...])
    o_ref[...] = x_ref[...] + _dot(merged.astype(BF16), wo_ref[...])


def _merge(x, gates, ya, yb, yc, yd, wa, wb, wc, wd, wo):
    m = x.shape[0]
    tm = min(m, 512)

    def rows(width):
        return pl.BlockSpec((tm, width), lambda i: (i, 0))

    def full(arr):
        return pl.BlockSpec(arr.shape, lambda i: (0, 0))

    return pl.pallas_call(
        _merge_kernel,
        grid=(m // tm,),
        in_specs=[rows(D_MODEL), rows(GZ_WIDTH), rows(A_WIDTH), rows(B_WIDTH), rows(C_WIDTH),
                  rows(D_WIDTH), full(wa), full(wb), full(wc), full(wd), full(wo)],
        out_specs=rows(D_MODEL),
        out_shape=jax.ShapeDtypeStruct((m, D_MODEL), F32),
        compiler_params=_cparams(1),
        name="merge",
    )(x, gates, ya, yb, yc, yd, wa, wb, wc, wd, wo)


def _ffn_init(x_ref, g_ref, h_s, acc_s):
    x = x_ref[...]
    y = x * lax.rsqrt(jnp.mean(x * x, axis=-1, keepdims=True) + EPS)
    h_s[...] = (y * g_ref[...]).astype(BF16)
    acc_s[...] = jnp.zeros(acc_s.shape, F32)


def _ffn_accumulate(w1_ref, w2_ref, h_s, acc_s):
    a = jnp.maximum(_dot(h_s[...], w1_ref[...]), 0.0)
    acc_s[...] += _dot((a * a).astype(BF16), w2_ref[...])


def _ffn_kernel(x_ref, g_ref, w1_ref, w2_ref, o_ref, h_s, acc_s):
    j = pl.program_id(1)

    @pl.when(j == 0)
    def _():
        _ffn_init(x_ref, g_ref, h_s, acc_s)

    _ffn_accumulate(w1_ref, w2_ref, h_s, acc_s)

    @pl.when(j == pl.num_programs(1) - 1)
    def _():
        o_ref[...] = x_ref[...] + acc_s[...]


def _ffn(x, g, w1, w2):
    m = x.shape[0]
    tm, tf = min(m, 1024), 1024
    return pl.pallas_call(
        _ffn_kernel,
        grid=(m // tm, D_FF // tf),
        in_specs=[pl.BlockSpec((tm, D_MODEL), lambda i, j: (i, 0)),
                  pl.BlockSpec((1, D_MODEL), lambda i, j: (0, 0)),
                  pl.BlockSpec((D_MODEL, tf), lambda i, j: (0, j)),
                  pl.BlockSpec((tf, D_MODEL), lambda i, j: (j, 0))],
        out_specs=pl.BlockSpec((tm, D_MODEL), lambda i, j: (i, 0)),
        out_shape=jax.ShapeDtypeStruct((m, D_MODEL), F32),
        scratch_shapes=[pltpu.VMEM((tm, D_MODEL), BF16), pltpu.VMEM((tm, D_MODEL), F32)],
        compiler_params=_cparams(2),
        name="ffn",
    )(x, g.reshape(1, D_MODEL), w1, w2)


def _layer_weights(p, l):
    w_in = p["w_in"][l]
    gate_pad = jnp.zeros((D_MODEL, 128 - D_HEADS), F32)
    wr = jnp.concatenate(
        [w_in[:, :3072], w_in[:, 3080:3336], w_in[:, 3072:3076], gate_pad,
         w_in[:, 3076:3080], gate_pad], axis=1).astype(BF16)
    wg = w_in[:, 3336:].astype(BF16)
    reps = A_WIDTH // A_QK_DIM
    gains = jnp.stack([jnp.tile(p["q_norm_g"][l], reps) * (A_QK_DIM ** -0.5),
                       jnp.tile(p["k_norm_g"][l], reps)]).reshape(2, 1, A_WIDTH)
    grp = jnp.arange(A_WIDTH) // A_QK_DIM
    ones_bd = (grp[:, None] == grp[None, :]).astype(BF16)
    lamv = jnp.stack([p["lam_q1"][l], p["lam_k1"][l], p["lam_q2"][l], p["lam_k2"][l]])
    gate_bias = jnp.pad(jnp.stack([p["d_i_bias"][l], p["d_f_bias"][l]]),
                        ((0, 0), (0, 128 - D_HEADS)))
    c_bd = jnp.zeros((C_WIDTH, C_WIDTH), F32)
    for g in range(4):
        sl = slice(g * C_GROUP_DIM, (g + 1) * C_GROUP_DIM)
        c_bd = c_bd.at[sl, sl].set(p["c_lin"][l][g])
    return dict(
        norm1_g=p["norm1_g"][l], norm2_g=p["norm2_g"][l], wr=wr, wg=wg, gains=gains,
        ones_bd=ones_bd, lamv=lamv, subg=p["subln_g"][l].reshape(1, A_V_DIM),
        b_ln_g=p["b_ln_g"][l].reshape(1, B_WIDTH), b_ln_b=p["b_ln_b"][l].reshape(1, B_WIDTH),
        b_ws=p["b_ws"][l], b_bias=p["b_bias"][l],
        c_bd=c_bd.astype(BF16), c_scale=p["c_scale"][l].reshape(1, C_WIDTH),
        gate_bias=gate_bias, d_norm_g=p["d_norm_g"][l].reshape(1, D_HEAD_DIM),
        w_pa=p["w_pa"][l].astype(BF16), w_pb=p["w_pb"][l].astype(BF16),
        w_pc=p["w_pc"][l].astype(BF16), w_pd=p["w_pd"][l].astype(BF16),
        w_out=p["w_out"][l].astype(BF16), w_ff1=p["w_ff1"][l].astype(BF16),
        w_ff2=p["w_ff2"][l].astype(BF16),
        lam_init=0.8 - 0.6 * math.exp(-0.3 * l), layer=l,
    )


def _chunk_weights(w, t):
    length = min(t, B_CHUNK)
    ws = jnp.tril(w["b_ws"][:, :length, :length])
    bias = jnp.transpose(w["b_bias"][:, :length])
    reps = B_CHUNK // length
    if reps > 1:
        eye = jnp.eye(reps, dtype=F32)
        ws = jax.vmap(lambda a: jnp.kron(eye, a))(ws)
        bias = jnp.tile(bias, (reps, 1))
    return ws.astype(BF16), jnp.repeat(bias, B_WIDTH // B_GROUPS, axis=1)


def _mix_and_merge(x2, w, gates, zr, ya, yc, yd, t):
    w_eff, bias_eff = _chunk_weights(w, t)
    yb, vb = _chunk_mlp(zr, w_eff, bias_eff, w["b_ln_g"], w["b_ln_b"])
    x2 = _merge(x2, gates, ya, yb, yc, yd, w["w_pa"], w["w_pb"], w["w_pc"], w["w_pd"], w["w_out"])
    return x2, vb


def _prompt_layer(x2, w, b, s, bias_p, tab_t, sample_attn):
    m = b * s
    gates, zr, k_rows, v_rows = _in_proj(x2, w["norm1_g"], w["wg"], w["wr"], w["gains"],
                                         w["ones_bd"])
    zr3 = zr.reshape(b, s, ZR_WIDTH)
    ya = _attn_prompt(zr3, bias_p, tab_t, w["lamv"], w["subg"], w["lam_init"]).reshape(m, A_WIDTH)
    yc = _pool_mix(zr3, COL_CX // C_WIDTH, w["c_bd"], w["c_scale"], 0, 0, s).reshape(m, C_WIDTH)
    hd = D_HEAD_DIM
    yd, c_new, n_new, m_new = _mlstm(
        zr3, COL_DQ // D_WIDTH, COL_DGI // 128, min(s, MLSTM_CHUNK), min(s, MLSTM_CHUNK),
        w["gate_bias"],
        w["d_norm_g"], jnp.zeros((b, D_HEADS, hd, hd), F32), jnp.zeros((b, D_HEADS, 1, hd), F32),
        jnp.zeros((b, 1, D_HEADS), F32))
    x2, _ = _mix_and_merge(x2, w, gates, zr, ya, yc, yd.reshape(m, D_WIDTH), s)
    x2, ya_sample = _ffn_attn(x2, w["norm2_g"], w["w_ff1"], w["w_ff2"], *sample_attn,
                              w["lamv"], w["subg"], w["layer"], w["lam_init"])
    outs = (k_rows.reshape(b, s, A_HEADS, 2 * A_QK_DIM), v_rows.reshape(b, s, A_HEADS, A_V_DIM),
            zr3[:, s - POOL_BUF:, COL_CX:COL_CX + C_WIDTH],
            c_new, n_new.reshape(b, D_HEADS, hd), m_new.reshape(b, D_HEADS))
    return x2, outs, ya_sample


def _sample_layer(x2, w, bd, t, proj, ya, past, pool0, c0, n0, m0):
    m = bd * t
    gates, zr, kn, vn = proj
    zr3 = zr.reshape(bd, t, ZR_WIDTH)
    ya = ya.reshape(m, A_WIDTH)
    cx = zr3[:, :, COL_CX:COL_CX + C_WIDTH]
    prefix = POOL_BUF + 1
    rows = -(-(prefix + t) // 8) * 8
    xx = jnp.concatenate([jnp.zeros((bd, 1, C_WIDTH), F32), pool0, cx,
                          jnp.zeros((bd, rows - prefix - t, C_WIDTH), F32)], axis=1)
    yc = _pool_mix(xx.reshape(1, bd * rows, C_WIDTH), 0, w["c_bd"], w["c_scale"], prefix, past, rows)
    yc = yc.reshape(bd, rows, C_WIDTH)[:, prefix:prefix + t].reshape(m, C_WIDTH)
    chunk = -(-t // SAMPLE_MLSTM_CHUNK) * SAMPLE_MLSTM_CHUNK
    dsrc = jnp.pad(zr3[:, :, COL_DQ:COL_DGF + 128], ((0, 0), (0, chunk - t), (0, 0)))
    hd = D_HEAD_DIM
    yd, c_new, n_new, m_new = _mlstm(
        dsrc, 0, (COL_DGI - COL_DQ) // 128, chunk, t, w["gate_bias"], w["d_norm_g"],
        c0, n0.reshape(bd, D_HEADS, 1, hd), m0.reshape(bd, 1, D_HEADS))
    yd = yd[:, :t].reshape(m, D_WIDTH)
    x2, vb = _mix_and_merge(x2, w, gates, zr, ya, yc, yd, t)
    x2 = _ffn(x2, w["norm2_g"], w["w_ff1"], w["w_ff2"])
    outs = (kn.reshape(bd, t, A_HEADS, 2 * A_QK_DIM), vn.reshape(bd, t, A_HEADS, A_V_DIM),
            vb.reshape(bd, t, B_WIDTH), jnp.concatenate([pool0, cx], axis=1)[:, -POOL_BUF:],
            c_new, n_new.reshape(bd, D_HEADS, hd), m_new.reshape(bd, D_HEADS))
    return x2, outs


def kernel(x_prompt, x_sample, cache_k, cache_v, page_table, state_pool, state_C, state_n, state_m, rel_bias, norm1_g, norm2_g, w_in, q_norm_g, k_norm_g, lam_q1, lam_k1, lam_q2, lam_k2, subln_g, b_ln_g, b_ln_b, b_ws, b_bias, c_lin, c_scale, d_i_bias, d_f_bias, d_norm_g, w_pa, w_pb, w_pc, w_pd, w_out, w_ff1, w_ff2):
    p = dict(norm1_g=norm1_g, norm2_g=norm2_g, w_in=w_in, q_norm_g=q_norm_g, k_norm_g=k_norm_g,
             lam_q1=lam_q1, lam_k1=lam_k1, lam_q2=lam_q2, lam_k2=lam_k2, subln_g=subln_g,
             b_ln_g=b_ln_g, b_ln_b=b_ln_b, b_ws=b_ws, b_bias=b_bias, c_lin=c_lin, c_scale=c_scale,
             d_i_bias=d_i_bias, d_f_bias=d_f_bias, d_norm_g=d_norm_g, w_pa=w_pa, w_pb=w_pb,
             w_pc=w_pc, w_pd=w_pd, w_out=w_out, w_ff1=w_ff1, w_ff2=w_ff2)
    depth = w_in.shape[0]
    bp, sp, _ = x_prompt.shape
    bd, td, _ = x_sample.shape
    n_phys = cache_k.shape[1]
    cache_k2d = cache_k.reshape(depth, n_phys, PAGE_SIZE * A_HEADS, 128)
    cache_v2d = cache_v.reshape(depth, n_phys, PAGE_SIZE * A_HEADS, 128)
    bias_p, bias_s = _bias_tiles(rel_bias, td)
    tab_t = rel_bias.T

    xp = x_prompt.reshape(bp * sp, D_MODEL)
    xs = x_sample.reshape(bd * td, D_MODEL)
    prompt_outs, sample_outs = [], []
    for l in range(depth):
        w = _layer_weights(p, l)
        proj = _in_proj(xs, w["norm1_g"], w["wg"], w["wr"], w["gains"], w["ones_bd"])
        sample_attn = (proj[1].reshape(bd, td, ZR_WIDTH),
                       proj[2].reshape(bd, td * A_HEADS, 128), proj[3].reshape(bd, td * A_HEADS, 128),
                       cache_k2d, cache_v2d, page_table, bias_s)
        xp, po, ya_sample = _prompt_layer(xp, w, bp, sp, bias_p, tab_t, sample_attn)
        xs, so = _sample_layer(xs, w, bd, td, proj, ya_sample, page_table.shape[1] * PAGE_SIZE,
                               state_pool[l], state_C[l], state_n[l], state_m[l])
        prompt_outs.append(po)
        sample_outs.append(so)

    def stack(outs, i):
        return jnp.stack([o[i] for o in outs])

    return (xp.reshape(bp, sp, D_MODEL), xs.reshape(bd, td, D_MODEL),
            stack(prompt_outs, 0), stack(prompt_outs, 1), stack(sample_outs, 0), stack(sample_outs, 1),
            stack(sample_outs, 2), stack(prompt_outs, 2), stack(sample_outs, 3),
            stack(prompt_outs, 3), stack(prompt_outs, 4), stack(prompt_outs, 5),
            stack(sample_outs, 4), stack(sample_outs, 5), stack(sample_outs, 6))
```

```python
import functools
import math

import jax
import jax.numpy as jnp
from jax import lax
from jax.experimental import pallas as pl
from jax.experimental.pallas import tpu as pltpu

F32 = jnp.float32
BF16 = jnp.bfloat16
NEG_INF = float("-inf")

D_MODEL = 1024
A_HEADS = 4
A_QK_DIM = 64
A_V_DIM = 128
A_WIDTH = 512
REL_BUCKETS = 32
REL_MAX_DIST = 128
PAGE_SIZE = 128
B_GROUPS = 4
B_WIDTH = 256
B_CHUNK = 128
C_WIDTH = 256
C_GROUP_DIM = 64
POOL_WINDOWS = (2, 4, 8, 16)
POOL_BUF = 15
D_HEADS = 4
D_WIDTH = 256
D_HEAD_DIM = 64
D_FF = 4096
N_BRANCH = 4
EPS = 1e-6

ZR_WIDTH = 3584
COL_AQ, COL_AK, COL_AV = 0, 512, 1024
COL_BU, COL_BV, COL_CX = 1536, 1792, 2048
COL_DQ, COL_DK, COL_DV, COL_DO = 2304, 2560, 2816, 3072
COL_DGI, COL_DGF = 3328, 3456
GZ_WIDTH = N_BRANCH * D_MODEL
EMT_CAP = 80.0
LOG2E = math.log2(math.e)
ONES_ROWS = 16

IN_PROJ_TM = 512
IN_PROJ_TN = 512
CHUNK_MLP_TM = 1024
ATT_TQ = 512
ATT_HEADS_PER_STEP = 2
MERGE_TM = 1024
FFN_ATTN_STEPS = 4
MLSTM_BATCH = 4
MLSTM_CHUNK = 256
SAMPLE_MLSTM_CHUNK = 32
VMEM_LIMIT = 56 * 1024 * 1024


def _cparams(n_axes):
    return pltpu.CompilerParams(dimension_semantics=("arbitrary",) * n_axes,
                                vmem_limit_bytes=VMEM_LIMIT)


def _dot(a, b):
    return jnp.dot(a, b, preferred_element_type=F32)


def _dot_nt(a, b):
    return lax.dot_general(a, b, (((1,), (1,)), ((), ())), preferred_element_type=F32)


def _dot_tn(a, b):
    return lax.dot_general(a, b, (((0,), (0,)), ((), ())), preferred_element_type=F32)


def _in_proj_kernel(x_ref, g_ref, wg_ref, wr_ref, gain_ref, ones_ref,
                    gates_ref, zr_ref, kout_ref, vout_ref):
    tm = x_ref.shape[0]
    tn = IN_PROJ_TN
    x = x_ref[...]
    h = (x * lax.rsqrt(jnp.mean(x * x, axis=-1, keepdims=True) + EPS) * g_ref[...]).astype(BF16)

    def head_rows(dst_ref, val):
        for hh in range(A_HEADS):
            dst_ref[pl.ds(hh, tm, stride=A_HEADS), :] = val[:, hh * 128:(hh + 1) * 128]

    for c in range(ZR_WIDTH // tn):
        cols = slice(c * tn, (c + 1) * tn)
        z = _dot(h, wr_ref[:, cols])
        if c * tn in (COL_AQ, COL_AK):
            z2 = z * z
            hi = z2.astype(BF16)
            lo = (z2 - hi.astype(F32)).astype(BF16)
            ssq = _dot(hi, ones_ref[...]) + _dot(lo, ones_ref[...])
            z = z * lax.rsqrt(ssq * (1.0 / A_QK_DIM) + EPS) * gain_ref[c]
        zr_ref[:, cols] = z
        if c * tn == COL_AK:
            head_rows(kout_ref, z)
        if c * tn == COL_AV:
            head_rows(vout_ref, z)

    for c in range(GZ_WIDTH // tn):
        cols = slice(c * tn, (c + 1) * tn)
        gates_ref[:, cols] = jax.nn.sigmoid(_dot(h, wg_ref[:, cols])).astype(gates_ref.dtype)


def _in_proj(x, g, wg, wr, gains, ones_bd):
    m = x.shape[0]
    tm = min(m, IN_PROJ_TM)

    def rows(width):
        return pl.BlockSpec((tm, width), lambda i: (i, 0))

    def resident(arr):
        return pl.BlockSpec(arr.shape, lambda i: (0,) * arr.ndim, pipeline_mode=pl.Buffered(1))

    kv_spec = pl.BlockSpec((tm * A_HEADS, 128), lambda i: (i, 0))
    kv_shape = jax.ShapeDtypeStruct((m * A_HEADS, 128), F32)
    g = g.reshape(1, D_MODEL)
    return pl.pallas_call(
        _in_proj_kernel,
        grid=(m // tm,),
        in_specs=[rows(D_MODEL), resident(g), resident(wg), resident(wr), resident(gains),
                  resident(ones_bd)],
        out_specs=[rows(GZ_WIDTH), rows(ZR_WIDTH), kv_spec, kv_spec],
        out_shape=[jax.ShapeDtypeStruct((m, GZ_WIDTH), BF16),
                   jax.ShapeDtypeStruct((m, ZR_WIDTH), F32), kv_shape, kv_shape],
        compiler_params=_cparams(1),
        name="in_proj",
    )(x, g, wg, wr, gains, ones_bd)


def _bucket(n):
    max_exact = REL_BUCKETS // 2
    large = max_exact + (jnp.log(jnp.maximum(n, 1).astype(F32) / max_exact)
                         / math.log(REL_MAX_DIST / max_exact)
                         * (REL_BUCKETS - max_exact)).astype(jnp.int32)
    return jnp.where(n < max_exact, n, jnp.minimum(large, REL_BUCKETS - 1))


def _bias_kernel(tab_ref, bp_ref, bs_ref, *, t_dec):
    h = pl.program_id(0)

    def lookup(n):
        bucket = _bucket(n)
        val = jnp.full(n.shape, tab_ref[h, REL_BUCKETS - 1], F32)
        for b in range(REL_BUCKETS - 1):
            val = jnp.where(bucket == b, tab_ref[h, b], val)
        return val

    tq = bp_ref.shape[2]
    key = lax.broadcasted_iota(jnp.int32, (tq, tq), 0)
    qry = lax.broadcasted_iota(jnp.int32, (tq, tq), 1)
    bp_ref[0, 0] = jnp.where(key <= qry, lookup(jnp.maximum(qry - key, 0)) * LOG2E, NEG_INF)
    bp_ref[0, 1] = lookup(qry - key + tq) * LOG2E

    rows, cols = bs_ref.shape[1], bs_ref.shape[2]
    r = lax.broadcasted_iota(jnp.int32, (rows, cols), 0)
    c = lax.broadcasted_iota(jnp.int32, (rows, cols), 1)
    t = r % t_dec
    tok = c // A_HEADS
    valid = (c % A_HEADS) == h
    far = jnp.full((rows, cols), tab_ref[h, REL_BUCKETS - 1], F32)
    bs_ref[0] = jnp.where(valid, far, NEG_INF)
    bs_ref[1] = jnp.where(valid, lookup(PAGE_SIZE + t - tok), NEG_INF)
    new_ok = valid & (tok <= t) & (tok < t_dec)
    bs_ref[2] = jnp.where(new_ok, lookup(jnp.maximum(t - tok, 0)), NEG_INF)


def _bias_tiles(table, t_dec):
    rows = 2 * t_dec
    return pl.pallas_call(
        functools.partial(_bias_kernel, t_dec=t_dec),
        grid=(A_HEADS,),
        in_specs=[pl.BlockSpec(memory_space=pltpu.SMEM)],
        out_specs=[pl.BlockSpec((1, 2, ATT_TQ, ATT_TQ), lambda h: (h, 0, 0, 0)),
                   pl.BlockSpec((3, rows, PAGE_SIZE * A_HEADS), lambda h: (0, h, 0))],
        out_shape=[jax.ShapeDtypeStruct((A_HEADS, 2, ATT_TQ, ATT_TQ), F32),
                   jax.ShapeDtypeStruct((3, A_HEADS * rows, PAGE_SIZE * A_HEADS), F32)],
        compiler_params=_cparams(1),
        name="bias_tiles",
    )(table.T)


def _lam(lamv_ref, lam_init):
    s1 = jnp.sum(lamv_ref[0:1, :] * lamv_ref[1:2, :], axis=1, keepdims=True)
    s2 = jnp.sum(lamv_ref[2:3, :] * lamv_ref[3:4, :], axis=1, keepdims=True)
    return jnp.exp(s1) - jnp.exp(s2) + lam_init


def _subln(o, g_ref, lam_init):
    y = o * lax.rsqrt(jnp.mean(o * o, axis=-1, keepdims=True) + EPS)
    return y * g_ref[...] * (1.0 - lam_init)


def _attn_prompt_kernel(tab_ref, q_ref, k_ref, v_ref, bias_ref, lamv_ref, subg_ref, o_ref,
                        kb, vt, m_s, acc_s, *, lam_init):
    hg = pl.program_id(1)
    qi = pl.program_id(2)
    tq = q_ref.shape[1]

    n_hd = kb.shape[0]
    d = A_V_DIM
    chains = [(hh, m) for hh in range(n_hd) for m in range(2)]

    @pl.when(qi == 0)
    def _():
        for hh in range(n_hd):
            hcols = slice(hh * 128, (hh + 1) * 128)
            kb[hh] = k_ref[0, :, hcols].astype(BF16)
            for jj in range(vt.shape[1]):
                vt[hh, jj, :d] = jnp.transpose(v_ref[0, jj * tq:(jj + 1) * tq, hcols]).astype(BF16)
                vt[hh, jj, d:] = jnp.ones((vt.shape[2] - d, tq), BF16)

    sub = lax.broadcasted_iota(jnp.int32, (128, tq), 0)
    qm = {}
    for hh in range(n_hd):
        qt = jnp.transpose(q_ref[0, :, hh * 128:(hh + 1) * 128]) * LOG2E
        qm[hh, 0] = jnp.where(sub < A_QK_DIM, qt, 0.0).astype(BF16)
        qm[hh, 1] = jnp.where(sub >= A_QK_DIM, qt, 0.0).astype(BF16)
    m_s[...] = jnp.full(m_s.shape, NEG_INF, F32)
    acc_s[...] = jnp.zeros(acc_s.shape, F32)

    def step(j, bias):
        start = pl.multiple_of(j * tq, tq)
        m_prev = [m_s[c] for c in range(len(chains))]
        ss = [_dot(kb[hh, pl.ds(start, tq), :], qm[hh, m]) + bias[hh] for hh, m in chains]
        m_new = [jnp.maximum(mp, jnp.max(s, axis=0, keepdims=True)) for mp, s in zip(m_prev, ss)]
        ps = [jnp.exp2(s - mn).astype(BF16) for s, mn in zip(ss, m_new)]
        for c, (hh, m) in enumerate(chains):
            acc_s[c] = jnp.exp2(m_prev[c] - m_new[c]) * acc_s[c] + _dot(vt[hh, j], ps[c])
            m_s[c] = m_new[c]

    far = [tab_ref[hg * n_hd + hh, REL_BUCKETS - 1] * LOG2E for hh in range(n_hd)]

    def far_body(j, carry):
        step(j, far)
        return carry

    lax.fori_loop(0, jnp.maximum(qi - 1, 0), far_body, 0)

    @pl.when(qi >= 1)
    def _():
        step(qi - 1, [bias_ref[hh, 1] for hh in range(n_hd)])

    step(qi, [bias_ref[hh, 0] for hh in range(n_hd)])

    lam = _lam(lamv_ref, lam_init)
    for hh in range(n_hd):
        a0, a1 = acc_s[2 * hh], acc_s[2 * hh + 1]
        o = a0[:d] * (1.0 / a0[d:d + 1]) - lam * (a1[:d] * (1.0 / a1[d:d + 1]))
        y = o * lax.rsqrt(jnp.mean(o * o, axis=0, keepdims=True) + EPS)
        y = y * subg_ref[...] * (1.0 - lam_init)
        o_ref[0, :, hh * 128:(hh + 1) * 128] = jnp.transpose(y).astype(o_ref.dtype)


def _attn_prompt(zr3, bias_p, tab_t, lamv, subg, lam_init):
    b, s, _ = zr3.shape
    tq = ATT_TQ
    n_hd = ATT_HEADS_PER_STEP
    wid = 128 * n_hd
    kern = functools.partial(_attn_prompt_kernel, lam_init=lam_init)
    return pl.pallas_call(
        kern,
        grid=(b, A_HEADS // n_hd, s // tq),
        in_specs=[pl.BlockSpec(memory_space=pltpu.SMEM),
                  pl.BlockSpec((1, tq, wid), lambda bi, hg, qi: (bi, qi, COL_AQ // wid + hg)),
                  pl.BlockSpec((1, s, wid), lambda bi, hg, qi: (bi, 0, COL_AK // wid + hg)),
                  pl.BlockSpec((1, s, wid), lambda bi, hg, qi: (bi, 0, COL_AV // wid + hg)),
                  pl.BlockSpec((n_hd, 2, tq, tq), lambda bi, hg, qi: (hg, 0, 0, 0)),
                  pl.BlockSpec((4, A_QK_DIM), lambda bi, hg, qi: (0, 0)),
                  pl.BlockSpec((A_V_DIM, 1), lambda bi, hg, qi: (0, 0))],
        out_specs=pl.BlockSpec((1, tq, wid), lambda bi, hg, qi: (bi, qi, hg)),
        out_shape=jax.ShapeDtypeStruct((b, s, A_WIDTH), BF16),
        scratch_shapes=[pltpu.VMEM((n_hd, s, 128), BF16),
                        pltpu.VMEM((n_hd, s // tq, A_V_DIM + ONES_ROWS, tq), BF16),
                        pltpu.VMEM((2 * n_hd, 1, tq), F32),
                        pltpu.VMEM((2 * n_hd, A_V_DIM + ONES_ROWS, tq), F32)],
        compiler_params=_cparams(3),
        name="attn_prompt",
    )(tab_t, zr3, zr3, zr3, bias_p, lamv, subg.reshape(A_V_DIM, 1))


def _attn_sample_init(q_ref, q_s, m_s, l_s, acc_s, t_dec):
    rows = 2 * t_dec
    q = q_ref[0]
    lane = lax.broadcasted_iota(jnp.int32, (t_dec, 128), 1)
    for h in range(A_HEADS):
        qh = q[:, h * 128:(h + 1) * 128]
        q_s[h * rows:h * rows + t_dec, :] = jnp.where(lane < A_QK_DIM, qh, 0.0)
        q_s[h * rows + t_dec:(h + 1) * rows, :] = jnp.where(lane >= A_QK_DIM, qh, 0.0)
    m_s[...] = jnp.full(m_s.shape, NEG_INF, F32)
    l_s[...] = jnp.zeros(l_s.shape, F32)
    acc_s[...] = jnp.zeros(acc_s.shape, F32)


def _attn_sample_update(q_s, m_s, l_s, acc_s, ks, vs, biases):
    qb = q_s[...].astype(BF16)
    ss = [_dot_nt(qb, kb) + bias for kb, bias in zip(ks, biases)]
    smax = functools.reduce(jnp.maximum, ss)
    m_prev = m_s[...]
    m_new = jnp.maximum(m_prev, jnp.max(smax, axis=1, keepdims=True))
    ps = [jnp.exp(s - m_new) for s in ss]
    alpha = jnp.exp(m_prev - m_new)
    l_s[...] = alpha * l_s[...] + jnp.sum(functools.reduce(jnp.add, ps), axis=1, keepdims=True)
    pv = functools.reduce(jnp.add, [_dot(p.astype(BF16), vb) for p, vb in zip(ps, vs)])
    acc_s[...] = alpha * acc_s[...] + pv
    m_s[...] = m_new


def _attn_sample_final(kn_ref, vn_ref, bias_ref, lamv_ref, subg_ref, o_ref, q_s, m_s, l_s, acc_s,
                       lam_init, t_dec):
    rows = 2 * t_dec
    nk = kn_ref.shape[1]
    _attn_sample_update(q_s, m_s, l_s, acc_s, [kn_ref[0].astype(BF16)], [vn_ref[0].astype(BF16)],
                        [bias_ref[2][:, :nk]])
    lam = _lam(lamv_ref, lam_init)
    on = acc_s[...] * (1.0 / l_s[...])
    for h in range(A_HEADS):
        o = on[h * rows:h * rows + t_dec] - lam * on[h * rows + t_dec:(h + 1) * rows]
        o_ref[0, :, h * 128:(h + 1) * 128] = _subln(o, subg_ref, lam_init).astype(o_ref.dtype)


def _ffn_attn_kernel(pt_ref, x_ref, g_ref, w1_ref, w2_ref, q_ref, kn_ref, vn_ref, bias_ref,
                     lamv_ref, subg_ref, *rest, lam_init, n_pg, t_dec):
    del pt_ref
    k_refs, v_refs = rest[:n_pg], rest[n_pg:2 * n_pg]
    o_ffn, o_att = rest[2 * n_pg:2 * n_pg + 2]
    h_s, acc_s, q_s, m_s, l_s, att_s = rest[2 * n_pg + 2:]
    j = pl.program_id(1)
    last = pl.num_programs(1) - 1

    @pl.when(j == 0)
    def _():
        _attn_sample_init(q_ref, q_s, m_s, l_s, att_s, t_dec)
        _ffn_init(x_ref, g_ref, h_s, acc_s)

    biases = [bias_ref[0]] * (n_pg - 1) + [jnp.where(j == last, bias_ref[1], bias_ref[0])]
    _attn_sample_update(q_s, m_s, l_s, att_s, [r[0, 0].astype(BF16) for r in k_refs],
                        [r[0, 0].astype(BF16) for r in v_refs], biases)
    _ffn_accumulate(w1_ref, w2_ref, h_s, acc_s)

    @pl.when(j == last)
    def _():
        _attn_sample_final(kn_ref, vn_ref, bias_ref, lamv_ref, subg_ref, o_att, q_s, m_s, l_s,
                           att_s, lam_init, t_dec)
        o_ffn[...] = x_ref[...] + acc_s[...]


def _ffn_attn(x, g, w1, w2, zrs3, kn2d, vn2d, cache_k2d, cache_v2d, page_table, bias_s, lamv,
              subg, layer, lam_init):
    m = x.shape[0]
    bd, t_dec, _ = zrs3.shape
    n_pages = page_table.shape[1]
    n_j = FFN_ATTN_STEPS
    assert m % bd == 0 and (m // bd) % 8 == 0 and n_pages % n_j == 0 and D_FF % n_j == 0
    tm, tf, n_pg = m // bd, D_FF // n_j, n_pages // n_j
    rows = 2 * t_dec * A_HEADS
    pcols = PAGE_SIZE * A_HEADS

    def page_spec(p):
        return pl.BlockSpec((1, 1, pcols, 128),
                            lambda i, j, pt: (layer, pt[i, j * n_pg + p], 0, 0))

    def per_seq(arr):
        return pl.BlockSpec((1,) + arr.shape[1:], lambda i, j, pt: (i, 0, 0))

    def const(shape):
        return pl.BlockSpec(shape, lambda i, j, pt: (0,) * len(shape))

    kern = functools.partial(_ffn_attn_kernel, lam_init=lam_init, n_pg=n_pg, t_dec=t_dec)
    grid_spec = pltpu.PrefetchScalarGridSpec(
        num_scalar_prefetch=1,
        grid=(bd, n_j),
        in_specs=[pl.BlockSpec((tm, D_MODEL), lambda i, j, pt: (i, 0)),
                  const((1, D_MODEL)),
                  pl.BlockSpec((D_MODEL, tf), lambda i, j, pt: (0, j)),
                  pl.BlockSpec((tf, D_MODEL), lambda i, j, pt: (j, 0)),
                  pl.BlockSpec((1, t_dec, A_WIDTH), lambda i, j, pt: (i, 0, 0)),
                  per_seq(kn2d), per_seq(vn2d),
                  const((3, rows, pcols)), const((4, A_QK_DIM)), const((1, A_V_DIM))]
                 + [page_spec(p) for p in range(n_pg)] * 2,
        out_specs=[pl.BlockSpec((tm, D_MODEL), lambda i, j, pt: (i, 0)),
                   pl.BlockSpec((1, t_dec, A_WIDTH), lambda i, j, pt: (i, 0, 0))],
        scratch_shapes=[pltpu.VMEM((tm, D_MODEL), BF16), pltpu.VMEM((tm, D_MODEL), F32),
                        pltpu.VMEM((rows, 128), F32), pltpu.VMEM((rows, 1), F32),
                        pltpu.VMEM((rows, 1), F32), pltpu.VMEM((rows, 128), F32)],
    )
    return pl.pallas_call(
        kern,
        grid_spec=grid_spec,
        out_shape=[jax.ShapeDtypeStruct((m, D_MODEL), F32),
                   jax.ShapeDtypeStruct((bd, t_dec, A_WIDTH), BF16)],
        compiler_params=_cparams(2),
        name="ffn_attn",
    )(page_table, x, g.reshape(1, D_MODEL), w1, w2, zrs3, kn2d, vn2d, bias_s, lamv, subg,
      *([cache_k2d] * n_pg), *([cache_v2d] * n_pg))


def _gelu(x):
    return 0.5 * x * (1.0 + jnp.tanh(math.sqrt(2.0 / math.pi) * (x + 0.044715 * (x * x * x))))


def _chunk_mlp_kernel(u_ref, v_ref, w_ref, bias_ref, g_ref, b_ref, y_ref, vb_ref):
    n_chunks = u_ref.shape[0] // B_CHUNK
    lane_grp = lax.broadcasted_iota(jnp.int32, (B_CHUNK, B_WIDTH), 1) // (B_WIDTH // B_GROUPS)
    for c in range(n_chunks):
        sl = slice(c * B_CHUNK, (c + 1) * B_CHUNK)
        gv = _gelu(v_ref[sl, :])
        xc = gv - jnp.mean(gv, axis=-1, keepdims=True)
        vb = xc * lax.rsqrt(jnp.mean(xc * xc, axis=-1, keepdims=True) + EPS) * g_ref[...] + b_ref[...]
        vb_ref[sl, :] = vb
        vbb = vb.astype(BF16)
        mixed = bias_ref[...]
        for g in range(B_GROUPS):
            mixed = mixed + jnp.where(lane_grp == g, _dot(w_ref[g], vbb), 0.0)
        y_ref[sl, :] = (_gelu(u_ref[sl, :]) * mixed).astype(y_ref.dtype)


def _chunk_mlp(zr, w_eff, bias_eff, ln_g, ln_b):
    m = zr.shape[0]
    tm = min(m, CHUNK_MLP_TM)
    return pl.pallas_call(
        _chunk_mlp_kernel,
        grid=(m // tm,),
        in_specs=[pl.BlockSpec((tm, B_WIDTH), lambda i: (i, COL_BU // B_WIDTH)),
                  pl.BlockSpec((tm, B_WIDTH), lambda i: (i, COL_BV // B_WIDTH)),
                  pl.BlockSpec((B_GROUPS, B_CHUNK, B_CHUNK), lambda i: (0, 0, 0)),
                  pl.BlockSpec((B_CHUNK, B_WIDTH), lambda i: (0, 0)),
                  pl.BlockSpec((1, B_WIDTH), lambda i: (0, 0)),
                  pl.BlockSpec((1, B_WIDTH), lambda i: (0, 0))],
        out_specs=[pl.BlockSpec((tm, B_WIDTH), lambda i: (i, 0)),
                   pl.BlockSpec((tm, B_WIDTH), lambda i: (i, 0))],
        out_shape=[jax.ShapeDtypeStruct((m, B_WIDTH), BF16),
                   jax.ShapeDtypeStruct((m, B_WIDTH), F32)],
        compiler_params=_cparams(1),
        name="chunk_mlp",
    )(zr, zr, w_eff, bias_eff, ln_g, ln_b)


def _pool_kernel(x_ref, w_ref, scale_ref, y_ref, *, prefix, pos0, seq_rows):
    x = x_ref[0]
    row = lax.broadcasted_iota(jnp.int32, x.shape, 0) % seq_rows
    grp = lax.broadcasted_iota(jnp.int32, x.shape, 1) // C_GROUP_DIM

    def shifted(a, k):
        return jnp.where(row >= k, pltpu.roll(a, k, 0), 0.0)

    sums = []
    acc = x
    for k in (1, 2, 4, 8):
        acc = acc + shifted(acc, k)
        sums.append(acc)
    total = sums[3]
    win = jnp.full(x.shape, POOL_WINDOWS[3], jnp.int32)
    for g in range(3):
        total = jnp.where(grp == g, sums[g], total)
        win = jnp.where(grp == g, POOL_WINDOWS[g], win)
    pos = pos0 + row - prefix
    cnt = jnp.clip(pos + 1, 1, win).astype(F32)
    d = total / cnt - x
    y = _dot(d.astype(BF16), w_ref[...]) * scale_ref[...]
    y_ref[0] = y.astype(y_ref.dtype)


def _pool_mix(xx, col_block, w_bd, scale, prefix, pos0, seq_rows):
    b, rows, _ = xx.shape
    kern = functools.partial(_pool_kernel, prefix=prefix, pos0=pos0, seq_rows=seq_rows)
    return pl.pallas_call(
        kern,
        grid=(b,),
        in_specs=[pl.BlockSpec((1, rows, C_WIDTH), lambda i: (i, 0, col_block)),
                  pl.BlockSpec((C_WIDTH, C_WIDTH), lambda i: (0, 0)),
                  pl.BlockSpec((1, C_WIDTH), lambda i: (0, 0))],
        out_specs=pl.BlockSpec((1, rows, C_WIDTH), lambda i: (i, 0, 0)),
        out_shape=jax.ShapeDtypeStruct((b, rows, C_WIDTH), BF16),
        compiler_params=_cparams(1),
        name="pool_mix",
    )(xx, w_bd, scale)


def _log_sigmoid(x):
    return jnp.minimum(x, 0.0) - jnp.log1p(jnp.exp(-jnp.abs(x)))


def _split3(x):
    def top8(a):
        bits = lax.bitcast_convert_type(a, jnp.int32) & jnp.int32(-65536)
        return lax.bitcast_convert_type(bits, F32)

    p1 = top8(x)
    r1 = x - p1
    p2 = top8(r1)
    return p1, p2, r1 - p2


def _pack3(pieces):
    return (pieces[0] + pltpu.roll(pieces[1], D_HEADS, 1) + pltpu.roll(pieces[2], 2 * D_HEADS, 1))


def _mlstm_kernel(q_ref, k_ref, v_ref, o_ref, gi_ref, gf_ref, gb_ref, ng_ref, sel_ref, bd_ref,
                  c0_ref, n0_ref, m0_ref, y_ref, c_out, n_out, m_out, c_s, n_s, m_s, *, t_valid):
    ci = pl.program_id(1)
    nb, chunk = q_ref.shape[0], q_ref.shape[1]
    hd = D_HEAD_DIM
    eye_h = (lax.broadcasted_iota(jnp.int32, (hd, hd), 0)
             == lax.broadcasted_iota(jnp.int32, (hd, hd), 1))

    @pl.when(ci == 0)
    def _():
        c_s[...] = jnp.zeros(c_s.shape, F32)
        n_s[...] = jnp.zeros(n_s.shape, F32)
        m_s[...] = m0_ref[...]
        for bi in range(nb):
            for h in range(D_HEADS):
                hs = slice(h * hd, (h + 1) * hd)
                c_s[bi, hs, hs] = c0_ref[bi, h]
                n_col = jnp.sum(jnp.where(eye_h, n0_ref[bi, h], 0.0), axis=1, keepdims=True)
                n_s[bi, hs, hs] = jnp.broadcast_to(n_col, (hd, hd))

    row = lax.broadcasted_iota(jnp.int32, (chunk, 128), 0)
    lane = lax.broadcasted_iota(jnp.int32, (chunk, 128), 1)
    head_lane = lane < D_HEADS
    grp = lax.broadcasted_iota(jnp.int32, (chunk, D_WIDTH), 1) // hd
    rr = lax.broadcasted_iota(jnp.int32, (chunk, chunk), 0)
    cc = lax.broadcasted_iota(jnp.int32, (chunk, chunk), 1)
    allowed = (cc <= rr) & (cc < t_valid)
    ones_w = jnp.ones((chunk, D_WIDTH), BF16)
    same_head = (lax.broadcasted_iota(jnp.int32, (D_WIDTH, D_WIDTH), 0) // hd
                 == lax.broadcasted_iota(jnp.int32, (D_WIDTH, D_WIDTH), 1) // hd)
    pick = [(lane % D_HEADS == h) & (lane < 6 * D_HEADS) for h in range(D_HEADS)]
    in_head = [grp == h for h in range(D_HEADS)]
    lower_ones = jnp.where(lane < 3 * D_HEADS, 1.0, 0.0)
    upper_ones = jnp.where((lane >= 3 * D_HEADS) & (lane < 6 * D_HEADS), 1.0, 0.0)
    head_one = [jnp.where(ih, 1.0, 0.0).astype(BF16) for ih in in_head]

    for bi in range(nb):
        gi = gi_ref[bi] + gb_ref[0:1, :]
        lf = _log_sigmoid(gf_ref[bi] + gb_ref[1:2, :])
        if t_valid < chunk:
            gi = jnp.where(row < t_valid, gi, NEG_INF)
            lf = jnp.where(row < t_valid, lf, 0.0)
        b = lf
        k = 1
        while k < chunk:
            b = b + jnp.where(row >= k, pltpu.roll(b, k, 0), 0.0)
            k *= 2
        u = gi - b
        cm = u
        k = 1
        while k < chunk:
            cm = jnp.maximum(cm, jnp.where(row >= k, pltpu.roll(cm, k, 0), NEG_INF))
            k *= 2
        m_prev = m_s[bi]
        big_m = jnp.maximum(m_prev, cm)
        m_last = big_m[chunk - 1:chunk, :]
        winter = jnp.exp(m_prev - big_m)
        emt = jnp.exp(jnp.minimum(-(b + big_m), EMT_CAP))
        ws = jnp.exp(u - m_last)
        m_s[bi] = b[chunk - 1:chunk, :] + m_last

        def per_head_lanes(z):
            packed = _pack3(_split3(jnp.where(head_lane, z, 0.0)))
            return _dot(packed.astype(BF16), sel_ref[...])

        winter_r, emt_r, ws_r = per_head_lanes(winter), per_head_lanes(emt), per_head_lanes(ws)
        decay_r = winter_r[chunk - 1:chunk, :]

        q = q_ref[bi]
        qb = q.astype(BF16)
        kf = k_ref[bi] * (hd ** -0.5)
        kb = kf.astype(BF16)
        vf = v_ref[bi]
        vb = vf.astype(BF16)
        u_fin = jnp.where(head_lane & (row < t_valid), u, 0.0)
        y_side = (_pack3(_split3(u_fin)) + upper_ones).astype(BF16)
        x_all = lower_ones + pltpu.roll(_pack3(_split3(jnp.where(head_lane, -big_m, 0.0))),
                                        3 * D_HEADS, 1)
        acc = jnp.zeros((chunk, 2 * D_WIDTH), F32)
        for h in range(D_HEADS):
            x_side = jnp.where(pick[h], x_all, 0.0).astype(BF16)
            expo = jnp.where(allowed, _dot_nt(x_side, y_side), NEG_INF)
            qk = _dot_nt(jnp.where(in_head[h], q, 0.0).astype(BF16), kb) * jnp.exp(expo)
            rhs = jnp.concatenate([jnp.where(in_head[h], vf, 0.0).astype(BF16), head_one[h]], axis=1)
            acc = acc + _dot(qk.astype(BF16), rhs)
        state = jnp.concatenate([c_s[bi].astype(BF16), n_s[bi].astype(BF16)], axis=1)
        inter = _dot(qb, state)
        num = winter_r * inter[:, :D_WIDTH] + acc[:, :D_WIDTH]
        den = winter_r * inter[:, D_WIDTH:] + acc[:, D_WIDTH:]
        hh = num / jnp.maximum(jnp.abs(den), emt_r)
        h2 = hh * hh
        hi = h2.astype(BF16)
        lo = (h2 - hi.astype(F32)).astype(BF16)
        ssq = _dot(hi, bd_ref[...]) + _dot(lo, bd_ref[...])
        y = hh * lax.rsqrt(ssq * (1.0 / hd) + EPS) * ng_ref[...]
        y_ref[bi] = (y * jax.nn.sigmoid(o_ref[bi])).astype(y_ref.dtype)
        kw = (ws_r * kf).astype(BF16)
        upd = _dot_tn(kw, jnp.concatenate([vb, ones_w], axis=1))
        c_s[bi] = decay_r * c_s[bi] + jnp.where(same_head, upd[:, :D_WIDTH], 0.0)
        n_s[bi] = decay_r * n_s[bi] + jnp.where(same_head, upd[:, D_WIDTH:], 0.0)

    @pl.when(ci == pl.num_programs(1) - 1)
    def _():
        m_out[...] = m_s[...]
        for bi in range(nb):
            for h in range(D_HEADS):
                hs = slice(h * hd, (h + 1) * hd)
                c_out[bi, h] = c_s[bi, hs, hs]
                n_out[bi, h] = jnp.sum(jnp.where(eye_h, n_s[bi, hs, hs], 0.0), axis=0, keepdims=True)


def _mlstm(src, col0, gate_block, chunk, t_valid, gate_bias, norm_g, c0, n0, m0):
    b, t, _ = src.shape
    nb = math.gcd(b, MLSTM_BATCH)
    hd = D_HEAD_DIM
    kern = functools.partial(_mlstm_kernel, t_valid=t_valid)
    head_of_lane = jnp.arange(D_WIDTH) // hd
    src_lane = jnp.arange(128)[:, None]
    sel3 = ((src_lane % D_HEADS == head_of_lane[None, :]) & (src_lane < 3 * D_HEADS)).astype(BF16)
    same_head = (head_of_lane[:, None] == head_of_lane[None, :]).astype(BF16)
    m0p = jnp.pad(m0, ((0, 0), (0, 0), (0, 128 - D_HEADS)))

    def col(cb, width=D_WIDTH):
        return pl.BlockSpec((nb, chunk, width), lambda bi, ci: (bi, ci, cb))

    def const(shape):
        return pl.BlockSpec(shape, lambda bi, ci: (0,) * len(shape))

    def per_seq(shape):
        return pl.BlockSpec((nb,) + shape, lambda bi, ci: (bi,) + (0,) * len(shape))

    y, c_new, n_new, m_new = pl.pallas_call(
        kern,
        grid=(b // nb, t // chunk),
        in_specs=[col(col0), col(col0 + 1), col(col0 + 2), col(col0 + 3),
                  col(gate_block, 128), col(gate_block + 1, 128),
                  const((2, 128)), const((1, D_WIDTH)), const((128, D_WIDTH)),
                  const((D_WIDTH, D_WIDTH)),
                  per_seq((D_HEADS, hd, hd)), per_seq((D_HEADS, 1, hd)), per_seq((1, 128))],
        out_specs=[col(0), per_seq((D_HEADS, hd, hd)), per_seq((D_HEADS, 1, hd)),
                   per_seq((1, 128))],
        out_shape=[jax.ShapeDtypeStruct((b, t, D_WIDTH), BF16),
                   jax.ShapeDtypeStruct((b, D_HEADS, hd, hd), F32),
                   jax.ShapeDtypeStruct((b, D_HEADS, 1, hd), F32),
                   jax.ShapeDtypeStruct((b, 1, 128), F32)],
        scratch_shapes=[pltpu.VMEM((nb, D_WIDTH, D_WIDTH), F32),
                        pltpu.VMEM((nb, D_WIDTH, D_WIDTH), F32),
                        pltpu.VMEM((nb, 1, 128), F32)],
        compiler_params=_cparams(2),
        name="mlstm",
    )(src, src, src, src, src, src, gate_bias, jnp.tile(norm_g, (1, D_HEADS)), sel3, same_head,
      c0, n0, m0p)
    return y, c_new, n_new, m_new[:, :, :D_HEADS]


def _merge_kernel(x_ref, g_ref, ya_ref, yb_ref, yc_ref, yd_ref, wa_ref, wb_ref, wc_ref, wd_ref,
                  wo_ref, o_ref):
    def gate(i):
        return g_ref[:, i * D_MODEL:(i + 1) * D_MODEL].astype(F32)

    merged = gate(0) * _dot(ya_ref[...], wa_ref[...])
    merged = merged + gate(1) * _dot(yb_ref[...], wb_ref[...])
    merged = merged + gate(2) * _dot(yc_ref[...], wc_ref[...])
    merged = merged + gate(3) * _dot(yd_ref[...], wd_ref[...])
    o_ref[...] = x_ref[...] + _dot(merged.astype(BF16), wo_ref[...])


def _merge(x, gates, ya, yb, yc, yd, wa, wb, wc, wd, wo):
    m = x.shape[0]
    tm = min(m, MERGE_TM)

    def rows(width):
        return pl.BlockSpec((tm, width), lambda i: (i, 0))

    def full(arr):
        return pl.BlockSpec(arr.shape, lambda i: (0, 0))

    return pl.pallas_call(
        _merge_kernel,
        grid=(m // tm,),
        in_specs=[rows(D_MODEL), rows(GZ_WIDTH), rows(A_WIDTH), rows(B_WIDTH), rows(C_WIDTH),
                  rows(D_WIDTH), full(wa), full(wb), full(wc), full(wd), full(wo)],
        out_specs=rows(D_MODEL),
        out_shape=jax.ShapeDtypeStruct((m, D_MODEL), F32),
        compiler_params=_cparams(1),
        name="merge",
    )(x, gates, ya, yb, yc, yd, wa, wb, wc, wd, wo)


def _ffn_init(x_ref, g_ref, h_s, acc_s):
    x = x_ref[...]
    y = x * lax.rsqrt(jnp.mean(x * x, axis=-1, keepdims=True) + EPS)
    h_s[...] = (y * g_ref[...]).astype(BF16)
    acc_s[...] = jnp.zeros(acc_s.shape, F32)


def _ffn_accumulate(w1_ref, w2_ref, h_s, acc_s):
    a = jnp.maximum(_dot(h_s[...], w1_ref[...]), 0.0)
    acc_s[...] += _dot((a * a).astype(BF16), w2_ref[...])


def _ffn_kernel(x_ref, g_ref, w1_ref, w2_ref, o_ref, h_s, acc_s):
    j = pl.program_id(1)

    @pl.when(j == 0)
    def _():
        _ffn_init(x_ref, g_ref, h_s, acc_s)

    _ffn_accumulate(w1_ref, w2_ref, h_s, acc_s)

    @pl.when(j == pl.num_programs(1) - 1)
    def _():
        o_ref[...] = x_ref[...] + acc_s[...]


def _ffn(x, g, w1, w2):
    m = x.shape[0]
    tm, tf = min(m, 1024), 1024
    return pl.pallas_call(
        _ffn_kernel,
        grid=(m // tm, D_FF // tf),
        in_specs=[pl.BlockSpec((tm, D_MODEL), lambda i, j: (i, 0)),
                  pl.BlockSpec((1, D_MODEL), lambda i, j: (0, 0)),
                  pl.BlockSpec((D_MODEL, tf), lambda i, j: (0, j)),
                  pl.BlockSpec((tf, D_MODEL), lambda i, j: (j, 0))],
        out_specs=pl.BlockSpec((tm, D_MODEL), lambda i, j: (i, 0)),
        out_shape=jax.ShapeDtypeStruct((m, D_MODEL), F32),
        scratch_shapes=[pltpu.VMEM((tm, D_MODEL), BF16), pltpu.VMEM((tm, D_MODEL), F32)],
        compiler_params=_cparams(2),
        name="ffn",
    )(x, g.reshape(1, D_MODEL), w1, w2)


def _layer_weights(p, l):
    w_in = p["w_in"][l]
    gate_pad = jnp.zeros((D_MODEL, 128 - D_HEADS), F32)
    wr = jnp.concatenate(
        [w_in[:, :3072], w_in[:, 3080:3336], w_in[:, 3072:3076], gate_pad,
         w_in[:, 3076:3080], gate_pad], axis=1).astype(BF16)
    wg = w_in[:, 3336:].astype(BF16)
    reps = A_WIDTH // A_QK_DIM
    gains = jnp.stack([jnp.tile(p["q_norm_g"][l], reps) * (A_QK_DIM ** -0.5),
                       jnp.tile(p["k_norm_g"][l], reps)]).reshape(2, 1, A_WIDTH)
    grp = jnp.arange(A_WIDTH) // A_QK_DIM
    ones_bd = (grp[:, None] == grp[None, :]).astype(BF16)
    lamv = jnp.stack([p["lam_q1"][l], p["lam_k1"][l], p["lam_q2"][l], p["lam_k2"][l]])
    gate_bias = jnp.pad(jnp.stack([p["d_i_bias"][l], p["d_f_bias"][l]]),
                        ((0, 0), (0, 128 - D_HEADS)))
    c_bd = jnp.zeros((C_WIDTH, C_WIDTH), F32)
    for g in range(4):
        sl = slice(g * C_GROUP_DIM, (g + 1) * C_GROUP_DIM)
        c_bd = c_bd.at[sl, sl].set(p["c_lin"][l][g])
    return dict(
        norm1_g=p["norm1_g"][l], norm2_g=p["norm2_g"][l], wr=wr, wg=wg, gains=gains,
        ones_bd=ones_bd, lamv=lamv, subg=p["subln_g"][l].reshape(1, A_V_DIM),
        b_ln_g=p["b_ln_g"][l].reshape(1, B_WIDTH), b_ln_b=p["b_ln_b"][l].reshape(1, B_WIDTH),
        b_ws=p["b_ws"][l], b_bias=p["b_bias"][l],
        c_bd=c_bd.astype(BF16), c_scale=p["c_scale"][l].reshape(1, C_WIDTH),
        gate_bias=gate_bias, d_norm_g=p["d_norm_g"][l].reshape(1, D_HEAD_DIM),
        w_pa=p["w_pa"][l].astype(BF16), w_pb=p["w_pb"][l].astype(BF16),
        w_pc=p["w_pc"][l].astype(BF16), w_pd=p["w_pd"][l].astype(BF16),
        w_out=p["w_out"][l].astype(BF16), w_ff1=p["w_ff1"][l].astype(BF16),
        w_ff2=p["w_ff2"][l].astype(BF16),
        lam_init=0.8 - 0.6 * math.exp(-0.3 * l), layer=l,
    )


def _chunk_weights(w, t):
    length = min(t, B_CHUNK)
    ws = jnp.tril(w["b_ws"][:, :length, :length])
    bias = jnp.transpose(w["b_bias"][:, :length])
    reps = B_CHUNK // length
    if reps > 1:
        eye = jnp.eye(reps, dtype=F32)
        ws = jax.vmap(lambda a: jnp.kron(eye, a))(ws)
        bias = jnp.tile(bias, (reps, 1))
    return ws.astype(BF16), jnp.repeat(bias, B_WIDTH // B_GROUPS, axis=1)


def _mix_and_merge(x2, w, gates, zr, ya, yc, yd, t):
    w_eff, bias_eff = _chunk_weights(w, t)
    yb, vb = _chunk_mlp(zr, w_eff, bias_eff, w["b_ln_g"], w["b_ln_b"])
    x2 = _merge(x2, gates, ya, yb, yc, yd, w["w_pa"], w["w_pb"], w["w_pc"], w["w_pd"], w["w_out"])
    return x2, vb


def _prompt_layer(x2, w, b, s, bias_p, tab_t, sample_attn):
    m = b * s
    gates, zr, k_rows, v_rows = _in_proj(x2, w["norm1_g"], w["wg"], w["wr"], w["gains"],
                                         w["ones_bd"])
    zr3 = zr.reshape(b, s, ZR_WIDTH)
    ya = _attn_prompt(zr3, bias_p, tab_t, w["lamv"], w["subg"], w["lam_init"]).reshape(m, A_WIDTH)
    yc = _pool_mix(zr3, COL_CX // C_WIDTH, w["c_bd"], w["c_scale"], 0, 0, s).reshape(m, C_WIDTH)
    hd = D_HEAD_DIM
    yd, c_new, n_new, m_new = _mlstm(
        zr3, COL_DQ // D_WIDTH, COL_DGI // 128, min(s, MLSTM_CHUNK), min(s, MLSTM_CHUNK),
        w["gate_bias"],
        w["d_norm_g"], jnp.zeros((b, D_HEADS, hd, hd), F32), jnp.zeros((b, D_HEADS, 1, hd), F32),
        jnp.zeros((b, 1, D_HEADS), F32))
    x2, _ = _mix_and_merge(x2, w, gates, zr, ya, yc, yd.reshape(m, D_WIDTH), s)
    x2, ya_sample = _ffn_attn(x2, w["norm2_g"], w["w_ff1"], w["w_ff2"], *sample_attn,
                              w["lamv"], w["subg"], w["layer"], w["lam_init"])
    outs = (k_rows.reshape(b, s, A_HEADS, 2 * A_QK_DIM), v_rows.reshape(b, s, A_HEADS, A_V_DIM),
            zr3[:, s - POOL_BUF:, COL_CX:COL_CX + C_WIDTH],
            c_new, n_new.reshape(b, D_HEADS, hd), m_new.reshape(b, D_HEADS))
    return x2, outs, ya_sample


def _sample_layer(x2, w, bd, t, proj, ya, past, pool0, c0, n0, m0):
    m = bd * t
    gates, zr, kn, vn = proj
    zr3 = zr.reshape(bd, t, ZR_WIDTH)
    ya = ya.reshape(m, A_WIDTH)
    cx = zr3[:, :, COL_CX:COL_CX + C_WIDTH]
    prefix = POOL_BUF + 1
    rows = -(-(prefix + t) // 8) * 8
    xx = jnp.concatenate([jnp.zeros((bd, 1, C_WIDTH), F32), pool0, cx,
                          jnp.zeros((bd, rows - prefix - t, C_WIDTH), F32)], axis=1)
    yc = _pool_mix(xx.reshape(1, bd * rows, C_WIDTH), 0, w["c_bd"], w["c_scale"], prefix, past, rows)
    yc = yc.reshape(bd, rows, C_WIDTH)[:, prefix:prefix + t].reshape(m, C_WIDTH)
    chunk = -(-t // SAMPLE_MLSTM_CHUNK) * SAMPLE_MLSTM_CHUNK
    dsrc = jnp.pad(zr3[:, :, COL_DQ:COL_DGF + 128], ((0, 0), (0, chunk - t), (0, 0)))
    hd = D_HEAD_DIM
    yd, c_new, n_new, m_new = _mlstm(
        dsrc, 0, (COL_DGI - COL_DQ) // 128, chunk, t, w["gate_bias"], w["d_norm_g"],
        c0, n0.reshape(bd, D_HEADS, 1, hd), m0.reshape(bd, 1, D_HEADS))
    yd = yd[:, :t].reshape(m, D_WIDTH)
    x2, vb = _mix_and_merge(x2, w, gates, zr, ya, yc, yd, t)
    x2 = _ffn(x2, w["norm2_g"], w["w_ff1"], w["w_ff2"])
    outs = (kn.reshape(bd, t, A_HEADS, 2 * A_QK_DIM), vn.reshape(bd, t, A_HEADS, A_V_DIM),
            vb.reshape(bd, t, B_WIDTH), jnp.concatenate([pool0, cx], axis=1)[:, -POOL_BUF:],
            c_new, n_new.reshape(bd, D_HEADS, hd), m_new.reshape(bd, D_HEADS))
    return x2, outs


def kernel(x_prompt, x_sample, cache_k, cache_v, page_table, state_pool, state_C, state_n, state_m, rel_bias, norm1_g, norm2_g, w_in, q_norm_g, k_norm_g, lam_q1, lam_k1, lam_q2, lam_k2, subln_g, b_ln_g, b_ln_b, b_ws, b_bias, c_lin, c_scale, d_i_bias, d_f_bias, d_norm_g, w_pa, w_pb, w_pc, w_pd, w_out, w_ff1, w_ff2):
    p = dict(norm1_g=norm1_g, norm2_g=norm2_g, w_in=w_in, q_norm_g=q_norm_g, k_norm_g=k_norm_g,
             lam_q1=lam_q1, lam_k1=lam_k1, lam_q2=lam_q2, lam_k2=lam_k2, subln_g=subln_g,
             b_ln_g=b_ln_g, b_ln_b=b_ln_b, b_ws=b_ws, b_bias=b_bias, c_lin=c_lin, c_scale=c_scale,
             d_i_bias=d_i_bias, d_f_bias=d_f_bias, d_norm_g=d_norm_g, w_pa=w_pa, w_pb=w_pb,
             w_pc=w_pc, w_pd=w_pd, w_out=w_out, w_ff1=w_ff1, w_ff2=w_ff2)
    depth = w_in.shape[0]
    bp, sp, _ = x_prompt.shape
    bd, td, _ = x_sample.shape
    n_phys = cache_k.shape[1]
    cache_k2d = cache_k.reshape(depth, n_phys, PAGE_SIZE * A_HEADS, 128)
    cache_v2d = cache_v.reshape(depth, n_phys, PAGE_SIZE * A_HEADS, 128)
    bias_p, bias_s = _bias_tiles(rel_bias, td)
    tab_t = rel_bias.T

    xp = x_prompt.reshape(bp * sp, D_MODEL)
    xs = x_sample.reshape(bd * td, D_MODEL)
    prompt_outs, sample_outs = [], []
    for l in range(depth):
        w = _layer_weights(p, l)
        proj = _in_proj(xs, w["norm1_g"], w["wg"], w["wr"], w["gains"], w["ones_bd"])
        sample_attn = (proj[1].reshape(bd, td, ZR_WIDTH),
                       proj[2].reshape(bd, td * A_HEADS, 128), proj[3].reshape(bd, td * A_HEADS, 128),
                       cache_k2d, cache_v2d, page_table, bias_s)
        xp, po, ya_sample = _prompt_layer(xp, w, bp, sp, bias_p, tab_t, sample_attn)
        xs, so = _sample_layer(xs, w, bd, td, proj, ya_sample, page_table.shape[1] * PAGE_SIZE,
                               state_pool[l], state_C[l], state_n[l], state_m[l])
        prompt_outs.append(po)
        sample_outs.append(so)

    def stack(outs, i):
        return jnp.stack([o[i] for o in outs])

    return (xp.reshape(bp, sp, D_MODEL), xs.reshape(bd, td, D_MODEL),
            stack(prompt_outs, 0), stack(prompt_outs, 1), stack(sample_outs, 0), stack(sample_outs, 1),
            stack(sample_outs, 2), stack(prompt_outs, 2), stack(sample_outs, 3),
            stack(prompt_outs, 3), stack(prompt_outs, 4), stack(prompt_outs, 5),
            stack(sample_outs, 4), stack(sample_outs, 5), stack(sample_outs, 6))
```

```python
import functools
import math

import jax
import jax.numpy as jnp
from jax import lax
from jax.experimental import pallas as pl
from jax.experimental.pallas import tpu as pltpu

F32 = jnp.float32
BF16 = jnp.bfloat16
NEG_INF = float("-inf")

D_MODEL = 1024
A_HEADS = 4
A_QK_DIM = 64
A_V_DIM = 128
A_WIDTH = 512
REL_BUCKETS = 32
REL_MAX_DIST = 128
PAGE_SIZE = 128
B_GROUPS = 4
B_WIDTH = 256
B_CHUNK = 128
C_WIDTH = 256
C_GROUP_DIM = 64
POOL_WINDOWS = (2, 4, 8, 16)
POOL_BUF = 15
D_HEADS = 4
D_WIDTH = 256
D_HEAD_DIM = 64
D_FF = 4096
N_BRANCH = 4
EPS = 1e-6

ZR_WIDTH = 3584
COL_AQ, COL_AK, COL_AV = 0, 512, 1024
COL_BU, COL_BV, COL_CX = 1536, 1792, 2048
COL_DQ, COL_DK, COL_DV, COL_DO = 2304, 2560, 2816, 3072
COL_DGI, COL_DGF = 3328, 3456
GZ_WIDTH = N_BRANCH * D_MODEL
EMT_CAP = 80.0
LOG2E = math.log2(math.e)
ONES_ROWS = 16

IN_PROJ_TM = 512
IN_PROJ_TN = 512
CHUNK_MLP_TM = 1024
ATT_TQ = 512
ATT_HEADS_PER_STEP = 4
MERGE_TM = 1024
FFN_ATTN_STEPS = 4
MLSTM_BATCH = 4
MLSTM_CHUNK = 256
SAMPLE_MLSTM_CHUNK = 32
VMEM_LIMIT = 56 * 1024 * 1024


def _cparams(n_axes):
    return pltpu.CompilerParams(dimension_semantics=("arbitrary",) * n_axes,
                                vmem_limit_bytes=VMEM_LIMIT)


def _dot(a, b):
    return jnp.dot(a, b, preferred_element_type=F32)


def _dot_nt(a, b):
    return lax.dot_general(a, b, (((1,), (1,)), ((), ())), preferred_element_type=F32)


def _dot_tn(a, b):
    return lax.dot_general(a, b, (((0,), (0,)), ((), ())), preferred_element_type=F32)


def _in_proj_kernel(x_ref, g_ref, wg_ref, wr_ref, gain_ref, ones_ref,
                    gates_ref, zr_ref, kout_ref, vout_ref):
    tm = x_ref.shape[0]
    tn = IN_PROJ_TN
    x = x_ref[...]
    h = (x * lax.rsqrt(jnp.mean(x * x, axis=-1, keepdims=True) + EPS) * g_ref[...]).astype(BF16)

    def head_rows(dst_ref, val):
        for hh in range(A_HEADS):
            dst_ref[pl.ds(hh, tm, stride=A_HEADS), :] = val[:, hh * 128:(hh + 1) * 128]

    for c in range(ZR_WIDTH // tn):
        cols = slice(c * tn, (c + 1) * tn)
        z = _dot(h, wr_ref[:, cols])
        if c * tn in (COL_AQ, COL_AK):
            z2 = z * z
            hi = z2.astype(BF16)
            lo = (z2 - hi.astype(F32)).astype(BF16)
            ssq = _dot(hi, ones_ref[...]) + _dot(lo, ones_ref[...])
            z = z * lax.rsqrt(ssq * (1.0 / A_QK_DIM) + EPS) * gain_ref[c]
        zr_ref[:, cols] = z
        if c * tn == COL_AK:
            head_rows(kout_ref, z)
        if c * tn == COL_AV:
            head_rows(vout_ref, z)

    for c in range(GZ_WIDTH // tn):
        cols = slice(c * tn, (c + 1) * tn)
        gates_ref[:, cols] = jax.nn.sigmoid(_dot(h, wg_ref[:, cols])).astype(gates_ref.dtype)


def _in_proj(x, g, wg, wr, gains, ones_bd):
    m = x.shape[0]
    tm = min(m, IN_PROJ_TM)

    def rows(width):
        return pl.BlockSpec((tm, width), lambda i: (i, 0))

    def resident(arr):
        return pl.BlockSpec(arr.shape, lambda i: (0,) * arr.ndim, pipeline_mode=pl.Buffered(1))

    kv_spec = pl.BlockSpec((tm * A_HEADS, 128), lambda i: (i, 0))
    kv_shape = jax.ShapeDtypeStruct((m * A_HEADS, 128), F32)
    g = g.reshape(1, D_MODEL)
    return pl.pallas_call(
        _in_proj_kernel,
        grid=(m // tm,),
        in_specs=[rows(D_MODEL), resident(g), resident(wg), resident(wr), resident(gains),
                  resident(ones_bd)],
        out_specs=[rows(GZ_WIDTH), rows(ZR_WIDTH), kv_spec, kv_spec],
        out_shape=[jax.ShapeDtypeStruct((m, GZ_WIDTH), BF16),
                   jax.ShapeDtypeStruct((m, ZR_WIDTH), F32), kv_shape, kv_shape],
        compiler_params=_cparams(1),
        name="in_proj",
    )(x, g, wg, wr, gains, ones_bd)


def _bucket(n):
    max_exact = REL_BUCKETS // 2
    large = max_exact + (jnp.log(jnp.maximum(n, 1).astype(F32) / max_exact)
                         / math.log(REL_MAX_DIST / max_exact)
                         * (REL_BUCKETS - max_exact)).astype(jnp.int32)
    return jnp.where(n < max_exact, n, jnp.minimum(large, REL_BUCKETS - 1))


def _bias_kernel(tab_ref, bp_ref, bs_ref, *, t_dec):
    h = pl.program_id(0)

    def lookup(n):
        bucket = _bucket(n)
        val = jnp.full(n.shape, tab_ref[h, REL_BUCKETS - 1], F32)
        for b in range(REL_BUCKETS - 1):
            val = jnp.where(bucket == b, tab_ref[h, b], val)
        return val

    tq = bp_ref.shape[2]
    key = lax.broadcasted_iota(jnp.int32, (tq, tq), 0)
    qry = lax.broadcasted_iota(jnp.int32, (tq, tq), 1)
    bp_ref[0, 0] = jnp.where(key <= qry, lookup(jnp.maximum(qry - key, 0)) * LOG2E, NEG_INF)
    bp_ref[0, 1] = lookup(qry - key + tq) * LOG2E

    rows, cols = bs_ref.shape[1], bs_ref.shape[2]
    r = lax.broadcasted_iota(jnp.int32, (rows, cols), 0)
    c = lax.broadcasted_iota(jnp.int32, (rows, cols), 1)
    t = r % t_dec
    tok = c // A_HEADS
    valid = (c % A_HEADS) == h
    far = jnp.full((rows, cols), tab_ref[h, REL_BUCKETS - 1], F32)
    bs_ref[0] = jnp.where(valid, far, NEG_INF)
    bs_ref[1] = jnp.where(valid, lookup(PAGE_SIZE + t - tok), NEG_INF)
    new_ok = valid & (tok <= t) & (tok < t_dec)
    bs_ref[2] = jnp.where(new_ok, lookup(jnp.maximum(t - tok, 0)), NEG_INF)


def _bias_tiles(table, t_dec):
    rows = 2 * t_dec
    return pl.pallas_call(
        functools.partial(_bias_kernel, t_dec=t_dec),
        grid=(A_HEADS,),
        in_specs=[pl.BlockSpec(memory_space=pltpu.SMEM)],
        out_specs=[pl.BlockSpec((1, 2, ATT_TQ, ATT_TQ), lambda h: (h, 0, 0, 0)),
                   pl.BlockSpec((3, rows, PAGE_SIZE * A_HEADS), lambda h: (0, h, 0))],
        out_shape=[jax.ShapeDtypeStruct((A_HEADS, 2, ATT_TQ, ATT_TQ), F32),
                   jax.ShapeDtypeStruct((3, A_HEADS * rows, PAGE_SIZE * A_HEADS), F32)],
        compiler_params=_cparams(1),
        name="bias_tiles",
    )(table.T)


def _lam(lamv_ref, lam_init):
    s1 = jnp.sum(lamv_ref[0:1, :] * lamv_ref[1:2, :], axis=1, keepdims=True)
    s2 = jnp.sum(lamv_ref[2:3, :] * lamv_ref[3:4, :], axis=1, keepdims=True)
    return jnp.exp(s1) - jnp.exp(s2) + lam_init


def _subln(o, g_ref, lam_init):
    y = o * lax.rsqrt(jnp.mean(o * o, axis=-1, keepdims=True) + EPS)
    return y * g_ref[...] * (1.0 - lam_init)


def _attn_prompt_kernel(tab_ref, q_ref, k_ref, v_ref, bias_ref, lamv_ref, subg_ref, o_ref,
                        kb, vt, m_s, acc_s, *, lam_init):
    hg = pl.program_id(1)
    qi = pl.program_id(2)
    tq = q_ref.shape[1]

    n_hd = kb.shape[0]
    d = A_V_DIM
    chains = [(hh, m) for hh in range(n_hd) for m in range(2)]

    @pl.when(qi == 0)
    def _():
        for hh in range(n_hd):
            hcols = slice(hh * 128, (hh + 1) * 128)
            kb[hh] = k_ref[0, :, hcols].astype(BF16)
            for jj in range(vt.shape[1]):
                vt[hh, jj, :d] = jnp.transpose(v_ref[0, jj * tq:(jj + 1) * tq, hcols]).astype(BF16)
                vt[hh, jj, d:] = jnp.ones((vt.shape[2] - d, tq), BF16)

    sub = lax.broadcasted_iota(jnp.int32, (128, tq), 0)
    qm = {}
    for hh in range(n_hd):
        qt = jnp.transpose(q_ref[0, :, hh * 128:(hh + 1) * 128]) * LOG2E
        qm[hh, 0] = jnp.where(sub < A_QK_DIM, qt, 0.0).astype(BF16)
        qm[hh, 1] = jnp.where(sub >= A_QK_DIM, qt, 0.0).astype(BF16)
    m_s[...] = jnp.full(m_s.shape, NEG_INF, F32)
    acc_s[...] = jnp.zeros(acc_s.shape, F32)

    def step(j, bias):
        start = pl.multiple_of(j * tq, tq)
        m_prev = [m_s[c] for c in range(len(chains))]
        ss = [_dot(kb[hh, pl.ds(start, tq), :], qm[hh, m]) + bias[hh] for hh, m in chains]
        m_new = [jnp.maximum(mp, jnp.max(s, axis=0, keepdims=True)) for mp, s in zip(m_prev, ss)]
        ps = [jnp.exp2(s - mn).astype(BF16) for s, mn in zip(ss, m_new)]
        for c, (hh, m) in enumerate(chains):
            acc_s[c] = jnp.exp2(m_prev[c] - m_new[c]) * acc_s[c] + _dot(vt[hh, j], ps[c])
            m_s[c] = m_new[c]

    far = [tab_ref[hg * n_hd + hh, REL_BUCKETS - 1] * LOG2E for hh in range(n_hd)]

    def far_body(j, carry):
        step(j, far)
        return carry

    lax.fori_loop(0, jnp.maximum(qi - 1, 0), far_body, 0)

    @pl.when(qi >= 1)
    def _():
        step(qi - 1, [bias_ref[hh, 1] for hh in range(n_hd)])

    step(qi, [bias_ref[hh, 0] for hh in range(n_hd)])

    lam = _lam(lamv_ref, lam_init)
    for hh in range(n_hd):
        a0, a1 = acc_s[2 * hh], acc_s[2 * hh + 1]
        o = a0[:d] * (1.0 / a0[d:d + 1]) - lam * (a1[:d] * (1.0 / a1[d:d + 1]))
        y = o * lax.rsqrt(jnp.mean(o * o, axis=0, keepdims=True) + EPS)
        y = y * subg_ref[...] * (1.0 - lam_init)
        o_ref[0, :, hh * 128:(hh + 1) * 128] = jnp.transpose(y).astype(o_ref.dtype)


def _attn_prompt(zr3, bias_p, tab_t, lamv, subg, lam_init):
    b, s, _ = zr3.shape
    tq = ATT_TQ
    n_hd = ATT_HEADS_PER_STEP
    wid = 128 * n_hd
    kern = functools.partial(_attn_prompt_kernel, lam_init=lam_init)
    return pl.pallas_call(
        kern,
        grid=(b, A_HEADS // n_hd, s // tq),
        in_specs=[pl.BlockSpec(memory_space=pltpu.SMEM),
                  pl.BlockSpec((1, tq, wid), lambda bi, hg, qi: (bi, qi, COL_AQ // wid + hg)),
                  pl.BlockSpec((1, s, wid), lambda bi, hg, qi: (bi, 0, COL_AK // wid + hg)),
                  pl.BlockSpec((1, s, wid), lambda bi, hg, qi: (bi, 0, COL_AV // wid + hg)),
                  pl.BlockSpec((n_hd, 2, tq, tq), lambda bi, hg, qi: (hg, 0, 0, 0)),
                  pl.BlockSpec((4, A_QK_DIM), lambda bi, hg, qi: (0, 0)),
                  pl.BlockSpec((A_V_DIM, 1), lambda bi, hg, qi: (0, 0))],
        out_specs=pl.BlockSpec((1, tq, wid), lambda bi, hg, qi: (bi, qi, hg)),
        out_shape=jax.ShapeDtypeStruct((b, s, A_WIDTH), BF16),
        scratch_shapes=[pltpu.VMEM((n_hd, s, 128), BF16),
                        pltpu.VMEM((n_hd, s // tq, A_V_DIM + ONES_ROWS, tq), BF16),
                        pltpu.VMEM((2 * n_hd, 1, tq), F32),
                        pltpu.VMEM((2 * n_hd, A_V_DIM + ONES_ROWS, tq), F32)],
        compiler_params=_cparams(3),
        name="attn_prompt",
    )(tab_t, zr3, zr3, zr3, bias_p, lamv, subg.reshape(A_V_DIM, 1))


def _attn_sample_init(q_ref, q_s, m_s, l_s, acc_s, t_dec):
    rows = 2 * t_dec
    q = q_ref[0]
    lane = lax.broadcasted_iota(jnp.int32, (t_dec, 128), 1)
    for h in range(A_HEADS):
        qh = q[:, h * 128:(h + 1) * 128]
        q_s[h * rows:h * rows + t_dec, :] = jnp.where(lane < A_QK_DIM, qh, 0.0)
        q_s[h * rows + t_dec:(h + 1) * rows, :] = jnp.where(lane >= A_QK_DIM, qh, 0.0)
    m_s[...] = jnp.full(m_s.shape, NEG_INF, F32)
    l_s[...] = jnp.zeros(l_s.shape, F32)
    acc_s[...] = jnp.zeros(acc_s.shape, F32)


def _attn_sample_update(q_s, m_s, l_s, acc_s, ks, vs, biases):
    qb = q_s[...].astype(BF16)
    ss = [_dot_nt(qb, kb) + bias for kb, bias in zip(ks, biases)]
    smax = functools.reduce(jnp.maximum, ss)
    m_prev = m_s[...]
    m_new = jnp.maximum(m_prev, jnp.max(smax, axis=1, keepdims=True))
    ps = [jnp.exp(s - m_new) for s in ss]
    alpha = jnp.exp(m_prev - m_new)
    l_s[...] = alpha * l_s[...] + jnp.sum(functools.reduce(jnp.add, ps), axis=1, keepdims=True)
    pv = functools.reduce(jnp.add, [_dot(p.astype(BF16), vb) for p, vb in zip(ps, vs)])
    acc_s[...] = alpha * acc_s[...] + pv
    m_s[...] = m_new


def _attn_sample_final(kn_ref, vn_ref, bias_ref, lamv_ref, subg_ref, o_ref, q_s, m_s, l_s, acc_s,
                       lam_init, t_dec):
    rows = 2 * t_dec
    nk = kn_ref.shape[1]
    _attn_sample_update(q_s, m_s, l_s, acc_s, [kn_ref[0].astype(BF16)], [vn_ref[0].astype(BF16)],
                        [bias_ref[2][:, :nk]])
    lam = _lam(lamv_ref, lam_init)
    on = acc_s[...] * (1.0 / l_s[...])
    for h in range(A_HEADS):
        o = on[h * rows:h * rows + t_dec] - lam * on[h * rows + t_dec:(h + 1) * rows]
        o_ref[0, :, h * 128:(h + 1) * 128] = _subln(o, subg_ref, lam_init).astype(o_ref.dtype)


def _ffn_attn_kernel(pt_ref, x_ref, g_ref, w1_ref, w2_ref, q_ref, kn_ref, vn_ref, bias_ref,
                     lamv_ref, subg_ref, *rest, lam_init, n_pg, t_dec):
    del pt_ref
    k_refs, v_refs = rest[:n_pg], rest[n_pg:2 * n_pg]
    o_ffn, o_att = rest[2 * n_pg:2 * n_pg + 2]
    h_s, acc_s, q_s, m_s, l_s, att_s = rest[2 * n_pg + 2:]
    j = pl.program_id(1)
    last = pl.num_programs(1) - 1

    @pl.when(j == 0)
    def _():
        _attn_sample_init(q_ref, q_s, m_s, l_s, att_s, t_dec)
        _ffn_init(x_ref, g_ref, h_s, acc_s)

    biases = [bias_ref[0]] * (n_pg - 1) + [jnp.where(j == last, bias_ref[1], bias_ref[0])]
    _attn_sample_update(q_s, m_s, l_s, att_s, [r[0, 0].astype(BF16) for r in k_refs],
                        [r[0, 0].astype(BF16) for r in v_refs], biases)
    _ffn_accumulate(w1_ref, w2_ref, h_s, acc_s)

    @pl.when(j == last)
    def _():
        _attn_sample_final(kn_ref, vn_ref, bias_ref, lamv_ref, subg_ref, o_att, q_s, m_s, l_s,
                           att_s, lam_init, t_dec)
        o_ffn[...] = x_ref[...] + acc_s[...]


def _ffn_attn(x, g, w1, w2, zrs3, kn2d, vn2d, cache_k2d, cache_v2d, page_table, bias_s, lamv,
              subg, layer, lam_init):
    m = x.shape[0]
    bd, t_dec, _ = zrs3.shape
    n_pages = page_table.shape[1]
    n_j = FFN_ATTN_STEPS
    assert m % bd == 0 and (m // bd) % 8 == 0 and n_pages % n_j == 0 and D_FF % n_j == 0
    tm, tf, n_pg = m // bd, D_FF // n_j, n_pages // n_j
    rows = 2 * t_dec * A_HEADS
    pcols = PAGE_SIZE * A_HEADS

    def page_spec(p):
        return pl.BlockSpec((1, 1, pcols, 128),
                            lambda i, j, pt: (layer, pt[i, j * n_pg + p], 0, 0))

    def per_seq(arr):
        return pl.BlockSpec((1,) + arr.shape[1:], lambda i, j, pt: (i, 0, 0))

    def const(shape):
        return pl.BlockSpec(shape, lambda i, j, pt: (0,) * len(shape))

    kern = functools.partial(_ffn_attn_kernel, lam_init=lam_init, n_pg=n_pg, t_dec=t_dec)
    grid_spec = pltpu.PrefetchScalarGridSpec(
        num_scalar_prefetch=1,
        grid=(bd, n_j),
        in_specs=[pl.BlockSpec((tm, D_MODEL), lambda i, j, pt: (i, 0)),
                  const((1, D_MODEL)),
                  pl.BlockSpec((D_MODEL, tf), lambda i, j, pt: (0, j)),
                  pl.BlockSpec((tf, D_MODEL), lambda i, j, pt: (j, 0)),
                  pl.BlockSpec((1, t_dec, A_WIDTH), lambda i, j, pt: (i, 0, 0)),
                  per_seq(kn2d), per_seq(vn2d),
                  const((3, rows, pcols)), const((4, A_QK_DIM)), const((1, A_V_DIM))]
                 + [page_spec(p) for p in range(n_pg)] * 2,
        out_specs=[pl.BlockSpec((tm, D_MODEL), lambda i, j, pt: (i, 0)),
                   pl.BlockSpec((1, t_dec, A_WIDTH), lambda i, j, pt: (i, 0, 0))],
        scratch_shapes=[pltpu.VMEM((tm, D_MODEL), BF16), pltpu.VMEM((tm, D_MODEL), F32),
                        pltpu.VMEM((rows, 128), F32), pltpu.VMEM((rows, 1), F32),
                        pltpu.VMEM((rows, 1), F32), pltpu.VMEM((rows, 128), F32)],
    )
    return pl.pallas_call(
        kern,
        grid_spec=grid_spec,
        out_shape=[jax.ShapeDtypeStruct((m, D_MODEL), F32),
                   jax.ShapeDtypeStruct((bd, t_dec, A_WIDTH), BF16)],
        compiler_params=_cparams(2),
        name="ffn_attn",
    )(page_table, x, g.reshape(1, D_MODEL), w1, w2, zrs3, kn2d, vn2d, bias_s, lamv, subg,
      *([cache_k2d] * n_pg), *([cache_v2d] * n_pg))


def _gelu(x):
    return 0.5 * x * (1.0 + jnp.tanh(math.sqrt(2.0 / math.pi) * (x + 0.044715 * (x * x * x))))


def _chunk_mlp_kernel(u_ref, v_ref, w_ref, bias_ref, g_ref, b_ref, y_ref, vb_ref):
    n_chunks = u_ref.shape[0] // B_CHUNK
    lane_grp = lax.broadcasted_iota(jnp.int32, (B_CHUNK, B_WIDTH), 1) // (B_WIDTH // B_GROUPS)
    for c in range(n_chunks):
        sl = slice(c * B_CHUNK, (c + 1) * B_CHUNK)
        gv = _gelu(v_ref[sl, :])
        xc = gv - jnp.mean(gv, axis=-1, keepdims=True)
        vb = xc * lax.rsqrt(jnp.mean(xc * xc, axis=-1, keepdims=True) + EPS) * g_ref[...] + b_ref[...]
        vb_ref[sl, :] = vb
        vbb = vb.astype(BF16)
        mixed = bias_ref[...]
        for g in range(B_GROUPS):
            mixed = mixed + jnp.where(lane_grp == g, _dot(w_ref[g], vbb), 0.0)
        y_ref[sl, :] = (_gelu(u_ref[sl, :]) * mixed).astype(y_ref.dtype)


def _chunk_mlp(zr, w_eff, bias_eff, ln_g, ln_b):
    m = zr.shape[0]
    tm = min(m, CHUNK_MLP_TM)
    return pl.pallas_call(
        _chunk_mlp_kernel,
        grid=(m // tm,),
        in_specs=[pl.BlockSpec((tm, B_WIDTH), lambda i: (i, COL_BU // B_WIDTH)),
                  pl.BlockSpec((tm, B_WIDTH), lambda i: (i, COL_BV // B_WIDTH)),
                  pl.BlockSpec((B_GROUPS, B_CHUNK, B_CHUNK), lambda i: (0, 0, 0)),
                  pl.BlockSpec((B_CHUNK, B_WIDTH), lambda i: (0, 0)),
                  pl.BlockSpec((1, B_WIDTH), lambda i: (0, 0)),
                  pl.BlockSpec((1, B_WIDTH), lambda i: (0, 0))],
        out_specs=[pl.BlockSpec((tm, B_WIDTH), lambda i: (i, 0)),
                   pl.BlockSpec((tm, B_WIDTH), lambda i: (i, 0))],
        out_shape=[jax.ShapeDtypeStruct((m, B_WIDTH), BF16),
                   jax.ShapeDtypeStruct((m, B_WIDTH), F32)],
        compiler_params=_cparams(1),
        name="chunk_mlp",
    )(zr, zr, w_eff, bias_eff, ln_g, ln_b)


def _pool_kernel(x_ref, w_ref, scale_ref, y_ref, *, prefix, pos0, seq_rows):
    x = x_ref[0]
    row = lax.broadcasted_iota(jnp.int32, x.shape, 0) % seq_rows
    grp = lax.broadcasted_iota(jnp.int32, x.shape, 1) // C_GROUP_DIM

    def shifted(a, k):
        return jnp.where(row >= k, pltpu.roll(a, k, 0), 0.0)

    sums = []
    acc = x
    for k in (1, 2, 4, 8):
        acc = acc + shifted(acc, k)
        sums.append(acc)
    total = sums[3]
    win = jnp.full(x.shape, POOL_WINDOWS[3], jnp.int32)
    for g in range(3):
        total = jnp.where(grp == g, sums[g], total)
        win = jnp.where(grp == g, POOL_WINDOWS[g], win)
    pos = pos0 + row - prefix
    cnt = jnp.clip(pos + 1, 1, win).astype(F32)
    d = total / cnt - x
    y = _dot(d.astype(BF16), w_ref[...]) * scale_ref[...]
    y_ref[0] = y.astype(y_ref.dtype)


def _pool_mix(xx, col_block, w_bd, scale, prefix, pos0, seq_rows):
    b, rows, _ = xx.shape
    kern = functools.partial(_pool_kernel, prefix=prefix, pos0=pos0, seq_rows=seq_rows)
    return pl.pallas_call(
        kern,
        grid=(b,),
        in_specs=[pl.BlockSpec((1, rows, C_WIDTH), lambda i: (i, 0, col_block)),
                  pl.BlockSpec((C_WIDTH, C_WIDTH), lambda i: (0, 0)),
                  pl.BlockSpec((1, C_WIDTH), lambda i: (0, 0))],
        out_specs=pl.BlockSpec((1, rows, C_WIDTH), lambda i: (i, 0, 0)),
        out_shape=jax.ShapeDtypeStruct((b, rows, C_WIDTH), BF16),
        compiler_params=_cparams(1),
        name="pool_mix",
    )(xx, w_bd, scale)


def _log_sigmoid(x):
    return jnp.minimum(x, 0.0) - jnp.log1p(jnp.exp(-jnp.abs(x)))


def _split3(x):
    def top8(a):
        bits = lax.bitcast_convert_type(a, jnp.int32) & jnp.int32(-65536)
        return lax.bitcast_convert_type(bits, F32)

    p1 = top8(x)
    r1 = x - p1
    p2 = top8(r1)
    return p1, p2, r1 - p2


def _pack3(pieces):
    return (pieces[0] + pltpu.roll(pieces[1], D_HEADS, 1) + pltpu.roll(pieces[2], 2 * D_HEADS, 1))


def _mlstm_kernel(q_ref, k_ref, v_ref, o_ref, gi_ref, gf_ref, gb_ref, ng_ref, sel_ref, bd_ref,
                  c0_ref, n0_ref, m0_ref, y_ref, c_out, n_out, m_out, c_s, n_s, m_s, *, t_valid):
    ci = pl.program_id(1)
    nb, chunk = q_ref.shape[0], q_ref.shape[1]
    hd = D_HEAD_DIM
    eye_h = (lax.broadcasted_iota(jnp.int32, (hd, hd), 0)
             == lax.broadcasted_iota(jnp.int32, (hd, hd), 1))

    @pl.when(ci == 0)
    def _():
        c_s[...] = jnp.zeros(c_s.shape, F32)
        n_s[...] = jnp.zeros(n_s.shape, F32)
        m_s[...] = m0_ref[...]
        for bi in range(nb):
            for h in range(D_HEADS):
                hs = slice(h * hd, (h + 1) * hd)
                c_s[bi, hs, hs] = c0_ref[bi, h]
                n_col = jnp.sum(jnp.where(eye_h, n0_ref[bi, h], 0.0), axis=1, keepdims=True)
                n_s[bi, hs, hs] = jnp.broadcast_to(n_col, (hd, hd))

    row = lax.broadcasted_iota(jnp.int32, (chunk, 128), 0)
    lane = lax.broadcasted_iota(jnp.int32, (chunk, 128), 1)
    head_lane = lane < D_HEADS
    grp = lax.broadcasted_iota(jnp.int32, (chunk, D_WIDTH), 1) // hd
    rr = lax.broadcasted_iota(jnp.int32, (chunk, chunk), 0)
    cc = lax.broadcasted_iota(jnp.int32, (chunk, chunk), 1)
    allowed = (cc <= rr) & (cc < t_valid)
    ones_w = jnp.ones((chunk, D_WIDTH), BF16)
    same_head = (lax.broadcasted_iota(jnp.int32, (D_WIDTH, D_WIDTH), 0) // hd
                 == lax.broadcasted_iota(jnp.int32, (D_WIDTH, D_WIDTH), 1) // hd)
    pick = [(lane % D_HEADS == h) & (lane < 6 * D_HEADS) for h in range(D_HEADS)]
    in_head = [grp == h for h in range(D_HEADS)]
    lower_ones = jnp.where(lane < 3 * D_HEADS, 1.0, 0.0)
    upper_ones = jnp.where((lane >= 3 * D_HEADS) & (lane < 6 * D_HEADS), 1.0, 0.0)
    head_one = [jnp.where(ih, 1.0, 0.0).astype(BF16) for ih in in_head]

    for bi in range(nb):
        gi = gi_ref[bi] + gb_ref[0:1, :]
        lf = _log_sigmoid(gf_ref[bi] + gb_ref[1:2, :])
        if t_valid < chunk:
            gi = jnp.where(row < t_valid, gi, NEG_INF)
            lf = jnp.where(row < t_valid, lf, 0.0)
        b = lf
        k = 1
        while k < chunk:
            b = b + jnp.where(row >= k, pltpu.roll(b, k, 0), 0.0)
            k *= 2
        u = gi - b
        cm = u
        k = 1
        while k < chunk:
            cm = jnp.maximum(cm, jnp.where(row >= k, pltpu.roll(cm, k, 0), NEG_INF))
            k *= 2
        m_prev = m_s[bi]
        big_m = jnp.maximum(m_prev, cm)
        m_last = big_m[chunk - 1:chunk, :]
        winter = jnp.exp(m_prev - big_m)
        emt = jnp.exp(jnp.minimum(-(b + big_m), EMT_CAP))
        ws = jnp.exp(u - m_last)
        m_s[bi] = b[chunk - 1:chunk, :] + m_last

        def per_head_lanes(z):
            packed = _pack3(_split3(jnp.where(head_lane, z, 0.0)))
            return _dot(packed.astype(BF16), sel_ref[...])

        winter_r, emt_r, ws_r = per_head_lanes(winter), per_head_lanes(emt), per_head_lanes(ws)
        decay_r = winter_r[chunk - 1:chunk, :]

        q = q_ref[bi]
        qb = q.astype(BF16)
        kf = k_ref[bi] * (hd ** -0.5)
        kb = kf.astype(BF16)
        vf = v_ref[bi]
        vb = vf.astype(BF16)
        u_fin = jnp.where(head_lane & (row < t_valid), u, 0.0)
        y_side = (_pack3(_split3(u_fin)) + upper_ones).astype(BF16)
        x_all = lower_ones + pltpu.roll(_pack3(_split3(jnp.where(head_lane, -big_m, 0.0))),
                                        3 * D_HEADS, 1)
        acc = jnp.zeros((chunk, 2 * D_WIDTH), F32)
        for h in range(D_HEADS):
            x_side = jnp.where(pick[h], x_all, 0.0).astype(BF16)
            expo = jnp.where(allowed, _dot_nt(x_side, y_side), NEG_INF)
            qk = _dot_nt(jnp.where(in_head[h], q, 0.0).astype(BF16), kb) * jnp.exp(expo)
            rhs = jnp.concatenate([jnp.where(in_head[h], vf, 0.0).astype(BF16), head_one[h]], axis=1)
            acc = acc + _dot(qk.astype(BF16), rhs)
        state = jnp.concatenate([c_s[bi].astype(BF16), n_s[bi].astype(BF16)], axis=1)
        inter = _dot(qb, state)
        num = winter_r * inter[:, :D_WIDTH] + acc[:, :D_WIDTH]
        den = winter_r * inter[:, D_WIDTH:] + acc[:, D_WIDTH:]
        hh = num / jnp.maximum(jnp.abs(den), emt_r)
        h2 = hh * hh
        hi = h2.astype(BF16)
        lo = (h2 - hi.astype(F32)).astype(BF16)
        ssq = _dot(hi, bd_ref[...]) + _dot(lo, bd_ref[...])
        y = hh * lax.rsqrt(ssq * (1.0 / hd) + EPS) * ng_ref[...]
        y_ref[bi] = (y * jax.nn.sigmoid(o_ref[bi])).astype(y_ref.dtype)
        kw = (ws_r * kf).astype(BF16)
        upd = _dot_tn(kw, jnp.concatenate([vb, ones_w], axis=1))
        c_s[bi] = decay_r * c_s[bi] + jnp.where(same_head, upd[:, :D_WIDTH], 0.0)
        n_s[bi] = decay_r * n_s[bi] + jnp.where(same_head, upd[:, D_WIDTH:], 0.0)

    @pl.when(ci == pl.num_programs(1) - 1)
    def _():
        m_out[...] = m_s[...]
        for bi in range(nb):
            for h in range(D_HEADS):
                hs = slice(h * hd, (h + 1) * hd)
                c_out[bi, h] = c_s[bi, hs, hs]
                n_out[bi, h] = jnp.sum(jnp.where(eye_h, n_s[bi, hs, hs], 0.0), axis=0, keepdims=True)


def _mlstm(src, col0, gate_block, chunk, t_valid, gate_bias, norm_g, c0, n0, m0):
    b, t, _ = src.shape
    nb = math.gcd(b, MLSTM_BATCH)
    hd = D_HEAD_DIM
    kern = functools.partial(_mlstm_kernel, t_valid=t_valid)
    head_of_lane = jnp.arange(D_WIDTH) // hd
    src_lane = jnp.arange(128)[:, None]
    sel3 = ((src_lane % D_HEADS == head_of_lane[None, :]) & (src_lane < 3 * D_HEADS)).astype(BF16)
    same_head = (head_of_lane[:, None] == head_of_lane[None, :]).astype(BF16)
    m0p = jnp.pad(m0, ((0, 0), (0, 0), (0, 128 - D_HEADS)))

    def col(cb, width=D_WIDTH):
        return pl.BlockSpec((nb, chunk, width), lambda bi, ci: (bi, ci, cb))

    def const(shape):
        return pl.BlockSpec(shape, lambda bi, ci: (0,) * len(shape))

    def per_seq(shape):
        return pl.BlockSpec((nb,) + shape, lambda bi, ci: (bi,) + (0,) * len(shape))

    y, c_new, n_new, m_new = pl.pallas_call(
        kern,
        grid=(b // nb, t // chunk),
        in_specs=[col(col0), col(col0 + 1), col(col0 + 2), col(col0 + 3),
                  col(gate_block, 128), col(gate_block + 1, 128),
                  const((2, 128)), const((1, D_WIDTH)), const((128, D_WIDTH)),
                  const((D_WIDTH, D_WIDTH)),
                  per_seq((D_HEADS, hd, hd)), per_seq((D_HEADS, 1, hd)), per_seq((1, 128))],
        out_specs=[col(0), per_seq((D_HEADS, hd, hd)), per_seq((D_HEADS, 1, hd)),
                   per_seq((1, 128))],
        out_shape=[jax.ShapeDtypeStruct((b, t, D_WIDTH), BF16),
                   jax.ShapeDtypeStruct((b, D_HEADS, hd, hd), F32),
                   jax.ShapeDtypeStruct((b, D_HEADS, 1, hd), F32),
                   jax.ShapeDtypeStruct((b, 1, 128), F32)],
        scratch_shapes=[pltpu.VMEM((nb, D_WIDTH, D_WIDTH), F32),
                        pltpu.VMEM((nb, D_WIDTH, D_WIDTH), F32),
                        pltpu.VMEM((nb, 1, 128), F32)],
        compiler_params=_cparams(2),
        name="mlstm",
    )(src, src, src, src, src, src, gate_bias, jnp.tile(norm_g, (1, D_HEADS)), sel3, same_head,
      c0, n0, m0p)
    return y, c_new, n_new, m_new[:, :, :D_HEADS]


def _merge_kernel(x_ref, g_ref, ya_ref, yb_ref, yc_ref, yd_ref, wa_ref, wb_ref, wc_ref, wd_ref,
                  wo_ref, o_ref):
    def gate(i):
        return g_ref[:, i * D_MODEL:(i + 1) * D_MODEL].astype(F32)

    merged = gate(0) * _dot(ya_ref[...], wa_ref[...])
    merged = merged + gate(1) * _dot(yb_ref[...], wb_ref[...])
    merged = merged + gate(2) * _dot(yc_ref[...], wc_ref[...])
    merged = merged + gate(3) * _dot(yd_ref[...], wd_ref[...])
    o_ref[...] = x_ref[...] + _dot(merged.astype(BF16), wo_ref[...])


def _merge(x, gates, ya, yb, yc, yd, wa, wb, wc, wd, wo):
    m = x.shape[0]
    tm = min(m, MERGE_TM)

    def rows(width):
        return pl.BlockSpec((tm, width), lambda i: (i, 0))

    def full(arr):
        return pl.BlockSpec(arr.shape, lambda i: (0, 0))

    return pl.pallas_call(
        _merge_kernel,
        grid=(m // tm,),
        in_specs=[rows(D_MODEL), rows(GZ_WIDTH), rows(A_WIDTH), rows(B_WIDTH), rows(C_WIDTH),
                  rows(D_WIDTH), full(wa), full(wb), full(wc), full(wd), full(wo)],
        out_specs=rows(D_MODEL),
        out_shape=jax.ShapeDtypeStruct((m, D_MODEL), F32),
        compiler_params=_cparams(1),
        name="merge",
    )(x, gates, ya, yb, yc, yd, wa, wb, wc, wd, wo)


def _ffn_init(x_ref, g_ref, h_s, acc_s):
    x = x_ref[...]
    y = x * lax.rsqrt(jnp.mean(x * x, axis=-1, keepdims=True) + EPS)
    h_s[...] = (y * g_ref[...]).astype(BF16)
    acc_s[...] = jnp.zeros(acc_s.shape, F32)


def _ffn_accumulate(w1_ref, w2_ref, h_s, acc_s):
    a = jnp.maximum(_dot(h_s[...], w1_ref[...]), 0.0)
    acc_s[...] += _dot((a * a).astype(BF16), w2_ref[...])


def _ffn_kernel(x_ref, g_ref, w1_ref, w2_ref, o_ref, h_s, acc_s):
    j = pl.program_id(1)

    @pl.when(j == 0)
    def _():
        _ffn_init(x_ref, g_ref, h_s, acc_s)

    _ffn_accumulate(w1_ref, w2_ref, h_s, acc_s)

    @pl.when(j == pl.num_programs(1) - 1)
    def _():
        o_ref[...] = x_ref[...] + acc_s[...]


def _ffn(x, g, w1, w2):
    m = x.shape[0]
    tm, tf = min(m, 1024), 1024
    return pl.pallas_call(
        _ffn_kernel,
        grid=(m // tm, D_FF // tf),
        in_specs=[pl.BlockSpec((tm, D_MODEL), lambda i, j: (i, 0)),
                  pl.BlockSpec((1, D_MODEL), lambda i, j: (0, 0)),
                  pl.BlockSpec((D_MODEL, tf), lambda i, j: (0, j)),
                  pl.BlockSpec((tf, D_MODEL), lambda i, j: (j, 0))],
        out_specs=pl.BlockSpec((tm, D_MODEL), lambda i, j: (i, 0)),
        out_shape=jax.ShapeDtypeStruct((m, D_MODEL), F32),
        scratch_shapes=[pltpu.VMEM((tm, D_MODEL), BF16), pltpu.VMEM((tm, D_MODEL), F32)],
        compiler_params=_cparams(2),
        name="ffn",
    )(x, g.reshape(1, D_MODEL), w1, w2)


def _layer_weights(p, l):
    w_in = p["w_in"][l]
    gate_pad = jnp.zeros((D_MODEL, 128 - D_HEADS), F32)
    wr = jnp.concatenate(
        [w_in[:, :3072], w_in[:, 3080:3336], w_in[:, 3072:3076], gate_pad,
         w_in[:, 3076:3080], gate_pad], axis=1).astype(BF16)
    wg = w_in[:, 3336:].astype(BF16)
    reps = A_WIDTH // A_QK_DIM
    gains = jnp.stack([jnp.tile(p["q_norm_g"][l], reps) * (A_QK_DIM ** -0.5),
                       jnp.tile(p["k_norm_g"][l], reps)]).reshape(2, 1, A_WIDTH)
    grp = jnp.arange(A_WIDTH) // A_QK_DIM
    ones_bd = (grp[:, None] == grp[None, :]).astype(BF16)
    lamv = jnp.stack([p["lam_q1"][l], p["lam_k1"][l], p["lam_q2"][l], p["lam_k2"][l]])
    gate_bias = jnp.pad(jnp.stack([p["d_i_bias"][l], p["d_f_bias"][l]]),
                        ((0, 0), (0, 128 - D_HEADS)))
    c_bd = jnp.zeros((C_WIDTH, C_WIDTH), F32)
    for g in range(4):
        sl = slice(g * C_GROUP_DIM, (g + 1) * C_GROUP_DIM)
        c_bd = c_bd.at[sl, sl].set(p["c_lin"][l][g])
    return dict(
        norm1_g=p["norm1_g"][l], norm2_g=p["norm2_g"][l], wr=wr, wg=wg, gains=gains,
        ones_bd=ones_bd, lamv=lamv, subg=p["subln_g"][l].reshape(1, A_V_DIM),
        b_ln_g=p["b_ln_g"][l].reshape(1, B_WIDTH), b_ln_b=p["b_ln_b"][l].reshape(1, B_WIDTH),
        b_ws=p["b_ws"][l], b_bias=p["b_bias"][l],
        c_bd=c_bd.astype(BF16), c_scale=p["c_scale"][l].reshape(1, C_WIDTH),
        gate_bias=gate_bias, d_norm_g=p["d_norm_g"][l].reshape(1, D_HEAD_DIM),
        w_pa=p["w_pa"][l].astype(BF16), w_pb=p["w_pb"][l].astype(BF16),
        w_pc=p["w_pc"][l].astype(BF16), w_pd=p["w_pd"][l].astype(BF16),
        w_out=p["w_out"][l].astype(BF16), w_ff1=p["w_ff1"][l].astype(BF16),
        w_ff2=p["w_ff2"][l].astype(BF16),
        lam_init=0.8 - 0.6 * math.exp(-0.3 * l), layer=l,
    )


def _chunk_weights(w, t):
    length = min(t, B_CHUNK)
    ws = jnp.tril(w["b_ws"][:, :length, :length])
    bias = jnp.transpose(w["b_bias"][:, :length])
    reps = B_CHUNK // length
    if reps > 1:
        eye = jnp.eye(reps, dtype=F32)
        ws = jax.vmap(lambda a: jnp.kron(eye, a))(ws)
        bias = jnp.tile(bias, (reps, 1))
    return ws.astype(BF16), jnp.repeat(bias, B_WIDTH // B_GROUPS, axis=1)


def _mix_and_merge(x2, w, gates, zr, ya, yc, yd, t):
    w_eff, bias_eff = _chunk_weights(w, t)
    yb, vb = _chunk_mlp(zr, w_eff, bias_eff, w["b_ln_g"], w["b_ln_b"])
    x2 = _merge(x2, gates, ya, yb, yc, yd, w["w_pa"], w["w_pb"], w["w_pc"], w["w_pd"], w["w_out"])
    return x2, vb


def _prompt_layer(x2, w, b, s, bias_p, tab_t, sample_attn):
    m = b * s
    gates, zr, k_rows, v_rows = _in_proj(x2, w["norm1_g"], w["wg"], w["wr"], w["gains"],
                                         w["ones_bd"])
    zr3 = zr.reshape(b, s, ZR_WIDTH)
    ya = _attn_prompt(zr3, bias_p, tab_t, w["lamv"], w["subg"], w["lam_init"]).reshape(m, A_WIDTH)
    yc = _pool_mix(zr3, COL_CX // C_WIDTH, w["c_bd"], w["c_scale"], 0, 0, s).reshape(m, C_WIDTH)
    hd = D_HEAD_DIM
    yd, c_new, n_new, m_new = _mlstm(
        zr3, COL_DQ // D_WIDTH, COL_DGI // 128, min(s, MLSTM_CHUNK), min(s, MLSTM_CHUNK),
        w["gate_bias"],
        w["d_norm_g"], jnp.zeros((b, D_HEADS, hd, hd), F32), jnp.zeros((b, D_HEADS, 1, hd), F32),
        jnp.zeros((b, 1, D_HEADS), F32))
    x2, _ = _mix_and_merge(x2, w, gates, zr, ya, yc, yd.reshape(m, D_WIDTH), s)
    x2, ya_sample = _ffn_attn(x2, w["norm2_g"], w["w_ff1"], w["w_ff2"], *sample_attn,
                              w["lamv"], w["subg"], w["layer"], w["lam_init"])
    outs = (k_rows.reshape(b, s, A_HEADS, 2 * A_QK_DIM), v_rows.reshape(b, s, A_HEADS, A_V_DIM),
            zr3[:, s - POOL_BUF:, COL_CX:COL_CX + C_WIDTH],
            c_new, n_new.reshape(b, D_HEADS, hd), m_new.reshape(b, D_HEADS))
    return x2, outs, ya_sample


def _sample_layer(x2, w, bd, t, proj, ya, past, pool0, c0, n0, m0):
    m = bd * t
    gates, zr, kn, vn = proj
    zr3 = zr.reshape(bd, t, ZR_WIDTH)
    ya = ya.reshape(m, A_WIDTH)
    cx = zr3[:, :, COL_CX:COL_CX + C_WIDTH]
    prefix = POOL_BUF + 1
    rows = -(-(prefix + t) // 8) * 8
    xx = jnp.concatenate([jnp.zeros((bd, 1, C_WIDTH), F32), pool0, cx,
                          jnp.zeros((bd, rows - prefix - t, C_WIDTH), F32)], axis=1)
    yc = _pool_mix(xx.reshape(1, bd * rows, C_WIDTH), 0, w["c_bd"], w["c_scale"], prefix, past, rows)
    yc = yc.reshape(bd, rows, C_WIDTH)[:, prefix:prefix + t].reshape(m, C_WIDTH)
    chunk = -(-t // SAMPLE_MLSTM_CHUNK) * SAMPLE_MLSTM_CHUNK
    dsrc = jnp.pad(zr3[:, :, COL_DQ:COL_DGF + 128], ((0, 0), (0, chunk - t), (0, 0)))
    hd = D_HEAD_DIM
    yd, c_new, n_new, m_new = _mlstm(
        dsrc, 0, (COL_DGI - COL_DQ) // 128, chunk, t, w["gate_bias"], w["d_norm_g"],
        c0, n0.reshape(bd, D_HEADS, 1, hd), m0.reshape(bd, 1, D_HEADS))
    yd = yd[:, :t].reshape(m, D_WIDTH)
    x2, vb = _mix_and_merge(x2, w, gates, zr, ya, yc, yd, t)
    x2 = _ffn(x2, w["norm2_g"], w["w_ff1"], w["w_ff2"])
    outs = (kn.reshape(bd, t, A_HEADS, 2 * A_QK_DIM), vn.reshape(bd, t, A_HEADS, A_V_DIM),
            vb.reshape(bd, t, B_WIDTH), jnp.concatenate([pool0, cx], axis=1)[:, -POOL_BUF:],
            c_new, n_new.reshape(bd, D_HEADS, hd), m_new.reshape(bd, D_HEADS))
    return x2, outs


def kernel(x_prompt, x_sample, cache_k, cache_v, page_table, state_pool, state_C, state_n, state_m, rel_bias, norm1_g, norm2_g, w_in, q_norm_g, k_norm_g, lam_q1, lam_k1, lam_q2, lam_k2, subln_g, b_ln_g, b_ln_b, b_ws, b_bias, c_lin, c_scale, d_i_bias, d_f_bias, d_norm_g, w_pa, w_pb, w_pc, w_pd, w_out, w_ff1, w_ff2):
    p = dict(norm1_g=norm1_g, norm2_g=norm2_g, w_in=w_in, q_norm_g=q_norm_g, k_norm_g=k_norm_g,
             lam_q1=lam_q1, lam_k1=lam_k1, lam_q2=lam_q2, lam_k2=lam_k2, subln_g=subln_g,
             b_ln_g=b_ln_g, b_ln_b=b_ln_b, b_ws=b_ws, b_bias=b_bias, c_lin=c_lin, c_scale=c_scale,
             d_i_bias=d_i_bias, d_f_bias=d_f_bias, d_norm_g=d_norm_g, w_pa=w_pa, w_pb=w_pb,
             w_pc=w_pc, w_pd=w_pd, w_out=w_out, w_ff1=w_ff1, w_ff2=w_ff2)
    depth = w_in.shape[0]
    bp, sp, _ = x_prompt.shape
    bd, td, _ = x_sample.shape
    n_phys = cache_k.shape[1]
    cache_k2d = cache_k.reshape(depth, n_phys, PAGE_SIZE * A_HEADS, 128)
    cache_v2d = cache_v.reshape(depth, n_phys, PAGE_SIZE * A_HEADS, 128)
    bias_p, bias_s = _bias_tiles(rel_bias, td)
    tab_t = rel_bias.T

    xp = x_prompt.reshape(bp * sp, D_MODEL)
    xs = x_sample.reshape(bd * td, D_MODEL)
    prompt_outs, sample_outs = [], []
    for l in range(depth):
        w = _layer_weights(p, l)
        proj = _in_proj(xs, w["norm1_g"], w["wg"], w["wr"], w["gains"], w["ones_bd"])
        sample_attn = (proj[1].reshape(bd, td, ZR_WIDTH),
                       proj[2].reshape(bd, td * A_HEADS, 128), proj[3].reshape(bd, td * A_HEADS, 128),
                       cache_k2d, cache_v2d, page_table, bias_s)
        xp, po, ya_sample = _prompt_layer(xp, w, bp, sp, bias_p, tab_t, sample_attn)
        xs, so = _sample_layer(xs, w, bd, td, proj, ya_sample, page_table.shape[1] * PAGE_SIZE,
                               state_pool[l], state_C[l], state_n[l], state_m[l])
        prompt_outs.append(po)
        sample_outs.append(so)

    def stack(outs, i):
        return jnp.stack([o[i] for o in outs])

    return (xp.reshape(bp, sp, D_MODEL), xs.reshape(bd, td, D_MODEL),
            stack(prompt_outs, 0), stack(prompt_outs, 1), stack(sample_outs, 0), stack(sample_outs, 1),
            stack(sample_outs, 2), stack(prompt_outs, 2), stack(sample_outs, 3),
            stack(prompt_outs, 3), stack(prompt_outs, 4), stack(prompt_outs, 5),
            stack(sample_outs, 4), stack(sample_outs, 5), stack(sample_outs, 6))
```
